```python
import math
import jax, jax.numpy as jnp
from jax import lax
import numpy as np

D_MODEL = 1024
BATCH = 2
SEQ = 16384
DEPTH = 2

CHUNK = 64
D_MIX = D_MODEL
ATTN_WIDTH = D_MIX // 2
HEAD_DIM = 64
N_Q_HEADS = ATTN_WIDTH // HEAD_DIM
N_KV_HEADS = 2
GQA_GROUP = N_Q_HEADS // N_KV_HEADS
WINDOW = 128
WINDOW_CHUNKS = WINDOW // CHUNK
BAND = (WINDOW_CHUNKS + 1) * CHUNK
NEG_INF = -1e30
LRU_WIDTH = D_MIX - ATTN_WIDTH
LRU_BLOCKS = 8
LRU_BLOCK = LRU_WIDTH // LRU_BLOCKS
CONV_WIDTH = 4
LRU_C = 8.0
N_BUCKETS = 32
MAX_DISTANCE = 128
Q_COLS = N_Q_HEADS * HEAD_DIM
KV_COLS = N_KV_HEADS * HEAD_DIM
IN_COLS = Q_COLS + 2 * KV_COLS + 2 * LRU_WIDTH
D_FF = 2816
N_EXPERTS = 8
TOP_K = 2
D_FF_EXPERT = 3584
N_DENSE = (DEPTH + 1) // 2
N_MOE = DEPTH // 2
DEEPNORM_ALPHA = (2 * DEPTH) ** 0.25
DEEPNORM_BETA = (8 * DEPTH) ** -0.25
LN_EPS = 1e-5
RMS_EPS = 1e-6

kernel_name = "hybrid_swa_sink_rglru_moe_deepnorm"


def layer_norm(x, g, b):
    xf = x.astype(jnp.float32)
    mu = xf.mean(-1, keepdims=True)
    var = jnp.square(xf - mu).mean(-1, keepdims=True)
    return ((xf - mu) * lax.rsqrt(var + LN_EPS) * g.astype(jnp.float32) + b.astype(jnp.float32)).astype(x.dtype)


def rms_norm(x, g):
    xf = x.astype(jnp.float32)
    return (xf * lax.rsqrt(jnp.mean(xf * xf, -1, keepdims=True) + RMS_EPS) * g.astype(jnp.float32)).astype(x.dtype)


def t5_relative_bucket(rel):
    half = N_BUCKETS // 2
    max_exact = half // 2
    ret = jnp.where(rel > 0, half, 0)
    n = jnp.abs(rel)
    large = max_exact + (jnp.log(jnp.maximum(n, 1).astype(jnp.float32) / max_exact)
                         / math.log(MAX_DISTANCE / max_exact) * (half - max_exact)).astype(jnp.int32)
    large = jnp.minimum(large, half - 1)
    return ret + jnp.where(n < max_exact, n, large)


def banded_position_bias(rel_bias):
    qi = jnp.arange(CHUNK)[:, None]
    kj = jnp.arange(BAND)[None, :]
    rel = (kj - WINDOW_CHUNKS * CHUNK) - qi
    return jnp.transpose(rel_bias[t5_relative_bucket(rel)], (2, 0, 1))


def sliding_window_sink_attention(q, k, v, sinks, pos_bias):
    b, s = q.shape[:2]
    nc = s // CHUNK
    qc = q.reshape(b, nc, CHUNK, N_KV_HEADS, GQA_GROUP, HEAD_DIM)

    def band(t):
        tp = jnp.pad(t, ((0, 0), (WINDOW_CHUNKS * CHUNK, 0), (0, 0), (0, 0)))
        tp = tp.reshape(b, nc + WINDOW_CHUNKS, CHUNK, N_KV_HEADS, HEAD_DIM)
        return jnp.concatenate([tp[:, w:w + nc] for w in range(WINDOW_CHUNKS + 1)], axis=2)

    kb, vb = band(k), band(v)
    logits = jnp.einsum("bcqhgd,bckhd->bchgqk", qc, kb,
                        preferred_element_type=jnp.float32) * (HEAD_DIM ** -0.5)
    logits = logits + pos_bias.astype(jnp.float32).reshape(N_KV_HEADS, GQA_GROUP, CHUNK, BAND)
    key_pos = (jnp.arange(nc)[:, None] - WINDOW_CHUNKS) * CHUNK + jnp.arange(BAND)[None, :]
    logits = jnp.where((key_pos >= 0)[None, :, None, None, None, :], logits, NEG_INF)
    sink = jnp.broadcast_to(sinks.astype(jnp.float32).reshape(1, 1, N_KV_HEADS, GQA_GROUP, 1, 1),
                            logits.shape[:-1] + (1,))
    probs = jax.nn.softmax(jnp.concatenate([logits, sink], axis=-1), axis=-1)[..., :BAND]
    out = jnp.einsum("bchgqk,bckhd->bcqhgd", probs.astype(v.dtype), vb)
    return out.reshape(b, s, N_Q_HEADS * HEAD_DIM)


def rg_lru_branch(xb, gb, conv_w, conv_b, wr, br, wi, bi, lam):
    xc = lax.conv_general_dilated(xb, conv_w[:, None, :], window_strides=(1,),
                                  padding=[(CONV_WIDTH - 1, 0)],
                                  dimension_numbers=("NWC", "WIO", "NWC"),
                                  feature_group_count=LRU_WIDTH) + conv_b
    b, s = xc.shape[:2]
    xblk = xc.reshape(b, s, LRU_BLOCKS, LRU_BLOCK)
    r = jax.nn.sigmoid(jnp.einsum("bsnd,nde->bsne", xblk, wr).reshape(b, s, LRU_WIDTH).astype(jnp.float32)
                       + br.astype(jnp.float32))
    i = jax.nn.sigmoid(jnp.einsum("bsnd,nde->bsne", xblk, wi).reshape(b, s, LRU_WIDTH).astype(jnp.float32)
                       + bi.astype(jnp.float32))
    log_a = -LRU_C * r * jax.nn.softplus(-lam.astype(jnp.float32))
    a = jnp.exp(log_a)
    u = jnp.sqrt(-jnp.expm1(2.0 * log_a)) * (i * xc.astype(jnp.float32))

    def combine(left, right):
        a_l, h_l = left
        a_r, h_r = right
        return a_l * a_r, a_r * h_l + h_r

    _, h = lax.associative_scan(combine, (a, u), axis=1)
    return (jax.nn.gelu(gb.astype(jnp.float32)) * h).astype(xb.dtype)


def hybrid_mixer(x, pos_bias, w_in, sinks, conv_w, conv_b, wr, br, wi, bi, lam, g_attn, g_lru, w_out):
    b, s, _ = x.shape
    z = x @ w_in
    q, k, v, xb, gb = jnp.split(z, [Q_COLS, Q_COLS + KV_COLS, Q_COLS + 2 * KV_COLS,
                                    Q_COLS + 2 * KV_COLS + LRU_WIDTH], axis=-1)
    y_attn = sliding_window_sink_attention(q.reshape(b, s, N_Q_HEADS, HEAD_DIM),
                                           k.reshape(b, s, N_KV_HEADS, HEAD_DIM),
                                           v.reshape(b, s, N_KV_HEADS, HEAD_DIM), sinks, pos_bias)
    y_lru = rg_lru_branch(xb, gb, conv_w, conv_b, wr, br, wi, bi, lam)
    y = jnp.concatenate([rms_norm(y_attn, g_attn), rms_norm(y_lru, g_lru)], axis=-1)
    return y @ w_out


def swiglu(x, w_gate, w_up, w_down):
    return (jax.nn.silu(x @ w_gate) * (x @ w_up)) @ w_down


def moe_swiglu(x, router_w, w_gate, w_up, w_down):
    b, s, d = x.shape
    xf = x.reshape(b * s, d)
    logits = (xf @ router_w).astype(jnp.float32)
    top_vals, top_idx = lax.top_k(logits, TOP_K)
    top_w = jax.nn.softmax(top_vals, axis=-1)
    comb = jnp.sum(jax.nn.one_hot(top_idx, N_EXPERTS, dtype=jnp.float32) * top_w[..., None], axis=1)
    out = jnp.zeros((b * s, d), jnp.float32)
    for e in range(N_EXPERTS):
        ye = swiglu(xf, w_gate[e], w_up[e], w_down[e]).astype(jnp.float32)
        out = out + comb[:, e:e + 1] * ye
    return out.astype(x.dtype).reshape(b, s, d)


def setup_inputs(seed: int = 0) -> dict:
    key = jax.random.key(seed)
    ks = jax.random.split(key, 26)
    nrm = lambda k, shape, scale: jax.random.normal(k, shape, jnp.float32) * scale
    u = jax.random.uniform(ks[10], (DEPTH, LRU_WIDTH), jnp.float32, minval=0.9, maxval=0.999)
    a = u ** (1.0 / LRU_C)
    lru_lambda = jnp.log(a) - jnp.log1p(-a)
    return {
        "x": nrm(ks[0], (BATCH, SEQ, D_MODEL), 1.0),
        "rel_bias": nrm(ks[1], (N_BUCKETS, N_Q_HEADS), 0.5),
        "w_in": nrm(ks[2], (DEPTH, D_MODEL, IN_COLS), D_MODEL ** -0.5),
        "attn_sinks": nrm(ks[3], (DEPTH, N_Q_HEADS), 1.0),
        "conv_w": nrm(ks[4], (DEPTH, CONV_WIDTH, LRU_WIDTH), CONV_WIDTH ** -0.5),
        "conv_b": nrm(ks[5], (DEPTH, LRU_WIDTH), 0.01),
        "gate_r_w": nrm(ks[6], (DEPTH, LRU_BLOCKS, LRU_BLOCK, LRU_BLOCK), LRU_BLOCK ** -0.5),
        "gate_r_b": nrm(ks[7], (DEPTH, LRU_WIDTH), 0.01),
        "gate_i_w": nrm(ks[8], (DEPTH, LRU_BLOCKS, LRU_BLOCK, LRU_BLOCK), LRU_BLOCK ** -0.5),
        "gate_i_b": nrm(ks[9], (DEPTH, LRU_WIDTH), 0.01),
        "lru_lambda": lru_lambda,
        "norm_attn_g": 1.0 + nrm(ks[11], (DEPTH, ATTN_WIDTH), 0.01),
        "norm_lru_g": 1.0 + nrm(ks[12], (DEPTH, LRU_WIDTH), 0.01),
        "w_out": nrm(ks[13], (DEPTH, D_MIX, D_MODEL), D_MIX ** -0.5 * DEEPNORM_BETA),
        "ln1_g": 1.0 + nrm(ks[14], (DEPTH, D_MODEL), 0.01),
        "ln1_b": nrm(ks[15], (DEPTH, D_MODEL), 0.01),
        "ffn_w_gate": nrm(ks[16], (N_DENSE, D_MODEL, D_FF), D_MODEL ** -0.5),
        "ffn_w_up": nrm(ks[17], (N_DENSE, D_MODEL, D_FF), D_MODEL ** -0.5),
        "ffn_w_down": nrm(ks[18], (N_DENSE, D_FF, D_MODEL), D_FF ** -0.5 * DEEPNORM_BETA),
        "router_w": nrm(ks[19], (N_MOE, D_MODEL, N_EXPERTS), D_MODEL ** -0.5),
        "exp_w_gate": nrm(ks[20], (N_MOE, N_EXPERTS, D_MODEL, D_FF_EXPERT), D_MODEL ** -0.5),
        "exp_w_up": nrm(ks[21], (N_MOE, N_EXPERTS, D_MODEL, D_FF_EXPERT), D_MODEL ** -0.5),
        "exp_w_down": nrm(ks[22], (N_MOE, N_EXPERTS, D_FF_EXPERT, D_MODEL), D_FF_EXPERT ** -0.5 * DEEPNORM_BETA),
        "ln2_g": 1.0 + nrm(ks[23], (DEPTH, D_MODEL), 0.01),
        "ln2_b": nrm(ks[24], (DEPTH, D_MODEL), 0.01),
    }


def reference(x, rel_bias, w_in, attn_sinks, conv_w, conv_b, gate_r_w, gate_r_b, gate_i_w, gate_i_b,
              lru_lambda, norm_attn_g, norm_lru_g, w_out, ln1_g, ln1_b, ffn_w_gate, ffn_w_up, ffn_w_down,
              router_w, exp_w_gate, exp_w_up, exp_w_down, ln2_g, ln2_b):
    pos_bias = banded_position_bias(rel_bias)
    for l in range(DEPTH):
        y = hybrid_mixer(x, pos_bias, w_in[l], attn_sinks[l], conv_w[l], conv_b[l],
                         gate_r_w[l], gate_r_b[l], gate_i_w[l], gate_i_b[l], lru_lambda[l],
                         norm_attn_g[l], norm_lru_g[l], w_out[l])
        x = layer_norm(DEEPNORM_ALPHA * x + y, ln1_g[l], ln1_b[l])
        if l % 2 == 0:
            f = swiglu(x, ffn_w_gate[l // 2], ffn_w_up[l // 2], ffn_w_down[l // 2])
        else:
            f = moe_swiglu(x, router_w[l // 2], exp_w_gate[l // 2], exp_w_up[l // 2], exp_w_down[l // 2])
        x = layer_norm(DEEPNORM_ALPHA * x + f, ln2_g[l], ln2_b[l])
    return x
```

```python
import functools
import math

import jax
import jax.numpy as jnp
from jax import lax
from jax.experimental import pallas as pl
from jax.experimental.pallas import tpu as pltpu

F32 = jnp.float32
BF16 = jnp.bfloat16

D_MODEL = 1024
CHUNK = 64
HEAD_DIM = 64
N_Q_HEADS = 8
N_KV_HEADS = 2
ATTN_WIDTH = N_Q_HEADS * HEAD_DIM
KV_COLS = N_KV_HEADS * HEAD_DIM
WINDOW_CHUNKS = 2
LRU_WIDTH = 512
LRU_BLOCKS = 8
CONV_WIDTH = 4
LRU_C = 8.0
N_BUCKETS = 32
MAX_DISTANCE = 128
N_EXPERTS = 8
NEG_INF = -1e30
LN_EPS = 1e-5
RMS_EPS = 1e-6

LANES = 128
SUBLANES = 8
VMEM_LIMIT = 56 * 1024 * 1024

QKV_COLS = ATTN_WIDTH + 4 * KV_COLS
Q_BLOCK = 2 * CHUNK
K_BLOCK = (WINDOW_CHUNKS + 2) * CHUNK
ATTN_TILE = 512
LRU_TILE = 512
TOKEN_TILE = 512
MOE_TILE = 1024
MOE_F_TILE = 896
DMA_TILE = 256


def _params(sem, **kw):
    return pltpu.CompilerParams(dimension_semantics=sem, vmem_limit_bytes=VMEM_LIMIT, **kw)


def _resident(shape):
    nd = len(shape)
    return pl.BlockSpec(shape, lambda *_: (0,) * nd, pipeline_mode=pl.Buffered(1))


def _layer_norm(x, g, b):
    mu = jnp.mean(x, axis=-1, keepdims=True)
    xc = x - mu
    var = jnp.mean(xc * xc, axis=-1, keepdims=True)
    return xc * lax.rsqrt(var + LN_EPS) * g + b


def _rms_norm(x, g):
    return x * lax.rsqrt(jnp.mean(x * x, axis=-1, keepdims=True) + RMS_EPS) * g


def _inproj_kernel(x_ref, w_ref, qkv_ref, lru_ref):
    z = jnp.dot(x_ref[...].astype(BF16), w_ref[...], preferred_element_type=F32)
    qkv_ref[...] = z[:, :QKV_COLS].astype(BF16)
    lru_ref[...] = z[:, QKV_COLS:]


def _inproj(x2d, w):
    n = x2d.shape[0]
    cols = w.shape[1]
    return pl.pallas_call(
        _inproj_kernel,
        grid=(n // TOKEN_TILE,),
        in_specs=[pl.BlockSpec((TOKEN_TILE, D_MODEL), lambda i: (i, 0)), _resident((D_MODEL, cols))],
        out_specs=[pl.BlockSpec((TOKEN_TILE, QKV_COLS), lambda i: (i, 0)),
                   pl.BlockSpec((TOKEN_TILE, 2 * LRU_WIDTH), lambda i: (i, 0))],
        out_shape=[jax.ShapeDtypeStruct((n, QKV_COLS), BF16),
                   jax.ShapeDtypeStruct((n, 2 * LRU_WIDTH), F32)],
        compiler_params=_params(("parallel",)),
        name="inproj",
    )(x2d, w)


def _attn_kernel(sink_ref, q_ref, k_ref, ks_ref, v_ref, vs_ref, kp_ref, ksp_ref, vp_ref, vsp_ref,
                 bias_ref, g_ref, o_ref, kk_ref, vv_ref, acc_ref):
    step = pl.program_id(1)
    hist = Q_BLOCK
    lane = lax.broadcasted_iota(jnp.int32, (1, LANES), 1)
    lo = lane < HEAD_DIM
    zero = jnp.zeros((), BF16)

    def fill(dst, plain_prev, plain, swap_prev, swap):
        for rows, plain_ref, swap_ref in ((slice(0, hist), plain_prev, swap_prev),
                                          (slice(hist, hist + ATTN_TILE), plain, swap)):
            a = plain_ref[...]
            s = swap_ref[...]
            dst[0, rows, :] = jnp.where(lo, a, zero)
            dst[1, rows, :] = jnp.where(lo, zero, s)
            dst[2, rows, :] = jnp.where(lo, s, zero)
            dst[3, rows, :] = jnp.where(lo, zero, a)

    fill(kk_ref, kp_ref, k_ref, ksp_ref, ks_ref)
    fill(vv_ref, vp_ref, v_ref, vsp_ref, vs_ref)

    col = lax.broadcasted_iota(jnp.int32, (1, K_BLOCK), 1)
    dn = (((1,), (1,)), ((), ()))

    def sub_block(s, carry):
        q0 = pl.multiple_of(s * Q_BLOCK, Q_BLOCK)
        before_start = jnp.logical_and(jnp.logical_and(step == 0, s == 0), col < hist)
        start_mask = jnp.where(before_start, NEG_INF, 0.0).astype(F32)
        for h in range(N_KV_HEADS):
            q2 = jnp.concatenate(
                [q_ref[pl.ds(q0, Q_BLOCK), pl.ds((2 * h + p) * LANES, LANES)] for p in range(2)], axis=0)
            probs = []
            inv_den = []
            for half in range(2):
                kk = kk_ref[2 * h + half, pl.ds(q0, K_BLOCK), :]
                sc = lax.dot_general(q2, kk, dn, preferred_element_type=F32)
                for p in range(2):
                    head = 4 * h + 2 * p + half
                    logit = sc[p * Q_BLOCK:(p + 1) * Q_BLOCK] + bias_ref[head] + start_mask
                    sink = sink_ref[head]
                    m = jnp.maximum(jnp.max(logit, axis=-1, keepdims=True), sink)
                    e = jnp.exp(logit - m)
                    den = jnp.sum(e, axis=-1, keepdims=True) + jnp.exp(sink - m)
                    probs.append((p, half, e.astype(BF16)))
                    inv_den.append((p, half, 1.0 / den))
            pmap = {(p, half): e for p, half, e in probs}
            dmap = {(p, half): d for p, half, d in inv_den}
            lhs = jnp.concatenate(
                [jnp.concatenate([pmap[(p, 0)], pmap[(p, 1)]], axis=1) for p in range(2)], axis=0)
            rhs = jnp.concatenate([vv_ref[2 * h, pl.ds(q0, K_BLOCK), :],
                                   vv_ref[2 * h + 1, pl.ds(q0, K_BLOCK), :]], axis=0)
            out = jnp.dot(lhs, rhs, preferred_element_type=F32)
            for p in range(2):
                scale = jnp.where(lo, dmap[(p, 0)], dmap[(p, 1)])
                acc_ref[pl.ds(q0, Q_BLOCK), pl.ds((2 * h + p) * LANES, LANES)] = (
                    out[p * Q_BLOCK:(p + 1) * Q_BLOCK] * scale)
        return carry

    lax.fori_loop(0, ATTN_TILE // Q_BLOCK, sub_block, 0)
    o_ref[...] = _rms_norm(acc_ref[...], g_ref[...]).astype(BF16)


def _attention(qkv, sinks, bias2, g_attn, batch, seq):
    n = qkv.shape[0]
    steps = seq // ATTN_TILE
    ratio = ATTN_TILE // Q_BLOCK
    qcb = ATTN_WIDTH // LANES

    def cur(cb):
        return pl.BlockSpec((ATTN_TILE, LANES), lambda b, i: (b * steps + i, cb))

    def prev(cb):
        return pl.BlockSpec((Q_BLOCK, LANES),
                            lambda b, i: (b * steps * ratio + jnp.maximum(i * ratio - 1, 0), cb))

    return pl.pallas_call(
        _attn_kernel,
        grid=(batch, steps),
        in_specs=[pl.BlockSpec(memory_space=pltpu.SMEM),
                  pl.BlockSpec((ATTN_TILE, ATTN_WIDTH), lambda b, i: (b * steps + i, 0)),
                  cur(qcb), cur(qcb + 1), cur(qcb + 2), cur(qcb + 3),
                  prev(qcb), prev(qcb + 1), prev(qcb + 2), prev(qcb + 3),
                  _resident((N_Q_HEADS, Q_BLOCK, K_BLOCK)),
                  _resident((1, ATTN_WIDTH))],
        out_specs=pl.BlockSpec((ATTN_TILE, ATTN_WIDTH), lambda b, i: (b * steps + i, 0)),
        out_shape=jax.ShapeDtypeStruct((n, ATTN_WIDTH), BF16),
        scratch_shapes=[pltpu.VMEM((4, Q_BLOCK + ATTN_TILE, LANES), BF16),
                        pltpu.VMEM((4, Q_BLOCK + ATTN_TILE, LANES), BF16),
                        pltpu.VMEM((ATTN_TILE, ATTN_WIDTH), F32)],
        compiler_params=_params(("parallel", "arbitrary")),
        name="attention",
    )(sinks, qkv, qkv, qkv, qkv, qkv, qkv, qkv, qkv, qkv, bias2, g_attn)


def _lru_kernel(x_ref, gate_ref, prev_ref, cw_ref, cb_ref, wg_ref, bg_ref, lam_ref, g_ref,
                o_ref, h_ref):
    step = pl.program_id(1)

    @pl.when(step == 0)
    def _():
        h_ref[...] = jnp.zeros_like(h_ref)

    x = x_ref[...]
    t = x.shape[0]
    rows = lax.broadcasted_iota(jnp.int32, (t, 1), 0)
    rows8 = lax.broadcasted_iota(jnp.int32, (SUBLANES, 1), 0)
    prev = jnp.where(step == 0, 0.0, prev_ref[...])

    xc = x * cw_ref[CONV_WIDTH - 1:CONV_WIDTH, :] + cb_ref[...]
    for k in range(1, CONV_WIDTH):
        sh = pltpu.roll(x, k, 0)
        head = jnp.where(rows8 < k, pltpu.roll(prev, k, 0), sh[:SUBLANES])
        sh = jnp.concatenate([head, sh[SUBLANES:]], axis=0)
        xc = xc + sh * cw_ref[CONV_WIDTH - 1 - k:CONV_WIDTH - k, :]

    gates = jnp.dot(xc.astype(BF16), wg_ref[...], preferred_element_type=F32) + bg_ref[...]
    r = jax.nn.sigmoid(gates[:, :LRU_WIDTH])
    i = jax.nn.sigmoid(gates[:, LRU_WIDTH:])
    neg_lam = -lam_ref[...]
    softplus = jnp.maximum(neg_lam, 0.0) + jnp.log1p(jnp.exp(-jnp.abs(neg_lam)))
    log_a = (-LRU_C) * r * softplus
    a = jnp.exp(log_a)
    h = jnp.sqrt(jnp.tanh(-log_a) * (1.0 + a * a)) * (i * xc)

    d = 1
    while d < t:
        keep = rows >= d
        a_sh = jnp.where(keep, pltpu.roll(a, d, 0), 1.0)
        h_sh = jnp.where(keep, pltpu.roll(h, d, 0), 0.0)
        h = a * h_sh + h
        a = a * a_sh
        d *= 2
    h = h + a * h_ref[...]
    h_ref[...] = h[t - 1:t, :]

    y = jax.nn.gelu(gate_ref[...]) * h
    o_ref[...] = _rms_norm(y, g_ref[...]).astype(BF16)


def _rg_lru(lru, conv_w, conv_b, w_gates, b_gates, lam, g_lru, batch, seq):
    n = lru.shape[0]
    steps = seq // LRU_TILE
    ratio = LRU_TILE // SUBLANES
    return pl.pallas_call(
        _lru_kernel,
        grid=(batch, steps),
        in_specs=[pl.BlockSpec((LRU_TILE, LRU_WIDTH), lambda b, i: (b * steps + i, 0)),
                  pl.BlockSpec((LRU_TILE, LRU_WIDTH), lambda b, i: (b * steps + i, 1)),
                  pl.BlockSpec((SUBLANES, LRU_WIDTH),
                               lambda b, i: (b * steps * ratio + jnp.maximum(i * ratio - 1, 0), 0)),
                  _resident((CONV_WIDTH, LRU_WIDTH)),
                  _resident((1, LRU_WIDTH)),
                  _resident((LRU_WIDTH, 2 * LRU_WIDTH)),
                  _resident((1, 2 * LRU_WIDTH)),
                  _resident((1, LRU_WIDTH)),
                  _resident((1, LRU_WIDTH))],
        out_specs=pl.BlockSpec((LRU_TILE, LRU_WIDTH), lambda b, i: (b * steps + i, 0)),
        out_shape=jax.ShapeDtypeStruct((n, LRU_WIDTH), BF16),
        scratch_shapes=[pltpu.VMEM((1, LRU_WIDTH), F32)],
        compiler_params=_params(("arbitrary", "arbitrary")),
        name="rg_lru",
    )(lru, lru, lru, conv_w, conv_b, w_gates, b_gates, lam, g_lru)


def _mix_ln1(ya_ref, yl_ref, x_ref, wo_ref, g1_ref, b1_ref, alpha):
    y = jnp.dot(ya_ref[...], wo_ref[:ATTN_WIDTH, :], preferred_element_type=F32)
    y = y + jnp.dot(yl_ref[...], wo_ref[ATTN_WIDTH:, :], preferred_element_type=F32)
    return _layer_norm(alpha * x_ref[...] + y, g1_ref[...], b1_ref[...])


def _dense_block_kernel(ya_ref, yl_ref, x_ref, wo_ref, g1_ref, b1_ref, wg_ref, wu_ref, wd_ref,
                        g2_ref, b2_ref, o_ref, *, alpha, f_chunk):
    x1 = _mix_ln1(ya_ref, yl_ref, x_ref, wo_ref, g1_ref, b1_ref, alpha)
    xb = x1.astype(BF16)
    f = None
    for c in range(wg_ref.shape[1] // f_chunk):
        cols = slice(c * f_chunk, (c + 1) * f_chunk)
        g = jnp.dot(xb, wg_ref[:, cols], preferred_element_type=F32)
        u = jnp.dot(xb, wu_ref[:, cols], preferred_element_type=F32)
        hid = (jax.nn.silu(g) * u).astype(BF16)
        part = jnp.dot(hid, wd_ref[cols, :], preferred_element_type=F32)
        f = part if f is None else f + part
    o_ref[...] = _layer_norm(alpha * x1 + f, g2_ref[...], b2_ref[...])


def _dense_block(ya, yl, x2d, wo, g1, b1, wg, wu, wd, g2, b2, alpha):
    n = x2d.shape[0]
    d_ff = wg.shape[1]
    tile = lambda w: pl.BlockSpec((TOKEN_TILE, w), lambda i: (i, 0))
    return pl.pallas_call(
        functools.partial(_dense_block_kernel, alpha=alpha, f_chunk=d_ff // 2),
        grid=(n // TOKEN_TILE,),
        in_specs=[tile(ATTN_WIDTH), tile(LRU_WIDTH), tile(D_MODEL),
                  _resident((D_MODEL, D_MODEL)), _resident((1, D_MODEL)), _resident((1, D_MODEL)),
                  _resident((D_MODEL, d_ff)), _resident((D_MODEL, d_ff)), _resident((d_ff, D_MODEL)),
                  _resident((1, D_MODEL)), _resident((1, D_MODEL))],
        out_specs=tile(D_MODEL),
        out_shape=jax.ShapeDtypeStruct((n, D_MODEL), F32),
        compiler_params=_params(("parallel",)),
        name="dense_block",
    )(ya, yl, x2d, wo, g1, b1, wg, wu, wd, g2, b2)


_L_IDX, _L_W, _L_RANK = 0, 2, 4


def _router_kernel(ya_ref, yl_ref, x_ref, wo_ref, g1_ref, b1_ref, rw_ref,
                   x1_ref, route_ref, count_ref, carry_ref, *, alpha):
    @pl.when(pl.program_id(0) == 0)
    def _():
        carry_ref[...] = jnp.zeros_like(carry_ref)

    x1 = _mix_ln1(ya_ref, yl_ref, x_ref, wo_ref, g1_ref, b1_ref, alpha)
    x1_ref[...] = x1
    t = x1.shape[0]
    logits = jnp.dot(x1, rw_ref[...], preferred_element_type=F32, precision=lax.Precision.HIGHEST)
    lane = lax.broadcasted_iota(jnp.int32, (t, LANES), 1).astype(F32)
    lowest = jnp.finfo(F32).min
    l1 = jnp.where(lane < N_EXPERTS, logits, lowest)
    m1 = jnp.max(l1, axis=-1, keepdims=True)
    i1 = jnp.min(jnp.where(l1 == m1, lane, float(LANES)), axis=-1, keepdims=True)
    l2 = jnp.where(lane == i1, lowest, l1)
    m2 = jnp.max(l2, axis=-1, keepdims=True)
    i2 = jnp.min(jnp.where(l2 == m2, lane, float(LANES)), axis=-1, keepdims=True)
    e = jnp.exp(m2 - m1)
    w1 = 1.0 / (1.0 + e)
    w2 = e / (1.0 + e)

    chosen = jnp.logical_or(lane == i1, lane == i2)
    sel = jnp.where(chosen, 1.0, 0.0).astype(BF16)
    r_i = lax.broadcasted_iota(jnp.int32, (t, t), 0)
    c_i = lax.broadcasted_iota(jnp.int32, (t, t), 1)
    earlier = jnp.where(c_i < r_i, 1.0, 0.0).astype(BF16)
    before = jnp.dot(earlier, sel, preferred_element_type=F32) + carry_ref[...]
    rank1 = jnp.sum(jnp.where(lane == i1, before, 0.0), axis=-1, keepdims=True)
    rank2 = jnp.sum(jnp.where(lane == i2, before, 0.0), axis=-1, keepdims=True)
    total = carry_ref[...] + jnp.sum(sel.astype(F32), axis=0, keepdims=True)
    carry_ref[...] = total
    count_ref[...] = total

    route = jnp.where(lane == _L_IDX, i1, 0.0)
    route = jnp.where(lane == _L_IDX + 1, i2, route)
    route = jnp.where(lane == _L_W, w1, route)
    route = jnp.where(lane == _L_W + 1, w2, route)
    route = jnp.where(lane == _L_RANK, rank1, route)
    route = jnp.where(lane == _L_RANK + 1, rank2, route)
    route_ref[...] = route


def _router_block(ya, yl, x2d, wo, g1, b1, rw, alpha):
    n = x2d.shape[0]
    tile = lambda w: pl.BlockSpec((TOKEN_TILE, w), lambda i: (i, 0))
    return pl.pallas_call(
        functools.partial(_router_kernel, alpha=alpha),
        grid=(n // TOKEN_TILE,),
        in_specs=[tile(ATTN_WIDTH), tile(LRU_WIDTH), tile(D_MODEL),
                  _resident((D_MODEL, D_MODEL)), _resident((1, D_MODEL)), _resident((1, D_MODEL)),
                  _resident((D_MODEL, LANES))],
        out_specs=[tile(D_MODEL), tile(LANES), pl.BlockSpec((1, LANES), lambda i: (0, 0))],
        out_shape=[jax.ShapeDtypeStruct((n, D_MODEL), F32),
                   jax.ShapeDtypeStruct((n, LANES), F32),
                   jax.ShapeDtypeStruct((1, LANES), F32)],
        scratch_shapes=[pltpu.VMEM((1, LANES), F32)],
        compiler_params=_params(("arbitrary",)),
        name="router_block",
    )(ya, yl, x2d, wo, g1, b1, rw)


def _row_copy(src_ref, src_row, dst_ref, dst_row, sem):
    return pltpu.make_async_copy(src_ref.at[pl.ds(src_row, 1)], dst_ref.at[pl.ds(dst_row, 1)], sem)


def _dispatch_kernel(pos_ref, x_ref, buf_in_ref, xs_ref, sem):
    del buf_in_ref

    def issue(t, carry):
        for k in range(2):
            _row_copy(x_ref, t, xs_ref, pos_ref[2 * t + k], sem).start()
        return carry

    lax.fori_loop(0, DMA_TILE, issue, 0)
    for k in range(2):
        pltpu.make_async_copy(x_ref, xs_ref.at[pl.ds(0, DMA_TILE)], sem).wait()


def _dispatch(pos, x1, sorted_rows):
    n = x1.shape[0]
    zeros = jnp.zeros((sorted_rows, D_MODEL), F32)
    return pl.pallas_call(
        _dispatch_kernel,
        grid=(n // DMA_TILE,),
        in_specs=[pl.BlockSpec((2 * DMA_TILE,), lambda i: (i,), memory_space=pltpu.SMEM),
                  pl.BlockSpec((DMA_TILE, D_MODEL), lambda i: (i, 0)),
                  pl.BlockSpec(memory_space=pl.ANY)],
        out_specs=pl.BlockSpec(memory_space=pl.ANY),
        out_shape=jax.ShapeDtypeStruct((sorted_rows, D_MODEL), F32),
        scratch_shapes=[pltpu.SemaphoreType.DMA(())],
        input_output_aliases={2: 0},
        compiler_params=_params(("arbitrary",), disable_bounds_checks=True),
        name="moe_dispatch",
    )(pos, x1, zeros)


def _experts_kernel(tile_expert_ref, n_used_ref, x_ref, wg_ref, wu_ref, wd_ref, y_ref, xb_ref):
    del tile_expert_ref
    i = pl.program_id(0)
    j = pl.program_id(1)

    @pl.when(i < n_used_ref[0])
    def _():
        @pl.when(j == 0)
        def _():
            xb_ref[...] = x_ref[...].astype(BF16)

        xb = xb_ref[...]
        g = jnp.dot(xb, wg_ref[...], preferred_element_type=F32)
        u = jnp.dot(xb, wu_ref[...], preferred_element_type=F32)
        hid = (jax.nn.silu(g) * u).astype(BF16)
        part = jnp.dot(hid, wd_ref[...], preferred_element_type=F32)

        @pl.when(j == 0)
        def _():
            y_ref[...] = part

        @pl.when(j > 0)
        def _():
            y_ref[...] += part

    @pl.when(jnp.logical_and(i >= n_used_ref[0], j == 0))
    def _():
        y_ref[...] = jnp.zeros_like(y_ref)


def _experts(tile_expert, n_used, xs, wg, wu, wd):
    rows = xs.shape[0]
    d_ff = wg.shape[2]
    n_tiles = rows // MOE_TILE
    n_f = d_ff // MOE_F_TILE

    def row_map(i, j, te, nu):
        return (jnp.minimum(i, nu[0] - 1), 0)

    def f_of(i, j, nu):
        return jnp.where(i < nu[0], j, n_f - 1)

    grid_spec = pltpu.PrefetchScalarGridSpec(
        num_scalar_prefetch=2,
        grid=(n_tiles, n_f),
        in_specs=[pl.BlockSpec((MOE_TILE, D_MODEL), row_map),
                  pl.BlockSpec((None, D_MODEL, MOE_F_TILE), lambda i, j, te, nu: (te[i], 0, f_of(i, j, nu))),
                  pl.BlockSpec((None, D_MODEL, MOE_F_TILE), lambda i, j, te, nu: (te[i], 0, f_of(i, j, nu))),
                  pl.BlockSpec((None, MOE_F_TILE, D_MODEL), lambda i, j, te, nu: (te[i], f_of(i, j, nu), 0))],
        out_specs=pl.BlockSpec((MOE_TILE, D_MODEL), lambda i, j, te, nu: (i, 0)),
        scratch_shapes=[pltpu.VMEM((MOE_TILE, D_MODEL), BF16)],
    )
    return pl.pallas_call(
        _experts_kernel,
        grid_spec=grid_spec,
        out_shape=jax.ShapeDtypeStruct((rows, D_MODEL), F32),
        compiler_params=_params(("arbitrary", "arbitrary")),
        name="moe_experts",
    )(tile_expert, n_used, xs, wg, wu, wd)


def _combine_kernel(pos_ref, ys_ref, x1_ref, route_ref, g2_ref, b2_ref, o_ref, buf_ref, sem, *, alpha):
    def issue(t, carry):
        for k in range(2):
            _row_copy(ys_ref, pos_ref[2 * t + k], buf_ref.at[k], t, sem).start()
        return carry

    lax.fori_loop(0, DMA_TILE, issue, 0)
    for k in range(2):
        pltpu.make_async_copy(ys_ref.at[pl.ds(0, DMA_TILE)], buf_ref.at[k], sem).wait()
    route = route_ref[...]
    f = route[:, _L_W:_L_W + 1] * buf_ref[0] + route[:, _L_W + 1:_L_W + 2] * buf_ref[1]
    o_ref[...] = _layer_norm(alpha * x1_ref[...] + f, g2_ref[...], b2_ref[...])


def _combine(pos, ys, x1, route, g2, b2, alpha):
    n = x1.shape[0]
    tile = lambda w: pl.BlockSpec((DMA_TILE, w), lambda i: (i, 0))
    return pl.pallas_call(
        functools.partial(_combine_kernel, alpha=alpha),
        grid=(n // DMA_TILE,),
        in_specs=[pl.BlockSpec((2 * DMA_TILE,), lambda i: (i,), memory_space=pltpu.SMEM),
                  pl.BlockSpec(memory_space=pl.ANY),
                  tile(D_MODEL), tile(LANES), _resident((1, D_MODEL)), _resident((1, D_MODEL))],
        out_specs=tile(D_MODEL),
        out_shape=jax.ShapeDtypeStruct((n, D_MODEL), F32),
        scratch_shapes=[pltpu.VMEM((2, DMA_TILE, D_MODEL), F32), pltpu.SemaphoreType.DMA(())],
        compiler_params=_params(("arbitrary",), disable_bounds_checks=True),
        name="moe_combine",
    )(pos, ys, x1, route, g2, b2)


def _moe_block(x1, route, counts, wg, wu, wd, g2, b2, alpha):
    n = x1.shape[0]
    n_tiles = (2 * n) // MOE_TILE + N_EXPERTS
    count = counts[0, :N_EXPERTS].astype(jnp.int32)
    tiles_per = (count + MOE_TILE - 1) // MOE_TILE
    tile_end = jnp.cumsum(tiles_per)
    group_start = (tile_end - tiles_per) * MOE_TILE
    n_used = tile_end[-1:]
    tile_id = jnp.minimum(jnp.arange(n_tiles, dtype=jnp.int32), n_used[0] - 1)
    tile_expert = jnp.sum((tile_id[:, None] >= tile_end[None, :]).astype(jnp.int32), axis=1)
    idx = route[:, _L_IDX:_L_IDX + 2].astype(jnp.int32)
    rank = route[:, _L_RANK:_L_RANK + 2].astype(jnp.int32)
    pos = (group_start[idx] + rank).reshape(2 * n)

    xs = _dispatch(pos, x1, n_tiles * MOE_TILE)
    ys = _experts(tile_expert, n_used, xs, wg, wu, wd)
    return _combine(pos, ys, x1, route, g2, b2, alpha)


def _t5_bucket(rel):
    half = N_BUCKETS // 2
    max_exact = half // 2
    ret = jnp.where(rel > 0, half, 0)
    n = jnp.abs(rel)
    large = max_exact + (jnp.log(jnp.maximum(n, 1).astype(F32) / max_exact)
                         / math.log(MAX_DISTANCE / max_exact) * (half - max_exact)).astype(jnp.int32)
    large = jnp.minimum(large, half - 1)
    return ret + jnp.where(n < max_exact, n, large)


def _block_bias(rel_bias):
    qi = jnp.arange(Q_BLOCK)[:, None]
    kj = jnp.arange(K_BLOCK)[None, :]
    rel = (kj - WINDOW_CHUNKS * CHUNK) - qi
    bias = jnp.transpose(rel_bias[_t5_bucket(rel)], (2, 0, 1))
    first = (kj // CHUNK) - (qi // CHUNK)
    visible = jnp.logical_and(first >= 0, first <= WINDOW_CHUNKS)
    return jnp.where(visible[None], bias, NEG_INF).astype(F32)


def _in_weights(w_in):
    q = w_in[:, :ATTN_WIDTH] * (HEAD_DIM ** -0.5)
    k = w_in[:, ATTN_WIDTH:ATTN_WIDTH + KV_COLS]
    v = w_in[:, ATTN_WIDTH + KV_COLS:ATTN_WIDTH + 2 * KV_COLS]
    rest = w_in[:, ATTN_WIDTH + 2 * KV_COLS:]
    swap = lambda w: jnp.concatenate([w[:, HEAD_DIM:], w[:, :HEAD_DIM]], axis=1)
    return jnp.concatenate([q, k, swap(k), v, swap(v), rest], axis=1).astype(BF16)


def _block_diag(w):
    nb, d, e = w.shape
    eye = jnp.eye(nb, dtype=w.dtype)
    return (eye[:, None, :, None] * w[:, :, None, :]).reshape(nb * d, nb * e)


def kernel(x, rel_bias, w_in, attn_sinks, conv_w, conv_b, gate_r_w, gate_r_b, gate_i_w, gate_i_b,
           lru_lambda, norm_attn_g, norm_lru_g, w_out, ln1_g, ln1_b, ffn_w_gate, ffn_w_up, ffn_w_down,
           router_w, exp_w_gate, exp_w_up, exp_w_down, ln2_g, ln2_b):
    batch, seq, d = x.shape
    depth = w_in.shape[0]
    alpha = float((2 * depth) ** 0.25)
    bias2 = _block_bias(rel_bias)
    row = lambda v: v.reshape(1, -1)

    h = x.reshape(batch * seq, d)
    for l in range(depth):
        qkv, lru = _inproj(h, _in_weights(w_in[l]))
        ya = _attention(qkv, attn_sinks[l], bias2, row(norm_attn_g[l]), batch, seq)
        w_gates = jnp.concatenate([_block_diag(gate_r_w[l]), _block_diag(gate_i_w[l])], axis=1).astype(BF16)
        b_gates = jnp.concatenate([gate_r_b[l], gate_i_b[l]]).reshape(1, -1)
        yl = _rg_lru(lru, conv_w[l], row(conv_b[l]), w_gates, b_gates, row(lru_lambda[l]),
                     row(norm_lru_g[l]), batch, seq)
        wo = w_out[l].astype(BF16)
        if l % 2 == 0:
            h = _dense_block(ya, yl, h, wo, row(ln1_g[l]), row(ln1_b[l]),
                             ffn_w_gate[l // 2].astype(BF16), ffn_w_up[l // 2].astype(BF16),
                             ffn_w_down[l // 2].astype(BF16), row(ln2_g[l]), row(ln2_b[l]), alpha)
        else:
            rw = jnp.pad(router_w[l // 2], ((0, 0), (0, LANES - N_EXPERTS)))
            x1, route, counts = _router_block(ya, yl, h, wo, row(ln1_g[l]), row(ln1_b[l]), rw, alpha)
            h = _moe_block(x1, route, counts, exp_w_gate[l // 2].astype(BF16),
                           exp_w_up[l // 2].astype(BF16), exp_w_down[l // 2].astype(BF16),
                           row(ln2_g[l]), row(ln2_b[l]), alpha)
    return h.reshape(batch, seq, d)
```

```python
import functools
import math

import jax
import jax.numpy as jnp
from jax import lax
from jax.experimental import pallas as pl
from jax.experimental.pallas import tpu as pltpu

F32 = jnp.float32
BF16 = jnp.bfloat16

D_MODEL = 1024
CHUNK = 64
HEAD_DIM = 64
N_Q_HEADS = 8
N_KV_HEADS = 2
ATTN_WIDTH = N_Q_HEADS * HEAD_DIM
KV_COLS = N_KV_HEADS * HEAD_DIM
WINDOW_CHUNKS = 2
LRU_WIDTH = 512
LRU_BLOCKS = 8
CONV_WIDTH = 4
LRU_C = 8.0
N_BUCKETS = 32
MAX_DISTANCE = 128
N_EXPERTS = 8
NEG_INF = -1e30
LN_EPS = 1e-5
RMS_EPS = 1e-6

LANES = 128
SUBLANES = 8
VMEM_LIMIT = 56 * 1024 * 1024

QKV_COLS = ATTN_WIDTH + 4 * KV_COLS
Q_BLOCK = 2 * CHUNK
K_BLOCK = (WINDOW_CHUNKS + 2) * CHUNK
ATTN_TILE = 512
LRU_TILE = 512
TOKEN_TILE = 512
MOE_TILE = 1024
MOE_F_TILE = 896
DISPATCH_TILE = 1024
COMBINE_TILE = 256
DMA_UNROLL = 8


def _params(sem, **kw):
    return pltpu.CompilerParams(dimension_semantics=sem, vmem_limit_bytes=VMEM_LIMIT, **kw)


def _resident(shape):
    nd = len(shape)
    return pl.BlockSpec(shape, lambda *_: (0,) * nd, pipeline_mode=pl.Buffered(1))


def _layer_norm(x, g, b):
    mu = jnp.mean(x, axis=-1, keepdims=True)
    xc = x - mu
    var = jnp.mean(xc * xc, axis=-1, keepdims=True)
    return xc * lax.rsqrt(var + LN_EPS) * g + b


def _rms_norm(x, g):
    return x * lax.rsqrt(jnp.mean(x * x, axis=-1, keepdims=True) + RMS_EPS) * g


ROW_TILE = D_MODEL // LANES


def _load_row_tiles(ref, rows, lead=()):
    return jnp.concatenate(
        [ref[lead + (pl.ds(c, rows, stride=ROW_TILE), slice(None))] for c in range(ROW_TILE)], axis=1)


def _store_row_tiles(ref, value):
    for c in range(ROW_TILE):
        ref[pl.ds(c, value.shape[0], stride=ROW_TILE), :] = value[:, c * LANES:(c + 1) * LANES]


def _inproj_kernel(x_ref, w_ref, qkv_ref, lru_ref):
    z = jnp.dot(x_ref[...].astype(BF16), w_ref[...], preferred_element_type=F32)
    qkv_ref[...] = z[:, :QKV_COLS].astype(BF16)
    lru_ref[...] = z[:, QKV_COLS:]


def _inproj(x2d, w):
    n = x2d.shape[0]
    cols = w.shape[1]
    return pl.pallas_call(
        _inproj_kernel,
        grid=(n // TOKEN_TILE,),
        in_specs=[pl.BlockSpec((TOKEN_TILE, D_MODEL), lambda i: (i, 0)), _resident((D_MODEL, cols))],
        out_specs=[pl.BlockSpec((TOKEN_TILE, QKV_COLS), lambda i: (i, 0)),
                   pl.BlockSpec((TOKEN_TILE, 2 * LRU_WIDTH), lambda i: (i, 0))],
        out_shape=[jax.ShapeDtypeStruct((n, QKV_COLS), BF16),
                   jax.ShapeDtypeStruct((n, 2 * LRU_WIDTH), F32)],
        compiler_params=_params(("parallel",)),
        name="inproj",
    )(x2d, w)


def _attn_kernel(sink_ref, q_ref, k_ref, ks_ref, v_ref, vs_ref, kp_ref, ksp_ref, vp_ref, vsp_ref,
                 bias_ref, g_ref, o_ref, kk_ref, vv_ref, acc_ref):
    step = pl.program_id(1)
    hist = Q_BLOCK
    lane = lax.broadcasted_iota(jnp.int32, (1, LANES), 1)
    lo = lane < HEAD_DIM
    zero = jnp.zeros((), BF16)

    def fill(dst, plain_prev, plain, swap_prev, swap):
        for rows, plain_ref, swap_ref in ((slice(0, hist), plain_prev, swap_prev),
                                          (slice(hist, hist + ATTN_TILE), plain, swap)):
            a = plain_ref[...]
            s = swap_ref[...]
            dst[0, rows, :] = jnp.where(lo, a, zero)
            dst[1, rows, :] = jnp.where(lo, zero, s)
            dst[2, rows, :] = jnp.where(lo, s, zero)
            dst[3, rows, :] = jnp.where(lo, zero, a)

    fill(kk_ref, kp_ref, k_ref, ksp_ref, ks_ref)
    fill(vv_ref, vp_ref, v_ref, vsp_ref, vs_ref)

    col = lax.broadcasted_iota(jnp.int32, (1, K_BLOCK), 1)
    dn = (((1,), (1,)), ((), ()))

    def sub_block(s, carry):
        q0 = pl.multiple_of(s * Q_BLOCK, Q_BLOCK)
        before_start = jnp.logical_and(jnp.logical_and(step == 0, s == 0), col < hist)
        start_mask = jnp.where(before_start, NEG_INF, 0.0).astype(F32)
        for h in range(N_KV_HEADS):
            q2 = jnp.concatenate(
                [q_ref[pl.ds(q0, Q_BLOCK), pl.ds((2 * h + p) * LANES, LANES)] for p in range(2)], axis=0)
            probs = []
            inv_den = []
            for half in range(2):
                kk = kk_ref[2 * h + half, pl.ds(q0, K_BLOCK), :]
                sc = lax.dot_general(q2, kk, dn, preferred_element_type=F32)
                for p in range(2):
                    head = 4 * h + 2 * p + half
                    logit = sc[p * Q_BLOCK:(p + 1) * Q_BLOCK] + bias_ref[head] + start_mask
                    sink = sink_ref[head]
                    m = jnp.maximum(jnp.max(logit, axis=-1, keepdims=True), sink)
                    e = jnp.exp(logit - m)
                    den = jnp.sum(e, axis=-1, keepdims=True) + jnp.exp(sink - m)
                    probs.append((p, half, e.astype(BF16)))
                    inv_den.append((p, half, 1.0 / den))
            pmap = {(p, half): e for p, half, e in probs}
            dmap = {(p, half): d for p, half, d in inv_den}
            lhs = jnp.concatenate(
                [jnp.concatenate([pmap[(p, 0)], pmap[(p, 1)]], axis=1) for p in range(2)], axis=0)
            rhs = jnp.concatenate([vv_ref[2 * h, pl.ds(q0, K_BLOCK), :],
                                   vv_ref[2 * h + 1, pl.ds(q0, K_BLOCK), :]], axis=0)
            out = jnp.dot(lhs, rhs, preferred_element_type=F32)
            for p in range(2):
                scale = jnp.where(lo, dmap[(p, 0)], dmap[(p, 1)])
                acc_ref[pl.ds(q0, Q_BLOCK), pl.ds((2 * h + p) * LANES, LANES)] = (
                    out[p * Q_BLOCK:(p + 1) * Q_BLOCK] * scale)
        return carry

    lax.fori_loop(0, ATTN_TILE // Q_BLOCK, sub_block, 0)
    o_ref[...] = _rms_norm(acc_ref[...], g_ref[...]).astype(BF16)


def _attention(qkv, sinks, bias2, g_attn, batch, seq):
    n = qkv.shape[0]
    steps = seq // ATTN_TILE
    ratio = ATTN_TILE // Q_BLOCK
    qcb = ATTN_WIDTH // LANES

    def cur(cb):
        return pl.BlockSpec((ATTN_TILE, LANES), lambda b, i: (b * steps + i, cb))

    def prev(cb):
        return pl.BlockSpec((Q_BLOCK, LANES),
                            lambda b, i: (b * steps * ratio + jnp.maximum(i * ratio - 1, 0), cb))

    return pl.pallas_call(
        _attn_kernel,
        grid=(batch, steps),
        in_specs=[pl.BlockSpec(memory_space=pltpu.SMEM),
                  pl.BlockSpec((ATTN_TILE, ATTN_WIDTH), lambda b, i: (b * steps + i, 0)),
                  cur(qcb), cur(qcb + 1), cur(qcb + 2), cur(qcb + 3),
                  prev(qcb), prev(qcb + 1), prev(qcb + 2), prev(qcb + 3),
                  _resident((N_Q_HEADS, Q_BLOCK, K_BLOCK)),
                  _resident((1, ATTN_WIDTH))],
        out_specs=pl.BlockSpec((ATTN_TILE, ATTN_WIDTH), lambda b, i: (b * steps + i, 0)),
        out_shape=jax.ShapeDtypeStruct((n, ATTN_WIDTH), BF16),
        scratch_shapes=[pltpu.VMEM((4, Q_BLOCK + ATTN_TILE, LANES), BF16),
                        pltpu.VMEM((4, Q_BLOCK + ATTN_TILE, LANES), BF16),
                        pltpu.VMEM((ATTN_TILE, ATTN_WIDTH), F32)],
        compiler_params=_params(("parallel", "arbitrary")),
        name="attention",
    )(sinks, qkv, qkv, qkv, qkv, qkv, qkv, qkv, qkv, qkv, bias2, g_attn)


def _lru_kernel(x_ref, gate_ref, prev_ref, cw_ref, cb_ref, wg_ref, bg_ref, lam_ref, g_ref,
                o_ref, h_ref):
    step = pl.program_id(1)

    @pl.when(step == 0)
    def _():
        h_ref[...] = jnp.zeros_like(h_ref)

    x = x_ref[...]
    t = x.shape[0]
    rows = lax.broadcasted_iota(jnp.int32, (t, 1), 0)
    rows8 = lax.broadcasted_iota(jnp.int32, (SUBLANES, 1), 0)
    prev = jnp.where(step == 0, 0.0, prev_ref[...])

    xc = x * cw_ref[CONV_WIDTH - 1:CONV_WIDTH, :] + cb_ref[...]
    for k in range(1, CONV_WIDTH):
        sh = pltpu.roll(x, k, 0)
        head = jnp.where(rows8 < k, pltpu.roll(prev, k, 0), sh[:SUBLANES])
        sh = jnp.concatenate([head, sh[SUBLANES:]], axis=0)
        xc = xc + sh * cw_ref[CONV_WIDTH - 1 - k:CONV_WIDTH - k, :]

    gates = jnp.dot(xc.astype(BF16), wg_ref[...], preferred_element_type=F32) + bg_ref[...]
    r = jax.nn.sigmoid(gates[:, :LRU_WIDTH])
    i = jax.nn.sigmoid(gates[:, LRU_WIDTH:])
    neg_lam = -lam_ref[...]
    softplus = jnp.maximum(neg_lam, 0.0) + jnp.log1p(jnp.exp(-jnp.abs(neg_lam)))
    log_a = (-LRU_C) * r * softplus
    a = jnp.exp(log_a)
    h = jnp.sqrt(jnp.tanh(-log_a) * (1.0 + a * a)) * (i * xc)

    d = 1
    while d < t:
        keep = rows >= d
        a_sh = jnp.where(keep, pltpu.roll(a, d, 0), 1.0)
        h_sh = jnp.where(keep, pltpu.roll(h, d, 0), 0.0)
        h = a * h_sh + h
        a = a * a_sh
        d *= 2
    h = h + a * h_ref[...]
    h_ref[...] = h[t - 1:t, :]

    y = jax.nn.gelu(gate_ref[...]) * h
    o_ref[...] = _rms_norm(y, g_ref[...]).astype(BF16)


def _rg_lru(lru, conv_w, conv_b, w_gates, b_gates, lam, g_lru, batch, seq):
    n = lru.shape[0]
    steps = seq // LRU_TILE
    ratio = LRU_TILE // SUBLANES
    return pl.pallas_call(
        _lru_kernel,
        grid=(batch, steps),
        in_specs=[pl.BlockSpec((LRU_TILE, LRU_WIDTH), lambda b, i: (b * steps + i, 0)),
                  pl.BlockSpec((LRU_TILE, LRU_WIDTH), lambda b, i: (b * steps + i, 1)),
                  pl.BlockSpec((SUBLANES, LRU_WIDTH),
                               lambda b, i: (b * steps * ratio + jnp.maximum(i * ratio - 1, 0), 0)),
                  _resident((CONV_WIDTH, LRU_WIDTH)),
                  _resident((1, LRU_WIDTH)),
                  _resident((LRU_WIDTH, 2 * LRU_WIDTH)),
                  _resident((1, 2 * LRU_WIDTH)),
                  _resident((1, LRU_WIDTH)),
                  _resident((1, LRU_WIDTH))],
        out_specs=pl.BlockSpec((LRU_TILE, LRU_WIDTH), lambda b, i: (b * steps + i, 0)),
        out_shape=jax.ShapeDtypeStruct((n, LRU_WIDTH), BF16),
        scratch_shapes=[pltpu.VMEM((1, LRU_WIDTH), F32)],
        compiler_params=_params(("arbitrary", "arbitrary")),
        name="rg_lru",
    )(lru, lru, lru, conv_w, conv_b, w_gates, b_gates, lam, g_lru)


def _mix_ln1(ya_ref, yl_ref, x_ref, wo_ref, g1_ref, b1_ref, alpha):
    y = jnp.dot(ya_ref[...], wo_ref[:ATTN_WIDTH, :], preferred_element_type=F32)
    y = y + jnp.dot(yl_ref[...], wo_ref[ATTN_WIDTH:, :], preferred_element_type=F32)
    return _layer_norm(alpha * x_ref[...] + y, g1_ref[...], b1_ref[...])


def _dense_block_kernel(ya_ref, yl_ref, x_ref, wo_ref, g1_ref, b1_ref, wg_ref, wu_ref, wd_ref,
                        g2_ref, b2_ref, o_ref, *, alpha, f_chunk):
    x1 = _mix_ln1(ya_ref, yl_ref, x_ref, wo_ref, g1_ref, b1_ref, alpha)
    xb = x1.astype(BF16)
    f = None
    for c in range(wg_ref.shape[1] // f_chunk):
        cols = slice(c * f_chunk, (c + 1) * f_chunk)
        g = jnp.dot(xb, wg_ref[:, cols], preferred_element_type=F32)
        u = jnp.dot(xb, wu_ref[:, cols], preferred_element_type=F32)
        hid = (jax.nn.silu(g) * u).astype(BF16)
        part = jnp.dot(hid, wd_ref[cols, :], preferred_element_type=F32)
        f = part if f is None else f + part
    o_ref[...] = _layer_norm(alpha * x1 + f, g2_ref[...], b2_ref[...])


def _dense_block(ya, yl, x2d, wo, g1, b1, wg, wu, wd, g2, b2, alpha):
    n = x2d.shape[0]
    d_ff = wg.shape[1]
    tile = lambda w: pl.BlockSpec((TOKEN_TILE, w), lambda i: (i, 0))
    return pl.pallas_call(
        functools.partial(_dense_block_kernel, alpha=alpha, f_chunk=d_ff // 2),
        grid=(n // TOKEN_TILE,),
        in_specs=[tile(ATTN_WIDTH), tile(LRU_WIDTH), tile(D_MODEL),
                  _resident((D_MODEL, D_MODEL)), _resident((1, D_MODEL)), _resident((1, D_MODEL)),
                  _resident((D_MODEL, d_ff)), _resident((D_MODEL, d_ff)), _resident((d_ff, D_MODEL)),
                  _resident((1, D_MODEL)), _resident((1, D_MODEL))],
        out_specs=tile(D_MODEL),
        out_shape=jax.ShapeDtypeStruct((n, D_MODEL), F32),
        compiler_params=_params(("parallel",)),
        name="dense_block",
    )(ya, yl, x2d, wo, g1, b1, wg, wu, wd, g2, b2)


_L_IDX, _L_W, _L_RANK = 0, 2, 4


def _router_kernel(ya_ref, yl_ref, x_ref, wo_ref, g1_ref, b1_ref, rw_ref,
                   x1_ref, route_ref, meta_ref, count_ref, carry_ref, *, alpha):
    @pl.when(pl.program_id(0) == 0)
    def _():
        carry_ref[...] = jnp.zeros_like(carry_ref)

    x1 = _mix_ln1(ya_ref, yl_ref, x_ref, wo_ref, g1_ref, b1_ref, alpha)
    _store_row_tiles(x1_ref, x1)
    t = x1.shape[0]
    rw = rw_ref[...]
    x_hi = x1.astype(BF16)
    x_lo = (x1 - x_hi.astype(F32)).astype(BF16)
    w_hi = rw.astype(BF16)
    w_lo = (rw - w_hi.astype(F32)).astype(BF16)
    logits = (jnp.dot(x_hi, w_hi, preferred_element_type=F32)
              + jnp.dot(x_lo, w_hi, preferred_element_type=F32)
              + jnp.dot(x_hi, w_lo, preferred_element_type=F32))
    lane = lax.broadcasted_iota(jnp.int32, (t, LANES), 1).astype(F32)
    lowest = jnp.finfo(F32).min
    l1 = jnp.where(lane < N_EXPERTS, logits, lowest)
    m1 = jnp.max(l1, axis=-1, keepdims=True)
    i1 = jnp.min(jnp.where(l1 == m1, lane, float(LANES)), axis=-1, keepdims=True)
    l2 = jnp.where(lane == i1, lowest, l1)
    m2 = jnp.max(l2, axis=-1, keepdims=True)
    i2 = jnp.min(jnp.where(l2 == m2, lane, float(LANES)), axis=-1, keepdims=True)
    e = jnp.exp(m2 - m1)
    w1 = 1.0 / (1.0 + e)
    w2 = e / (1.0 + e)

    chosen = jnp.logical_or(lane == i1, lane == i2)
    sel = jnp.where(chosen, 1.0, 0.0).astype(BF16)
    r_i = lax.broadcasted_iota(jnp.int32, (t, t), 0)
    c_i = lax.broadcasted_iota(jnp.int32, (t, t), 1)
    earlier = jnp.where(c_i < r_i, 1.0, 0.0).astype(BF16)
    before = jnp.dot(earlier, sel, preferred_element_type=F32) + carry_ref[...]
    rank1 = jnp.sum(jnp.where(lane == i1, before, 0.0), axis=-1, keepdims=True)
    rank2 = jnp.sum(jnp.where(lane == i2, before, 0.0), axis=-1, keepdims=True)
    total = carry_ref[...] + jnp.sum(sel.astype(F32), axis=0, keepdims=True)
    carry_ref[...] = total
    count_ref[...] = total

    route = jnp.where(lane == _L_IDX, i1, 0.0)
    route = jnp.where(lane == _L_IDX + 1, i2, route)
    route = jnp.where(lane == _L_W, w1, route)
    route = jnp.where(lane == _L_W + 1, w2, route)
    route = jnp.where(lane == _L_RANK, rank1, route)
    route = jnp.where(lane == _L_RANK + 1, rank2, route)
    route_ref[...] = route
    meta_ref[...] = route.T[:SUBLANES]


def _router_block(ya, yl, x2d, wo, g1, b1, rw, alpha):
    n = x2d.shape[0]
    tile = lambda w: pl.BlockSpec((TOKEN_TILE, w), lambda i: (i, 0))
    return pl.pallas_call(
        functools.partial(_router_kernel, alpha=alpha),
        grid=(n // TOKEN_TILE,),
        in_specs=[tile(ATTN_WIDTH), tile(LRU_WIDTH), tile(D_MODEL),
                  _resident((D_MODEL, D_MODEL)), _resident((1, D_MODEL)), _resident((1, D_MODEL)),
                  _resident((D_MODEL, LANES))],
        out_specs=[pl.BlockSpec((TOKEN_TILE * ROW_TILE, LANES), lambda i: (i, 0)), tile(LANES),
                   pl.BlockSpec((SUBLANES, TOKEN_TILE), lambda i: (0, i)),
                   pl.BlockSpec((1, LANES), lambda i: (0, 0))],
        out_shape=[jax.ShapeDtypeStruct((n * ROW_TILE, LANES), F32),
                   jax.ShapeDtypeStruct((n, LANES), F32),
                   jax.ShapeDtypeStruct((SUBLANES, n), F32),
                   jax.ShapeDtypeStruct((1, LANES), F32)],
        scratch_shapes=[pltpu.VMEM((1, LANES), F32)],
        compiler_params=_params(("arbitrary",)),
        name="router_block",
    )(ya, yl, x2d, wo, g1, b1, rw)


def _tile_copy(src_ref, src_row, dst_ref, dst_row, sem):
    src = src_ref.at[pl.ds(pl.multiple_of(src_row * ROW_TILE, ROW_TILE), ROW_TILE)]
    dst = dst_ref.at[pl.ds(pl.multiple_of(dst_row * ROW_TILE, ROW_TILE), ROW_TILE)]
    return pltpu.make_async_copy(src, dst, sem)


def _dispatch_kernel(pad_start_ref, pad_len_ref, pos0_ref, pos1_ref, x_ref, xs_ref, sem, pad_sem):
    def issue(t, carry):
        _tile_copy(x_ref, t, xs_ref, pos0_ref[t], sem).start(priority=0)
        _tile_copy(x_ref, t, xs_ref, pos1_ref[t], sem).start(priority=1)
        return carry

    lax.fori_loop(0, DISPATCH_TILE, issue, 0, unroll=DMA_UNROLL)
    for _ in range(2):
        pltpu.make_async_copy(x_ref, xs_ref.at[pl.ds(0, DISPATCH_TILE * ROW_TILE)], sem).wait()

    @pl.when(pl.program_id(0) == pl.num_programs(0) - 1)
    def _():
        for e in range(N_EXPERTS):
            def fill(r, carry):
                _tile_copy(x_ref, 0, xs_ref, pad_start_ref[e] + r, pad_sem).start()
                return carry

            def drain(r, carry):
                _tile_copy(x_ref, 0, xs_ref, 0, pad_sem).wait()
                return carry

            lax.fori_loop(0, pad_len_ref[e], fill, 0)
            lax.fori_loop(0, pad_len_ref[e], drain, 0)

        def idle_copy(r):
            first = pl.multiple_of((pad_start_ref[N_EXPERTS] + r * MOE_TILE) * ROW_TILE, MOE_TILE * ROW_TILE)
            return pltpu.make_async_copy(x_ref, xs_ref.at[pl.ds(first, MOE_TILE * ROW_TILE)], pad_sem)

        def fill_idle(r, carry):
            idle_copy(r).start()
            return carry

        def drain_idle(r, carry):
            idle_copy(r).wait()
            return carry

        lax.fori_loop(0, pad_len_ref[N_EXPERTS], fill_idle, 0)
        lax.fori_loop(0, pad_len_ref[N_EXPERTS], drain_idle, 0)


def _dispatch(pad_start, pad_len, pos0, pos1, x1, sorted_rows):
    n = x1.shape[0] // ROW_TILE
    assert DISPATCH_TILE == MOE_TILE
    smem_tile = pl.BlockSpec((DISPATCH_TILE,), lambda i, *_: (i,), memory_space=pltpu.SMEM)
    grid_spec = pltpu.PrefetchScalarGridSpec(
        num_scalar_prefetch=2,
        grid=(n // DISPATCH_TILE,),
        in_specs=[smem_tile, smem_tile,
                  pl.BlockSpec((DISPATCH_TILE * ROW_TILE, LANES), lambda i, *_: (i, 0))],
        out_specs=pl.BlockSpec(memory_space=pl.ANY),
        scratch_shapes=[pltpu.SemaphoreType.DMA(()), pltpu.SemaphoreType.DMA(())],
    )
    return pl.pallas_call(
        _dispatch_kernel,
        grid_spec=grid_spec,
        out_shape=jax.ShapeDtypeStruct((sorted_rows * ROW_TILE, LANES), F32),
        compiler_params=_params(("arbitrary",), disable_bounds_checks=True),
        name="moe_dispatch",
    )(pad_start, pad_len, pos0, pos1, x1)


def _experts_kernel(tile_expert_ref, n_used_ref, x_ref, wg_ref, wu_ref, wd_ref, y_ref, xb_ref, acc_ref):
    del tile_expert_ref
    i = pl.program_id(0)
    j = pl.program_id(1)
    last = pl.num_programs(1) - 1

    @pl.when(i < n_used_ref[0])
    def _():
        @pl.when(j == 0)
        def _():
            for c in range(ROW_TILE):
                xb_ref[:, c * LANES:(c + 1) * LANES] = (
                    x_ref[pl.ds(c, MOE_TILE, stride=ROW_TILE), :].astype(BF16))

        xb = xb_ref[...]
        g = jnp.dot(xb, wg_ref[...], preferred_element_type=F32)
        u = jnp.dot(xb, wu_ref[...], preferred_element_type=F32)
        hid = (jax.nn.silu(g) * u).astype(BF16)
        part = jnp.dot(hid, wd_ref[...], preferred_element_type=F32)

        @pl.when(j == 0)
        def _():
            acc_ref[...] = part

        @pl.when(jnp.logical_and(j > 0, j < last))
        def _():
            acc_ref[...] += part

        @pl.when(j == last)
        def _():
            _store_row_tiles(y_ref, acc_ref[...] + part)

    @pl.when(jnp.logical_and(i >= n_used_ref[0], j == 0))
    def _():
        y_ref[...] = jnp.zeros_like(y_ref)


def _experts(tile_expert, n_used, xs, wg, wu, wd):
    rows = xs.shape[0] // ROW_TILE
    d_ff = wg.shape[2]
    n_tiles = rows // MOE_TILE
    n_f = d_ff // MOE_F_TILE
    assert n_f >= 2

    def row_map(i, j, te, nu):
        return (jnp.minimum(i, nu[0] - 1), 0)

    def f_of(i, j, nu):
        return jnp.where(i < nu[0], j, n_f - 1)

    grid_spec = pltpu.PrefetchScalarGridSpec(
        num_scalar_prefetch=2,
        grid=(n_tiles, n_f),
        in_specs=[pl.BlockSpec((MOE_TILE * ROW_TILE, LANES), row_map),
                  pl.BlockSpec((None, D_MODEL, MOE_F_TILE), lambda i, j, te, nu: (te[i], 0, f_of(i, j, nu))),
                  pl.BlockSpec((None, D_MODEL, MOE_F_TILE), lambda i, j, te, nu: (te[i], 0, f_of(i, j, nu))),
                  pl.BlockSpec((None, MOE_F_TILE, D_MODEL), lambda i, j, te, nu: (te[i], f_of(i, j, nu), 0))],
        out_specs=pl.BlockSpec((MOE_TILE * ROW_TILE, LANES), lambda i, j, te, nu: (i, 0)),
        scratch_shapes=[pltpu.VMEM((MOE_TILE, D_MODEL), BF16), pltpu.VMEM((MOE_TILE, D_MODEL), F32)],
    )
    return pl.pallas_call(
        _experts_kernel,
        grid_spec=grid_spec,
        out_shape=jax.ShapeDtypeStruct((rows * ROW_TILE, LANES), F32),
        compiler_params=_params(("arbitrary", "arbitrary")),
        name="moe_experts",
    )(tile_expert, n_used, xs, wg, wu, wd)


def _combine_kernel(pos0_ref, pos1_ref, nxt0_ref, nxt1_ref, ys_ref, x1_ref, route_ref, g2_ref, b2_ref,
                    o_ref, buf_ref, sem, *, alpha):
    i = pl.program_id(0)
    slot = i % 2

    def issue(p0_ref, p1_ref, dst_slot):
        def body(t, carry):
            for k, p_ref in enumerate((p0_ref, p1_ref)):
                _tile_copy(ys_ref, p_ref[t], buf_ref.at[dst_slot, k], t, sem.at[dst_slot]).start(priority=k)
            return carry
        lax.fori_loop(0, COMBINE_TILE, body, 0, unroll=DMA_UNROLL)

    @pl.when(i == 0)
    def _():
        issue(pos0_ref, pos1_ref, 0)

    @pl.when(i + 1 < pl.num_programs(0))
    def _():
        issue(nxt0_ref, nxt1_ref, 1 - slot)

    for k in range(2):
        pltpu.make_async_copy(ys_ref.at[pl.ds(0, COMBINE_TILE * ROW_TILE)], buf_ref.at[slot, k],
                              sem.at[slot]).wait()
    route = route_ref[...]
    y0 = _load_row_tiles(buf_ref, COMBINE_TILE, (slot, 0))
    y1 = _load_row_tiles(buf_ref, COMBINE_TILE, (slot, 1))
    f = route[:, _L_W:_L_W + 1] * y0 + route[:, _L_W + 1:_L_W + 2] * y1
    x1 = _load_row_tiles(x1_ref, COMBINE_TILE)
    o_ref[...] = _layer_norm(alpha * x1 + f, g2_ref[...], b2_ref[...])


def _combine(pos0, pos1, ys, x1, route, g2, b2, alpha):
    n = x1.shape[0] // ROW_TILE
    steps = n // COMBINE_TILE
    tile = lambda w: pl.BlockSpec((COMBINE_TILE, w), lambda i: (i, 0))
    smem_cur = pl.BlockSpec((COMBINE_TILE,), lambda i: (i,), memory_space=pltpu.SMEM)
    smem_next = pl.BlockSpec((COMBINE_TILE,), lambda i: (jnp.minimum(i + 1, steps - 1),),
                             memory_space=pltpu.SMEM)
    return pl.pallas_call(
        functools.partial(_combine_kernel, alpha=alpha),
        grid=(steps,),
        in_specs=[smem_cur, smem_cur, smem_next, smem_next,
                  pl.BlockSpec(memory_space=pl.ANY),
                  pl.BlockSpec((COMBINE_TILE * ROW_TILE, LANES), lambda i: (i, 0)),
                  tile(LANES), _resident((1, D_MODEL)), _resident((1, D_MODEL))],
        out_specs=tile(D_MODEL),
        out_shape=jax.ShapeDtypeStruct((n, D_MODEL), F32),
        scratch_shapes=[pltpu.VMEM((2, 2, COMBINE_TILE * ROW_TILE, LANES), F32),
                        pltpu.SemaphoreType.DMA((2,))],
        compiler_params=_params(("arbitrary",), disable_bounds_checks=True),
        name="moe_combine",
    )(pos0, pos1, pos0, pos1, ys, x1, route, g2, b2)


def _moe_block(x1, route, meta, counts, wg, wu, wd, g2, b2, alpha):
    n = route.shape[0]
    n_tiles = (2 * n) // MOE_TILE + N_EXPERTS
    count = counts[0, :N_EXPERTS].astype(jnp.int32)
    tiles_per = (count + MOE_TILE - 1) // MOE_TILE
    tile_end = jnp.cumsum(tiles_per)
    group_start = (tile_end - tiles_per) * MOE_TILE
    n_used = tile_end[-1:]
    tile_id = jnp.minimum(jnp.arange(n_tiles, dtype=jnp.int32), n_used[0] - 1)
    tile_expert = jnp.sum((tile_id[:, None] >= tile_end[None, :]).astype(jnp.int32), axis=1)
    idx = meta[_L_IDX:_L_IDX + 2].astype(jnp.int32)
    rank = meta[_L_RANK:_L_RANK + 2].astype(jnp.int32)
    start = jnp.sum(jnp.where(idx[:, :, None] == jnp.arange(N_EXPERTS), group_start, 0), axis=-1)
    pos = start + rank

    pad_start = jnp.concatenate([group_start + count, n_used * MOE_TILE])
    pad_len = jnp.concatenate([tiles_per * MOE_TILE - count, n_tiles - n_used])
    xs = _dispatch(pad_start, pad_len, pos[0], pos[1], x1, n_tiles * MOE_TILE)
    ys = _experts(tile_expert, n_used, xs, wg, wu, wd)
    return _combine(pos[0], pos[1], ys, x1, route, g2, b2, alpha)


def _t5_bucket(rel):
    half = N_BUCKETS // 2
    max_exact = half // 2
    ret = jnp.where(rel > 0, half, 0)
    n = jnp.abs(rel)
    large = max_exact + (jnp.log(jnp.maximum(n, 1).astype(F32) / max_exact)
                         / math.log(MAX_DISTANCE / max_exact) * (half - max_exact)).astype(jnp.int32)
    large = jnp.minimum(large, half - 1)
    return ret + jnp.where(n < max_exact, n, large)


def _block_bias(rel_bias):
    qi = jnp.arange(Q_BLOCK)[:, None]
    kj = jnp.arange(K_BLOCK)[None, :]
    rel = (kj - WINDOW_CHUNKS * CHUNK) - qi
    bias = jnp.transpose(rel_bias[_t5_bucket(rel)], (2, 0, 1))
    first = (kj // CHUNK) - (qi // CHUNK)
    visible = jnp.logical_and(first >= 0, first <= WINDOW_CHUNKS)
    return jnp.where(visible[None], bias, NEG_INF).astype(F32)


def _in_weights(w_in):
    q = w_in[:, :ATTN_WIDTH] * (HEAD_DIM ** -0.5)
    k = w_in[:, ATTN_WIDTH:ATTN_WIDTH + KV_COLS]
    v = w_in[:, ATTN_WIDTH + KV_COLS:ATTN_WIDTH + 2 * KV_COLS]
    rest = w_in[:, ATTN_WIDTH + 2 * KV_COLS:]
    swap = lambda w: jnp.concatenate([w[:, HEAD_DIM:], w[:, :HEAD_DIM]], axis=1)
    return jnp.concatenate([q, k, swap(k), v, swap(v), rest], axis=1).astype(BF16)


def _block_diag(w):
    nb, d, e = w.shape
    eye = jnp.eye(nb, dtype=w.dtype)
    return (eye[:, None, :, None] * w[:, :, None, :]).reshape(nb * d, nb * e)


def kernel(x, rel_bias, w_in, attn_sinks, conv_w, conv_b, gate_r_w, gate_r_b, gate_i_w, gate_i_b,
           lru_lambda, norm_attn_g, norm_lru_g, w_out, ln1_g, ln1_b, ffn_w_gate, ffn_w_up, ffn_w_down,
           router_w, exp_w_gate, exp_w_up, exp_w_down, ln2_g, ln2_b):
    batch, seq, d = x.shape
    depth = w_in.shape[0]
    alpha = float((2 * depth) ** 0.25)
    bias2 = _block_bias(rel_bias)
    row = lambda v: v.reshape(1, -1)

    h = x.reshape(batch * seq, d)
    for l in range(depth):
        qkv, lru = _inproj(h, _in_weights(w_in[l]))
        ya = _attention(qkv, attn_sinks[l], bias2, row(norm_attn_g[l]), batch, seq)
        w_gates = jnp.concatenate([_block_diag(gate_r_w[l]), _block_diag(gate_i_w[l])], axis=1).astype(BF16)
        b_gates = jnp.concatenate([gate_r_b[l], gate_i_b[l]]).reshape(1, -1)
        yl = _rg_lru(lru, conv_w[l], row(conv_b[l]), w_gates, b_gates, row(lru_lambda[l]),
                     row(norm_lru_g[l]), batch, seq)
        wo = w_out[l].astype(BF16)
        if l % 2 == 0:
            h = _dense_block(ya, yl, h, wo, row(ln1_g[l]), row(ln1_b[l]),
                             ffn_w_gate[l // 2].astype(BF16), ffn_w_up[l // 2].astype(BF16),
                             ffn_w_down[l // 2].astype(BF16), row(ln2_g[l]), row(ln2_b[l]), alpha)
        else:
            rw = jnp.pad(router_w[l // 2], ((0, 0), (0, LANES - N_EXPERTS)))
            x1, route, meta, counts = _router_block(ya, yl, h, wo, row(ln1_g[l]), row(ln1_b[l]), rw, alpha)
            h = _moe_block(x1, route, meta, counts, exp_w_gate[l // 2].astype(BF16),
                           exp_w_up[l // 2].astype(BF16), exp_w_down[l // 2].astype(BF16),
                           row(ln2_g[l]), row(ln2_b[l]), alpha)
    return h.reshape(batch, seq, d)
```

```python
import functools
import math

import jax
import jax.numpy as jnp
from jax import lax
from jax.experimental import pallas as pl
from jax.experimental.pallas import tpu as pltpu

F32 = jnp.float32
BF16 = jnp.bfloat16

D_MODEL = 1024
CHUNK = 64
HEAD_DIM = 64
N_Q_HEADS = 8
N_KV_HEADS = 2
ATTN_WIDTH = N_Q_HEADS * HEAD_DIM
KV_COLS = N_KV_HEADS * HEAD_DIM
WINDOW_CHUNKS = 2
LRU_WIDTH = 512
LRU_BLOCKS = 8
CONV_WIDTH = 4
LRU_C = 8.0
N_BUCKETS = 32
MAX_DISTANCE = 128
N_EXPERTS = 8
NEG_INF = -1e30
LN_EPS = 1e-5
RMS_EPS = 1e-6
LOG2E = math.log2(math.e)

LANES = 128
SUBLANES = 8
VMEM_LIMIT = 56 * 1024 * 1024

QKV_COLS = ATTN_WIDTH + 4 * KV_COLS
Q_BLOCK = 2 * CHUNK
K_BLOCK = (WINDOW_CHUNKS + 2) * CHUNK
ATTN_TILE = 512
LRU_TILE = 512
TOKEN_TILE = 512
MXU_TILE = 256
MOE_TILE = 512
MOE_F_TILE = 7 * MXU_TILE
DISPATCH_TILE = 1024
COMBINE_TILE = 256
DMA_UNROLL = 8


def _params(sem, **kw):
    return pltpu.CompilerParams(dimension_semantics=sem, vmem_limit_bytes=VMEM_LIMIT, **kw)


def _resident(shape):
    nd = len(shape)
    return pl.BlockSpec(shape, lambda *_: (0,) * nd, pipeline_mode=pl.Buffered(1))


def _layer_norm(x, g, b):
    mu = jnp.mean(x, axis=-1, keepdims=True)
    xc = x - mu
    var = jnp.mean(xc * xc, axis=-1, keepdims=True)
    return xc * lax.rsqrt(var + LN_EPS) * g + b


def _rms_norm(x, g):
    return x * lax.rsqrt(jnp.mean(x * x, axis=-1, keepdims=True) + RMS_EPS) * g


ROW_TILE = D_MODEL // LANES


def _load_row_tiles(ref, rows, lead=()):
    return jnp.concatenate(
        [ref[lead + (pl.ds(c, rows, stride=ROW_TILE), slice(None))] for c in range(ROW_TILE)], axis=1)


def _store_row_tiles(ref, value):
    for c in range(ROW_TILE):
        ref[pl.ds(c, value.shape[0], stride=ROW_TILE), :] = value[:, c * LANES:(c + 1) * LANES]


def _inproj_kernel(x_ref, w_ref, qkv_ref, lru_ref):
    z = jnp.dot(x_ref[...].astype(BF16), w_ref[...], preferred_element_type=F32)
    qkv_ref[...] = z[:, :QKV_COLS].astype(BF16)
    lru_ref[...] = z[:, QKV_COLS:]


def _inproj(x2d, w):
    n = x2d.shape[0]
    cols = w.shape[1]
    return pl.pallas_call(
        _inproj_kernel,
        grid=(n // TOKEN_TILE,),
        in_specs=[pl.BlockSpec((TOKEN_TILE, D_MODEL), lambda i: (i, 0)), _resident((D_MODEL, cols))],
        out_specs=[pl.BlockSpec((TOKEN_TILE, QKV_COLS), lambda i: (i, 0)),
                   pl.BlockSpec((TOKEN_TILE, 2 * LRU_WIDTH), lambda i: (i, 0))],
        out_shape=[jax.ShapeDtypeStruct((n, QKV_COLS), BF16),
                   jax.ShapeDtypeStruct((n, 2 * LRU_WIDTH), F32)],
        compiler_params=_params(("parallel",)),
        name="inproj",
    )(x2d, w)


def _attn_kernel(sink_ref, q_ref, k_ref, ks_ref, v_ref, vs_ref, kp_ref, ksp_ref, vp_ref, vsp_ref,
                 bias_ref, g_ref, o_ref, kk_ref, vv_ref, acc_ref):
    step = pl.program_id(1)
    hist = Q_BLOCK
    lane = lax.broadcasted_iota(jnp.int32, (1, LANES), 1)
    lo = lane < HEAD_DIM
    zero = jnp.zeros((), BF16)

    def fill(dst, plain_prev, plain, swap_prev, swap):
        for rows, plain_ref, swap_ref in ((slice(0, hist), plain_prev, swap_prev),
                                          (slice(hist, hist + ATTN_TILE), plain, swap)):
            a = plain_ref[...]
            s = swap_ref[...]
            dst[0, rows, :] = jnp.where(lo, a, zero)
            dst[1, rows, :] = jnp.where(lo, zero, s)
            dst[2, rows, :] = jnp.where(lo, s, zero)
            dst[3, rows, :] = jnp.where(lo, zero, a)

    fill(kk_ref, kp_ref, k_ref, ksp_ref, ks_ref)
    fill(vv_ref, vp_ref, v_ref, vsp_ref, vs_ref)

    col = lax.broadcasted_iota(jnp.int32, (1, K_BLOCK), 1)
    dn = (((1,), (1,)), ((), ()))

    for s in range(ATTN_TILE // Q_BLOCK):
        q0 = s * Q_BLOCK
        start_mask = None
        if s == 0:
            before_start = jnp.logical_and(step == 0, col < hist)
            start_mask = jnp.where(before_start, NEG_INF, 0.0).astype(F32)
        for h in range(N_KV_HEADS):
            q2 = jnp.concatenate(
                [q_ref[pl.ds(q0, Q_BLOCK), pl.ds((2 * h + p) * LANES, LANES)] for p in range(2)], axis=0)
            probs = []
            inv_den = []
            for half in range(2):
                kk = kk_ref[2 * h + half, pl.ds(q0, K_BLOCK), :]
                sc = lax.dot_general(q2, kk, dn, preferred_element_type=F32)
                for p in range(2):
                    head = 4 * h + 2 * p + half
                    logit = sc[p * Q_BLOCK:(p + 1) * Q_BLOCK] + bias_ref[head]
                    if start_mask is not None:
                        logit = logit + start_mask
                    sink = sink_ref[head]
                    m = jnp.maximum(jnp.max(logit, axis=-1, keepdims=True), sink)
                    e = jnp.exp2(logit - m)
                    den = jnp.sum(e, axis=-1, keepdims=True) + jnp.exp2(sink - m)
                    probs.append((p, half, e.astype(BF16)))
                    inv_den.append((p, half, 1.0 / den))
            pmap = {(p, half): e for p, half, e in probs}
            dmap = {(p, half): d for p, half, d in inv_den}
            lhs = jnp.concatenate(
                [jnp.concatenate([pmap[(p, 0)], pmap[(p, 1)]], axis=1) for p in range(2)], axis=0)
            rhs = jnp.concatenate([vv_ref[2 * h, pl.ds(q0, K_BLOCK), :],
                                   vv_ref[2 * h + 1, pl.ds(q0, K_BLOCK), :]], axis=0)
            out = jnp.dot(lhs, rhs, preferred_element_type=F32)
            for p in range(2):
                scale = jnp.where(lo, dmap[(p, 0)], dmap[(p, 1)])
                acc_ref[pl.ds(q0, Q_BLOCK), pl.ds((2 * h + p) * LANES, LANES)] = (
                    out[p * Q_BLOCK:(p + 1) * Q_BLOCK] * scale)

    o_ref[...] = _rms_norm(acc_ref[...], g_ref[...]).astype(BF16)


def _attention(qkv, sinks, bias2, g_attn, batch, seq):
    n = qkv.shape[0]
    steps = seq // ATTN_TILE
    ratio = ATTN_TILE // Q_BLOCK
    qcb = ATTN_WIDTH // LANES

    def cur(cb):
        return pl.BlockSpec((ATTN_TILE, LANES), lambda b, i: (b * steps + i, cb))

    def prev(cb):
        return pl.BlockSpec((Q_BLOCK, LANES),
                            lambda b, i: (b * steps * ratio + jnp.maximum(i * ratio - 1, 0), cb))

    return pl.pallas_call(
        _attn_kernel,
        grid=(batch, steps),
        in_specs=[pl.BlockSpec(memory_space=pltpu.SMEM),
                  pl.BlockSpec((ATTN_TILE, ATTN_WIDTH), lambda b, i: (b * steps + i, 0)),
                  cur(qcb), cur(qcb + 1), cur(qcb + 2), cur(qcb + 3),
                  prev(qcb), prev(qcb + 1), prev(qcb + 2), prev(qcb + 3),
                  _resident((N_Q_HEADS, Q_BLOCK, K_BLOCK)),
                  _resident((1, ATTN_WIDTH))],
        out_specs=pl.BlockSpec((ATTN_TILE, ATTN_WIDTH), lambda b, i: (b * steps + i, 0)),
        out_shape=jax.ShapeDtypeStruct((n, ATTN_WIDTH), BF16),
        scratch_shapes=[pltpu.VMEM((4, Q_BLOCK + ATTN_TILE, LANES), BF16),
                        pltpu.VMEM((4, Q_BLOCK + ATTN_TILE, LANES), BF16),
                        pltpu.VMEM((ATTN_TILE, ATTN_WIDTH), F32)],
        compiler_params=_params(("parallel", "arbitrary")),
        name="attention",
    )(sinks, qkv, qkv, qkv, qkv, qkv, qkv, qkv, qkv, qkv, bias2, g_attn)


def _lru_kernel(x_ref, gate_ref, prev_ref, cw_ref, cb_ref, wg_ref, bg_ref, lam_ref, g_ref,
                o_ref, h_ref, a_ref, u_ref, ext_ref):
    step = pl.program_id(1)

    @pl.when(step == 0)
    def _():
        h_ref[...] = jnp.zeros_like(h_ref)

    x = x_ref[...]
    t = x.shape[0]
    prev = jnp.where(step == 0, 0.0, prev_ref[...])

    ext_ref[:SUBLANES, :] = prev
    ext_ref[SUBLANES:, :] = x
    xc = x * cw_ref[CONV_WIDTH - 1:CONV_WIDTH, :] + cb_ref[...]
    for k in range(1, CONV_WIDTH):
        xc = xc + ext_ref[SUBLANES - k:SUBLANES - k + t, :] * cw_ref[CONV_WIDTH - 1 - k:CONV_WIDTH - k, :]

    gates = jnp.dot(xc.astype(BF16), wg_ref[...], preferred_element_type=F32) + bg_ref[...]
    r = jax.nn.sigmoid(gates[:, :LRU_WIDTH])
    i = jax.nn.sigmoid(gates[:, LRU_WIDTH:])
    neg_lam = -lam_ref[...]
    softplus = jnp.maximum(neg_lam, 0.0) + jnp.log1p(jnp.exp(-jnp.abs(neg_lam)))
    log_a = (-LRU_C) * r * softplus
    a = jnp.exp(log_a)
    h = jnp.sqrt(jnp.tanh(-log_a) * (1.0 + a * a)) * (i * xc)

    def doubling(av, hv, index, length, axis):
        d = 1
        while d < length:
            keep = index >= d
            a_sh = jnp.where(keep, pltpu.roll(av, d, axis), 1.0)
            h_sh = jnp.where(keep, pltpu.roll(hv, d, axis), 0.0)
            hv = av * h_sh + hv
            av = av * a_sh
            d *= 2
        return av, hv

    groups = t // SUBLANES
    grouped = (groups, SUBLANES, LRU_WIDTH)
    sub = lax.broadcasted_iota(jnp.int32, (1, SUBLANES, 1), 1)
    a, h = doubling(a.reshape(grouped), h.reshape(grouped), sub, SUBLANES, 1)
    a = a.reshape(t, LRU_WIDTH)
    h = h.reshape(t, LRU_WIDTH)
    grow = lax.broadcasted_iota(jnp.int32, (groups, 1), 0)

    def group_last(ref, value):
        for c in range(LRU_WIDTH // LANES):
            ref[c] = value[:, c * LANES:(c + 1) * LANES]
        return jnp.concatenate([ref[c, pl.ds(SUBLANES - 1, groups, stride=SUBLANES), :]
                                for c in range(LRU_WIDTH // LANES)], axis=1)

    ag, hg = doubling(group_last(a_ref, a), group_last(u_ref, h), grow, groups, 0)
    state = hg + ag * h_ref[...]
    enter = jnp.where(grow >= 1, pltpu.roll(state, 1, 0), h_ref[...])
    h_ref[...] = state[groups - 1:groups, :]
    enter = jnp.broadcast_to(enter[:, None, :], (groups, SUBLANES, LRU_WIDTH)).reshape(t, LRU_WIDTH)
    h = h + a * enter

    y = jax.nn.gelu(gate_ref[...]) * h
    o_ref[...] = _rms_norm(y, g_ref[...]).astype(BF16)


def _rg_lru(lru, conv_w, conv_b, w_gates, b_gates, lam, g_lru, batch, seq):
    n = lru.shape[0]
    steps = seq // LRU_TILE
    ratio = LRU_TILE // SUBLANES
    return pl.pallas_call(
        _lru_kernel,
        grid=(batch, steps),
        in_specs=[pl.BlockSpec((LRU_TILE, LRU_WIDTH), lambda b, i: (b * steps + i, 0)),
                  pl.BlockSpec((LRU_TILE, LRU_WIDTH), lambda b, i: (b * steps + i, 1)),
                  pl.BlockSpec((SUBLANES, LRU_WIDTH),
                               lambda b, i: (b * steps * ratio + jnp.maximum(i * ratio - 1, 0), 0)),
                  _resident((CONV_WIDTH, LRU_WIDTH)),
                  _resident((1, LRU_WIDTH)),
                  _resident((LRU_WIDTH, 2 * LRU_WIDTH)),
                  _resident((1, 2 * LRU_WIDTH)),
                  _resident((1, LRU_WIDTH)),
                  _resident((1, LRU_WIDTH))],
        out_specs=pl.BlockSpec((LRU_TILE, LRU_WIDTH), lambda b, i: (b * steps + i, 0)),
        out_shape=jax.ShapeDtypeStruct((n, LRU_WIDTH), BF16),
        scratch_shapes=[pltpu.VMEM((1, LRU_WIDTH), F32),
                        pltpu.VMEM((LRU_WIDTH // LANES, LRU_TILE, LANES), F32),
                        pltpu.VMEM((LRU_WIDTH // LANES, LRU_TILE, LANES), F32),
                        pltpu.VMEM((SUBLANES + LRU_TILE, LRU_WIDTH), F32)],
        compiler_params=_params(("arbitrary", "arbitrary")),
        name="rg_lru",
    )(lru, lru, lru, conv_w, conv_b, w_gates, b_gates, lam, g_lru)


def _mix_ln1(ya_ref, yl_ref, x_ref, wo_ref, g1_ref, b1_ref, alpha):
    y = jnp.dot(ya_ref[...], wo_ref[:ATTN_WIDTH, :], preferred_element_type=F32)
    y = y + jnp.dot(yl_ref[...], wo_ref[ATTN_WIDTH:, :], preferred_element_type=F32)
    return _layer_norm(alpha * x_ref[...] + y, g1_ref[...], b1_ref[...])


def _dense_block_kernel(ya_ref, yl_ref, x_ref, wo_ref, g1_ref, b1_ref, wg_ref, wu_ref, wd_ref,
                        g2_ref, b2_ref, o_ref, *, alpha, f_splits):
    x1 = _mix_ln1(ya_ref, yl_ref, x_ref, wo_ref, g1_ref, b1_ref, alpha)
    xb = x1.astype(BF16)
    f = None
    for lo, hi in zip(f_splits[:-1], f_splits[1:]):
        cols = slice(lo, hi)
        g = jnp.dot(xb, wg_ref[:, cols], preferred_element_type=F32)
        u = jnp.dot(xb, wu_ref[:, cols], preferred_element_type=F32)
        hid = (jax.nn.silu(g) * u).astype(BF16)
        part = jnp.dot(hid, wd_ref[cols, :], preferred_element_type=F32)
        f = part if f is None else f + part
    o_ref[...] = _layer_norm(alpha * x1 + f, g2_ref[...], b2_ref[...])


def _dense_block(ya, yl, x2d, wo, g1, b1, wg, wu, wd, g2, b2, alpha):
    n = x2d.shape[0]
    d_ff = wg.shape[1]
    tile = lambda w: pl.BlockSpec((TOKEN_TILE, w), lambda i: (i, 0))
    mxu_tiles = pl.cdiv(d_ff, MXU_TILE)
    f_splits = (0, (mxu_tiles + 1) // 2 * MXU_TILE, d_ff)
    return pl.pallas_call(
        functools.partial(_dense_block_kernel, alpha=alpha, f_splits=f_splits),
        grid=(n // TOKEN_TILE,),
        in_specs=[tile(ATTN_WIDTH), tile(LRU_WIDTH), tile(D_MODEL),
                  _resident((D_MODEL, D_MODEL)), _resident((1, D_MODEL)), _resident((1, D_MODEL)),
                  _resident((D_MODEL, d_ff)), _resident((D_MODEL, d_ff)), _resident((d_ff, D_MODEL)),
                  _resident((1, D_MODEL)), _resident((1, D_MODEL))],
        out_specs=tile(D_MODEL),
        out_shape=jax.ShapeDtypeStruct((n, D_MODEL), F32),
        compiler_params=_params(("parallel",)),
        name="dense_block",
    )(ya, yl, x2d, wo, g1, b1, wg, wu, wd, g2, b2)


_L_IDX, _L_W, _L_RANK = 0, 2, 4


def _router_kernel(ya_ref, yl_ref, x_ref, wo_ref, g1_ref, b1_ref, rw_ref,
                   x1_ref, route_ref, meta_ref, count_ref, carry_ref, *, alpha):
    @pl.when(pl.program_id(0) == 0)
    def _():
        carry_ref[...] = jnp.zeros_like(carry_ref)

    x1 = _mix_ln1(ya_ref, yl_ref, x_ref, wo_ref, g1_ref, b1_ref, alpha)
    _store_row_tiles(x1_ref, x1)
    t = x1.shape[0]
    rw = rw_ref[...]
    x_hi = x1.astype(BF16)
    x_lo = (x1 - x_hi.astype(F32)).astype(BF16)
    w_hi = rw.astype(BF16)
    w_lo = (rw - w_hi.astype(F32)).astype(BF16)
    logits = (jnp.dot(x_hi, w_hi, preferred_element_type=F32)
              + jnp.dot(x_lo, w_hi, preferred_element_type=F32)
              + jnp.dot(x_hi, w_lo, preferred_element_type=F32))
    lane = lax.broadcasted_iota(jnp.int32, (t, LANES), 1).astype(F32)
    lowest = jnp.finfo(F32).min
    l1 = jnp.where(lane < N_EXPERTS, logits, lowest)
    m1 = jnp.max(l1, axis=-1, keepdims=True)
    i1 = jnp.min(jnp.where(l1 == m1, lane, float(LANES)), axis=-1, keepdims=True)
    l2 = jnp.where(lane == i1, lowest, l1)
    m2 = jnp.max(l2, axis=-1, keepdims=True)
    i2 = jnp.min(jnp.where(l2 == m2, lane, float(LANES)), axis=-1, keepdims=True)
    e = jnp.exp(m2 - m1)
    w1 = 1.0 / (1.0 + e)
    w2 = e / (1.0 + e)

    chosen = jnp.logical_or(lane == i1, lane == i2)
    sel = jnp.where(chosen, 1.0, 0.0).astype(BF16)
    r_i = lax.broadcasted_iota(jnp.int32, (t, t), 0)
    c_i = lax.broadcasted_iota(jnp.int32, (t, t), 1)
    earlier = jnp.where(c_i < r_i, 1.0, 0.0).astype(BF16)
    before = jnp.dot(earlier, sel, preferred_element_type=F32) + carry_ref[...]
    rank1 = jnp.sum(jnp.where(lane == i1, before, 0.0), axis=-1, keepdims=True)
    rank2 = jnp.sum(jnp.where(lane == i2, before, 0.0), axis=-1, keepdims=True)
    total = carry_ref[...] + jnp.sum(sel.astype(F32), axis=0, keepdims=True)
    carry_ref[...] = total
    count_ref[...] = total

    route = jnp.where(lane == _L_IDX, i1, 0.0)
    route = jnp.where(lane == _L_IDX + 1, i2, route)
    route = jnp.where(lane == _L_W, w1, route)
    route = jnp.where(lane == _L_W + 1, w2, route)
    route = jnp.where(lane == _L_RANK, rank1, route)
    route = jnp.where(lane == _L_RANK + 1, rank2, route)
    route_ref[...] = route
    meta_ref[...] = route.T[:SUBLANES]


def _router_block(ya, yl, x2d, wo, g1, b1, rw, alpha):
    n = x2d.shape[0]
    tile = lambda w: pl.BlockSpec((TOKEN_TILE, w), lambda i: (i, 0))
    return pl.pallas_call(
        functools.partial(_router_kernel, alpha=alpha),
        grid=(n // TOKEN_TILE,),
        in_specs=[tile(ATTN_WIDTH), tile(LRU_WIDTH), tile(D_MODEL),
                  _resident((D_MODEL, D_MODEL)), _resident((1, D_MODEL)), _resident((1, D_MODEL)),
                  _resident((D_MODEL, LANES))],
        out_specs=[pl.BlockSpec((TOKEN_TILE * ROW_TILE, LANES), lambda i: (i, 0)), tile(LANES),
                   pl.BlockSpec((SUBLANES, TOKEN_TILE), lambda i: (0, i)),
                   pl.BlockSpec((1, LANES), lambda i: (0, 0))],
        out_shape=[jax.ShapeDtypeStruct((n * ROW_TILE, LANES), F32),
                   jax.ShapeDtypeStruct((n, LANES), F32),
                   jax.ShapeDtypeStruct((SUBLANES, n), F32),
                   jax.ShapeDtypeStruct((1, LANES), F32)],
        scratch_shapes=[pltpu.VMEM((1, LANES), F32)],
        compiler_params=_params(("arbitrary",)),
        name="router_block",
    )(ya, yl, x2d, wo, g1, b1, rw)


def _tile_copy(src_ref, src_row, dst_ref, dst_row, sem):
    src = src_ref.at[pl.ds(pl.multiple_of(src_row * ROW_TILE, ROW_TILE), ROW_TILE)]
    dst = dst_ref.at[pl.ds(pl.multiple_of(dst_row * ROW_TILE, ROW_TILE), ROW_TILE)]
    return pltpu.make_async_copy(src, dst, sem)


def _dispatch_kernel(pad_start_ref, pad_len_ref, pos0_ref, pos1_ref, x_ref, xs_ref, sem, pad_sem):
    def issue(t, carry):
        _tile_copy(x_ref, t, xs_ref, pos0_ref[t], sem).start(priority=0)
        _tile_copy(x_ref, t, xs_ref, pos1_ref[t], sem).start(priority=1)
        return carry

    lax.fori_loop(0, DISPATCH_TILE, issue, 0, unroll=DMA_UNROLL)
    for _ in range(2):
        pltpu.make_async_copy(x_ref, xs_ref.at[pl.ds(0, DISPATCH_TILE * ROW_TILE)], sem).wait()

    @pl.when(pl.program_id(0) == pl.num_programs(0) - 1)
    def _():
        for e in range(N_EXPERTS):
            def fill(r, carry):
                _tile_copy(x_ref, 0, xs_ref, pad_start_ref[e] + r, pad_sem).start()
                return carry

            def drain(r, carry):
                _tile_copy(x_ref, 0, xs_ref, 0, pad_sem).wait()
                return carry

            lax.fori_loop(0, pad_len_ref[e], fill, 0)
            lax.fori_loop(0, pad_len_ref[e], drain, 0)

        def idle_copy(r):
            first = pl.multiple_of((pad_start_ref[N_EXPERTS] + r * MOE_TILE) * ROW_TILE, MOE_TILE * ROW_TILE)
            return pltpu.make_async_copy(x_ref.at[pl.ds(0, MOE_TILE * ROW_TILE)],
                                         xs_ref.at[pl.ds(first, MOE_TILE * ROW_TILE)], pad_sem)

        def fill_idle(r, carry):
            idle_copy(r).start()
            return carry

        def drain_idle(r, carry):
            idle_copy(r).wait()
            return carry

        lax.fori_loop(0, pad_len_ref[N_EXPERTS], fill_idle, 0)
        lax.fori_loop(0, pad_len_ref[N_EXPERTS], drain_idle, 0)


def _dispatch(pad_start, pad_len, pos0, pos1, x1, sorted_rows):
    n = x1.shape[0] // ROW_TILE
    assert DISPATCH_TILE >= MOE_TILE
    smem_tile = pl.BlockSpec((DISPATCH_TILE,), lambda i, *_: (i,), memory_space=pltpu.SMEM)
    grid_spec = pltpu.PrefetchScalarGridSpec(
        num_scalar_prefetch=2,
        grid=(n // DISPATCH_TILE,),
        in_specs=[smem_tile, smem_tile,
                  pl.BlockSpec((DISPATCH_TILE * ROW_TILE, LANES), lambda i, *_: (i, 0))],
        out_specs=pl.BlockSpec(memory_space=pl.ANY),
        scratch_shapes=[pltpu.SemaphoreType.DMA(()), pltpu.SemaphoreType.DMA(())],
    )
    return pl.pallas_call(
        _dispatch_kernel,
        grid_spec=grid_spec,
        out_shape=jax.ShapeDtypeStruct((sorted_rows * ROW_TILE, LANES), F32),
        compiler_params=_params(("arbitrary",), disable_bounds_checks=True),
        name="moe_dispatch",
    )(pad_start, pad_len, pos0, pos1, x1)


def _experts_kernel(tile_expert_ref, n_used_ref, x_ref, wg_ref, wu_ref, wd_ref, y_ref, xb_ref, acc_ref):
    del tile_expert_ref
    i = pl.program_id(0)
    j = pl.program_id(1)
    last = pl.num_programs(1) - 1

    @pl.when(i < n_used_ref[0])
    def _():
        @pl.when(j == 0)
        def _():
            for c in range(ROW_TILE):
                xb_ref[:, c * LANES:(c + 1) * LANES] = (
                    x_ref[pl.ds(c, MOE_TILE, stride=ROW_TILE), :].astype(BF16))

        xb = xb_ref[...]
        g = jnp.dot(xb, wg_ref[...], preferred_element_type=F32)
        u = jnp.dot(xb, wu_ref[...], preferred_element_type=F32)
        hid = (jax.nn.silu(g) * u).astype(BF16)
        part = jnp.dot(hid, wd_ref[...], preferred_element_type=F32)

        @pl.when(j == 0)
        def _():
            acc_ref[...] = part

        @pl.when(jnp.logical_and(j > 0, j < last))
        def _():
            acc_ref[...] += part

        @pl.when(j == last)
        def _():
            _store_row_tiles(y_ref, acc_ref[...] + part)

    @pl.when(jnp.logical_and(i >= n_used_ref[0], j == 0))
    def _():
        y_ref[...] = jnp.zeros_like(y_ref)


def _experts(tile_expert, n_used, xs, wg, wu, wd):
    rows = xs.shape[0] // ROW_TILE
    d_ff = wg.shape[2]
    n_tiles = rows // MOE_TILE
    n_f = d_ff // MOE_F_TILE
    assert n_f >= 2

    def row_map(i, j, te, nu):
        return (jnp.minimum(i, nu[0] - 1), 0)

    def f_of(i, j, nu):
        return jnp.where(i < nu[0], j, n_f - 1)

    grid_spec = pltpu.PrefetchScalarGridSpec(
        num_scalar_prefetch=2,
        grid=(n_tiles, n_f),
        in_specs=[pl.BlockSpec((MOE_TILE * ROW_TILE, LANES), row_map),
                  pl.BlockSpec((None, D_MODEL, MOE_F_TILE), lambda i, j, te, nu: (te[i], 0, f_of(i, j, nu))),
                  pl.BlockSpec((None, D_MODEL, MOE_F_TILE), lambda i, j, te, nu: (te[i], 0, f_of(i, j, nu))),
                  pl.BlockSpec((None, MOE_F_TILE, D_MODEL), lambda i, j, te, nu: (te[i], f_of(i, j, nu), 0))],
        out_specs=pl.BlockSpec((MOE_TILE * ROW_TILE, LANES), lambda i, j, te, nu: (i, 0)),
        scratch_shapes=[pltpu.VMEM((MOE_TILE, D_MODEL), BF16), pltpu.VMEM((MOE_TILE, D_MODEL), F32)],
    )
    return pl.pallas_call(
        _experts_kernel,
        grid_spec=grid_spec,
        out_shape=jax.ShapeDtypeStruct((rows * ROW_TILE, LANES), F32),
        compiler_params=_params(("arbitrary", "arbitrary")),
        name="moe_experts",
    )(tile_expert, n_used, xs, wg, wu, wd)


def _combine_kernel(pos0_ref, pos1_ref, nxt0_ref, nxt1_ref, ys_ref, x1_ref, route_ref, g2_ref, b2_ref,
                    o_ref, buf_ref, sem, *, alpha):
    i = pl.program_id(0)
    slot = i % 2

    def issue(p0_ref, p1_ref, dst_slot):
        def body(t, carry):
            for k, p_ref in enumerate((p0_ref, p1_ref)):
                _tile_copy(ys_ref, p_ref[t], buf_ref.at[dst_slot, k], t, sem.at[dst_slot]).start(priority=k)
            return carry
        lax.fori_loop(0, COMBINE_TILE, body, 0, unroll=DMA_UNROLL)

    @pl.when(i == 0)
    def _():
        issue(pos0_ref, pos1_ref, 0)

    @pl.when(i + 1 < pl.num_programs(0))
    def _():
        issue(nxt0_ref, nxt1_ref, 1 - slot)

    for k in range(2):
        pltpu.make_async_copy(ys_ref.at[pl.ds(0, COMBINE_TILE * ROW_TILE)], buf_ref.at[slot, k],
                              sem.at[slot]).wait()
    route = route_ref[...]
    y0 = _load_row_tiles(buf_ref, COMBINE_TILE, (slot, 0))
    y1 = _load_row_tiles(buf_ref, COMBINE_TILE, (slot, 1))
    f = route[:, _L_W:_L_W + 1] * y0 + route[:, _L_W + 1:_L_W + 2] * y1
    x1 = _load_row_tiles(x1_ref, COMBINE_TILE)
    o_ref[...] = _layer_norm(alpha * x1 + f, g2_ref[...], b2_ref[...])


def _combine(pos0, pos1, ys, x1, route, g2, b2, alpha):
    n = x1.shape[0] // ROW_TILE
    steps = n // COMBINE_TILE
    tile = lambda w: pl.BlockSpec((COMBINE_TILE, w), lambda i: (i, 0))
    smem_cur = pl.BlockSpec((COMBINE_TILE,), lambda i: (i,), memory_space=pltpu.SMEM)
    smem_next = pl.BlockSpec((COMBINE_TILE,), lambda i: (jnp.minimum(i + 1, steps - 1),),
                             memory_space=pltpu.SMEM)
    return pl.pallas_call(
        functools.partial(_combine_kernel, alpha=alpha),
        grid=(steps,),
        in_specs=[smem_cur, smem_cur, smem_next, smem_next,
                  pl.BlockSpec(memory_space=pl.ANY),
                  pl.BlockSpec((COMBINE_TILE * ROW_TILE, LANES), lambda i: (i, 0)),
                  tile(LANES), _resident((1, D_MODEL)), _resident((1, D_MODEL))],
        out_specs=tile(D_MODEL),
        out_shape=jax.ShapeDtypeStruct((n, D_MODEL), F32),
        scratch_shapes=[pltpu.VMEM((2, 2, COMBINE_TILE * ROW_TILE, LANES), F32),
                        pltpu.SemaphoreType.DMA((2,))],
        compiler_params=_params(("arbitrary",), disable_bounds_checks=True),
        name="moe_combine",
    )(pos0, pos1, pos0, pos1, ys, x1, route, g2, b2)


def _moe_block(x1, route, meta, counts, wg, wu, wd, g2, b2, alpha):
    n = route.shape[0]
    n_tiles = (2 * n) // MOE_TILE + N_EXPERTS
    count = counts[0, :N_EXPERTS].astype(jnp.int32)
    tiles_per = (count + MOE_TILE - 1) // MOE_TILE
    tile_end = jnp.cumsum(tiles_per)
    group_start = (tile_end - tiles_per) * MOE_TILE
    n_used = tile_end[-1:]
    tile_id = jnp.minimum(jnp.arange(n_tiles, dtype=jnp.int32), n_used[0] - 1)
    tile_expert = jnp.sum((tile_id[:, None] >= tile_end[None, :]).astype(jnp.int32), axis=1)
    idx = meta[_L_IDX:_L_IDX + 2].astype(jnp.int32)
    rank = meta[_L_RANK:_L_RANK + 2].astype(jnp.int32)
    start = jnp.sum(jnp.where(idx[:, :, None] == jnp.arange(N_EXPERTS), group_start, 0), axis=-1)
    pos = start + rank

    pad_start = jnp.concatenate([group_start + count, n_used * MOE_TILE])
    pad_len = jnp.concatenate([tiles_per * MOE_TILE - count, n_tiles - n_used])
    xs = _dispatch(pad_start, pad_len, pos[0], pos[1], x1, n_tiles * MOE_TILE)
    ys = _experts(tile_expert, n_used, xs, wg, wu, wd)
    return _combine(pos[0], pos[1], ys, x1, route, g2, b2, alpha)


def _t5_bucket(rel):
    half = N_BUCKETS // 2
    max_exact = half // 2
    ret = jnp.where(rel > 0, half, 0)
    n = jnp.abs(rel)
    large = max_exact + (jnp.log(jnp.maximum(n, 1).astype(F32) / max_exact)
                         / math.log(MAX_DISTANCE / max_exact) * (half - max_exact)).astype(jnp.int32)
    large = jnp.minimum(large, half - 1)
    return ret + jnp.where(n < max_exact, n, large)


def _block_bias(rel_bias):
    qi = jnp.arange(Q_BLOCK)[:, None]
    kj = jnp.arange(K_BLOCK)[None, :]
    rel = (kj - WINDOW_CHUNKS * CHUNK) - qi
    hit = _t5_bucket(rel)[None, :, :, None] == jnp.arange(N_BUCKETS)
    bias = jnp.sum(jnp.where(hit, rel_bias.T[:, None, None, :], 0.0), axis=-1)
    first = (kj // CHUNK) - (qi // CHUNK)
    visible = jnp.logical_and(first >= 0, first <= WINDOW_CHUNKS)
    return jnp.where(visible[None], bias, NEG_INF).astype(F32) * LOG2E


def _in_weights(w_in):
    q = w_in[:, :ATTN_WIDTH] * (HEAD_DIM ** -0.5 * LOG2E)
    k = w_in[:, ATTN_WIDTH:ATTN_WIDTH + KV_COLS]
    v = w_in[:, ATTN_WIDTH + KV_COLS:ATTN_WIDTH + 2 * KV_COLS]
    rest = w_in[:, ATTN_WIDTH + 2 * KV_COLS:]
    swap = lambda w: jnp.concatenate([w[:, HEAD_DIM:], w[:, :HEAD_DIM]], axis=1)
    return jnp.concatenate([q, k, swap(k), v, swap(v), rest], axis=1).astype(BF16)


def _block_diag(w):
    nb, d, e = w.shape
    eye = jnp.eye(nb, dtype=w.dtype)
    return (eye[:, None, :, None] * w[:, :, None, :]).reshape(nb * d, nb * e)


def kernel(x, rel_bias, w_in, attn_sinks, conv_w, conv_b, gate_r_w, gate_r_b, gate_i_w, gate_i_b,
           lru_lambda, norm_attn_g, norm_lru_g, w_out, ln1_g, ln1_b, ffn_w_gate, ffn_w_up, ffn_w_down,
           router_w, exp_w_gate, exp_w_up, exp_w_down, ln2_g, ln2_b):
    batch, seq, d = x.shape
    depth = w_in.shape[0]
    alpha = float((2 * depth) ** 0.25)
    bias2 = _block_bias(rel_bias)
    row = lambda v: v.reshape(1, -1)

    h = x.reshape(batch * seq, d)
    for l in range(depth):
        qkv, lru = _inproj(h, _in_weights(w_in[l]))
        ya = _attention(qkv, attn_sinks[l] * LOG2E, bias2, row(norm_attn_g[l]), batch, seq)
        w_gates = jnp.concatenate([_block_diag(gate_r_w[l]), _block_diag(gate_i_w[l])], axis=1).astype(BF16)
        b_gates = jnp.concatenate([gate_r_b[l], gate_i_b[l]]).reshape(1, -1)
        yl = _rg_lru(lru, conv_w[l], row(conv_b[l]), w_gates, b_gates, row(lru_lambda[l]),
                     row(norm_lru_g[l]), batch, seq)
        wo = w_out[l].astype(BF16)
        if l % 2 == 0:
            h = _dense_block(ya, yl, h, wo, row(ln1_g[l]), row(ln1_b[l]),
                             ffn_w_gate[l // 2].astype(BF16), ffn_w_up[l // 2].astype(BF16),
                             ffn_w_down[l // 2].astype(BF16), row(ln2_g[l]), row(ln2_b[l]), alpha)
        else:
            rw = jnp.pad(router_w[l // 2], ((0, 0), (0, LANES - N_EXPERTS)))
            x1, route, meta, counts = _router_block(ya, yl, h, wo, row(ln1_g[l]), row(ln1_b[l]), rw, alpha)
            h = _moe_block(x1, route, meta, counts, exp_w_gate[l // 2].astype(BF16),
                           exp_w_up[l // 2].astype(BF16), exp_w_down[l // 2].astype(BF16),
                           row(ln2_g[l]), row(ln2_b[l]), alpha)
    return h.reshape(batch, seq, d)
```

```python
import functools
import math

import jax
import jax.numpy as jnp
from jax import lax
from jax.experimental import pallas as pl
from jax.experimental.pallas import tpu as pltpu

F32 = jnp.float32
BF16 = jnp.bfloat16

D_MODEL = 1024
CHUNK = 64
HEAD_DIM = 64
N_Q_HEADS = 8
N_KV_HEADS = 2
ATTN_WIDTH = N_Q_HEADS * HEAD_DIM
KV_COLS = N_KV_HEADS * HEAD_DIM
WINDOW_CHUNKS = 2
LRU_WIDTH = 512
LRU_BLOCKS = 8
CONV_WIDTH = 4
LRU_C = 8.0
N_BUCKETS = 32
MAX_DISTANCE = 128
N_EXPERTS = 8
NEG_INF = -1e30
LN_EPS = 1e-5
RMS_EPS = 1e-6
LOG2E = math.log2(math.e)

LANES = 128
SUBLANES = 8
VMEM_LIMIT = 56 * 1024 * 1024

QKV_COLS = ATTN_WIDTH + 4 * KV_COLS
Q_BLOCK = 2 * CHUNK
K_BLOCK = (WINDOW_CHUNKS + 2) * CHUNK
ATTN_TILE = 512
LRU_TILE = 512
TOKEN_TILE = 512
ROUTER_SLABS = 2
MXU_TILE = 256
MOE_TILE = 512
MOE_F_TILE = 7 * MXU_TILE
DISPATCH_TILE = 1024
COMBINE_TILE = 256
DMA_UNROLL = 8


def _params(sem, **kw):
    return pltpu.CompilerParams(dimension_semantics=sem, vmem_limit_bytes=VMEM_LIMIT, **kw)


def _resident(shape):
    nd = len(shape)
    return pl.BlockSpec(shape, lambda *_: (0,) * nd, pipeline_mode=pl.Buffered(1))


def _layer_norm(x, g, b):
    mu = jnp.mean(x, axis=-1, keepdims=True)
    xc = x - mu
    var = jnp.mean(xc * xc, axis=-1, keepdims=True)
    return xc * lax.rsqrt(var + LN_EPS) * g + b


def _rms_norm(x, g):
    return x * lax.rsqrt(jnp.mean(x * x, axis=-1, keepdims=True) + RMS_EPS) * g


ROW_TILE = D_MODEL // LANES


def _load_row_tiles(ref, rows, lead=()):
    return jnp.concatenate(
        [ref[lead + (pl.ds(c, rows, stride=ROW_TILE), slice(None))] for c in range(ROW_TILE)], axis=1)


def _store_row_tiles(ref, value, first_row=0):
    for c in range(ROW_TILE):
        ref[pl.ds(first_row * ROW_TILE + c, value.shape[0], stride=ROW_TILE), :] = (
            value[:, c * LANES:(c + 1) * LANES])


BF16_ROWS = 2 * SUBLANES


def _with_casts(kernel_fn, n_in, n_out, n_casts):
    def wrapped(*refs):
        ins, refs = refs[:n_in], refs[n_in:]
        srcs, refs = refs[:n_casts], refs[n_casts:]
        outs, refs = refs[:n_out], refs[n_out:]
        dsts, scratch = refs[:n_casts], refs[n_casts:]
        for src, dst in zip(srcs, dsts):
            dst[...] = src[...].astype(BF16)
        kernel_fn(*ins, *outs, *scratch)
    return wrapped


def _cast_specs(weights, linear_step, steps):
    specs, shapes = [], []
    for w in weights:
        rows, cols = w.shape
        per = next(p for p in range(BF16_ROWS, rows + 1, BF16_ROWS) if rows % p == 0 and p * steps >= rows)
        last = rows // per - 1
        specs.append(pl.BlockSpec((per, cols),
                                  lambda *g, last=last: (jnp.minimum(linear_step(*g), last), 0)))
        shapes.append(jax.ShapeDtypeStruct(w.shape, BF16))
    return specs, shapes


def _inproj_kernel(x_ref, w_ref, qkv_ref, lru_ref):
    z = jnp.dot(x_ref[...].astype(BF16), w_ref[...], preferred_element_type=F32)
    qkv_ref[...] = z[:, :QKV_COLS].astype(BF16)
    lru_ref[...] = z[:, QKV_COLS:]


def _inproj(x2d, w, casts=()):
    n = x2d.shape[0]
    cols = w.shape[1]
    steps = n // TOKEN_TILE
    cast_specs, cast_shapes = _cast_specs(casts, lambda i: i, steps)
    return pl.pallas_call(
        _with_casts(_inproj_kernel, 2, 2, len(casts)),
        grid=(steps,),
        in_specs=[pl.BlockSpec((TOKEN_TILE, D_MODEL), lambda i: (i, 0)), _resident((D_MODEL, cols))]
        + cast_specs,
        out_specs=[pl.BlockSpec((TOKEN_TILE, QKV_COLS), lambda i: (i, 0)),
                   pl.BlockSpec((TOKEN_TILE, 2 * LRU_WIDTH), lambda i: (i, 0))] + cast_specs,
        out_shape=[jax.ShapeDtypeStruct((n, QKV_COLS), BF16),
                   jax.ShapeDtypeStruct((n, 2 * LRU_WIDTH), F32)] + cast_shapes,
        compiler_params=_params(("arbitrary",)),
        name="inproj",
    )(x2d, w, *casts)


def _attn_kernel(sink_ref, q_ref, k_ref, ks_ref, v_ref, vs_ref, kp_ref, ksp_ref, vp_ref, vsp_ref,
                 bias_ref, g_ref, o_ref, kk_ref, vv_ref, acc_ref):
    step = pl.program_id(1)
    hist = Q_BLOCK
    lane = lax.broadcasted_iota(jnp.int32, (1, LANES), 1)
    lo = lane < HEAD_DIM
    zero = jnp.zeros((), BF16)

    def fill(dst, plain_prev, plain, swap_prev, swap):
        for rows, plain_ref, swap_ref in ((slice(0, hist), plain_prev, swap_prev),
                                          (slice(hist, hist + ATTN_TILE), plain, swap)):
            a = plain_ref[...]
            s = swap_ref[...]
            dst[0, rows, :] = jnp.where(lo, a, zero)
            dst[1, rows, :] = jnp.where(lo, zero, s)
            dst[2, rows, :] = jnp.where(lo, s, zero)
            dst[3, rows, :] = jnp.where(lo, zero, a)

    fill(kk_ref, kp_ref, k_ref, ksp_ref, ks_ref)
    fill(vv_ref, vp_ref, v_ref, vsp_ref, vs_ref)

    col = lax.broadcasted_iota(jnp.int32, (1, K_BLOCK), 1)
    dn = (((1,), (1,)), ((), ()))

    for s in range(ATTN_TILE // Q_BLOCK):
        q0 = s * Q_BLOCK
        start_mask = None
        if s == 0:
            before_start = jnp.logical_and(step == 0, col < hist)
            start_mask = jnp.where(before_start, NEG_INF, 0.0).astype(F32)
        for h in range(N_KV_HEADS):
            q2 = jnp.concatenate(
                [q_ref[pl.ds(q0, Q_BLOCK), pl.ds((2 * h + p) * LANES, LANES)] for p in range(2)], axis=0)
            probs = []
            inv_den = []
            for half in range(2):
                kk = kk_ref[2 * h + half, pl.ds(q0, K_BLOCK), :]
                sc = lax.dot_general(q2, kk, dn, preferred_element_type=F32)
                for p in range(2):
                    head = 4 * h + 2 * p + half
                    logit = sc[p * Q_BLOCK:(p + 1) * Q_BLOCK] + bias_ref[head]
                    if start_mask is not None:
                        logit = logit + start_mask
                    sink = sink_ref[head]
                    m = jnp.maximum(jnp.max(logit, axis=-1, keepdims=True), sink)
                    e = jnp.exp2(logit - m)
                    den = jnp.sum(e, axis=-1, keepdims=True) + jnp.exp2(sink - m)
                    probs.append((p, half, e.astype(BF16)))
                    inv_den.append((p, half, 1.0 / den))
            pmap = {(p, half): e for p, half, e in probs}
            dmap = {(p, half): d for p, half, d in inv_den}
            lhs = jnp.concatenate(
                [jnp.concatenate([pmap[(p, 0)], pmap[(p, 1)]], axis=1) for p in range(2)], axis=0)
            rhs = jnp.concatenate([vv_ref[2 * h, pl.ds(q0, K_BLOCK), :],
                                   vv_ref[2 * h + 1, pl.ds(q0, K_BLOCK), :]], axis=0)
            out = jnp.dot(lhs, rhs, preferred_element_type=F32)
            for p in range(2):
                scale = jnp.where(lo, dmap[(p, 0)], dmap[(p, 1)])
                acc_ref[pl.ds(q0, Q_BLOCK), pl.ds((2 * h + p) * LANES, LANES)] = (
                    out[p * Q_BLOCK:(p + 1) * Q_BLOCK] * scale)

    o_ref[...] = _rms_norm(acc_ref[...], g_ref[...]).astype(BF16)


def _attention(qkv, sinks, bias2, g_attn, batch, seq, casts=()):
    n = qkv.shape[0]
    steps = seq // ATTN_TILE
    ratio = ATTN_TILE // Q_BLOCK
    qcb = ATTN_WIDTH // LANES
    cast_specs, cast_shapes = _cast_specs(casts, lambda b, i: b * steps + i, batch * steps)

    def cur(cb):
        return pl.BlockSpec((ATTN_TILE, LANES), lambda b, i: (b * steps + i, cb))

    def prev(cb):
        return pl.BlockSpec((Q_BLOCK, LANES),
                            lambda b, i: (b * steps * ratio + jnp.maximum(i * ratio - 1, 0), cb))

    return pl.pallas_call(
        _with_casts(_attn_kernel, 12, 1, len(casts)),
        grid=(batch, steps),
        in_specs=[pl.BlockSpec(memory_space=pltpu.SMEM),
                  pl.BlockSpec((ATTN_TILE, ATTN_WIDTH), lambda b, i: (b * steps + i, 0)),
                  cur(qcb), cur(qcb + 1), cur(qcb + 2), cur(qcb + 3),
                  prev(qcb), prev(qcb + 1), prev(qcb + 2), prev(qcb + 3),
                  _resident((N_Q_HEADS, Q_BLOCK, K_BLOCK)),
                  _resident((1, ATTN_WIDTH))] + cast_specs,
        out_specs=[pl.BlockSpec((ATTN_TILE, ATTN_WIDTH), lambda b, i: (b * steps + i, 0))] + cast_specs,
        out_shape=[jax.ShapeDtypeStruct((n, ATTN_WIDTH), BF16)] + cast_shapes,
        scratch_shapes=[pltpu.VMEM((4, Q_BLOCK + ATTN_TILE, LANES), BF16),
                        pltpu.VMEM((4, Q_BLOCK + ATTN_TILE, LANES), BF16),
                        pltpu.VMEM((ATTN_TILE, ATTN_WIDTH), F32)],
        compiler_params=_params(("arbitrary", "arbitrary")),
        name="attention",
    )(sinks, qkv, qkv, qkv, qkv, qkv, qkv, qkv, qkv, qkv, bias2, g_attn, *casts)


def _lru_kernel(x_ref, gate_ref, prev_ref, cw_ref, cb_ref, wg_ref, bg_ref, lam_ref, g_ref,
                o_ref, h_ref, a_ref, u_ref, ext_ref):
    step = pl.program_id(1)

    @pl.when(step == 0)
    def _():
        h_ref[...] = jnp.zeros_like(h_ref)

    x = x_ref[...]
    t = x.shape[0]
    prev = jnp.where(step == 0, 0.0, prev_ref[...])

    ext_ref[:SUBLANES, :] = prev
    ext_ref[SUBLANES:, :] = x
    xc = x * cw_ref[CONV_WIDTH - 1:CONV_WIDTH, :] + cb_ref[...]
    for k in range(1, CONV_WIDTH):
        xc = xc + ext_ref[SUBLANES - k:SUBLANES - k + t, :] * cw_ref[CONV_WIDTH - 1 - k:CONV_WIDTH - k, :]

    gates = jnp.dot(xc.astype(BF16), wg_ref[...], preferred_element_type=F32) + bg_ref[...]
    r = jax.nn.sigmoid(gates[:, :LRU_WIDTH])
    i = jax.nn.sigmoid(gates[:, LRU_WIDTH:])
    neg_lam = -lam_ref[...]
    softplus = jnp.maximum(neg_lam, 0.0) + jnp.log1p(jnp.exp(-jnp.abs(neg_lam)))
    log_a = (-LRU_C) * r * softplus
    a = jnp.exp(log_a)
    h = jnp.sqrt(jnp.tanh(-log_a) * (1.0 + a * a)) * (i * xc)

    def doubling(av, hv, index, length, axis):
        d = 1
        while d < length:
            keep = index >= d
            a_sh = jnp.where(keep, pltpu.roll(av, d, axis), 1.0)
            h_sh = jnp.where(keep, pltpu.roll(hv, d, axis), 0.0)
            hv = av * h_sh + hv
            av = av * a_sh
            d *= 2
        return av, hv

    groups = t // SUBLANES
    grouped = (groups, SUBLANES, LRU_WIDTH)
    sub = lax.broadcasted_iota(jnp.int32, (1, SUBLANES, 1), 1)
    a, h = doubling(a.reshape(grouped), h.reshape(grouped), sub, SUBLANES, 1)
    a = a.reshape(t, LRU_WIDTH)
    h = h.reshape(t, LRU_WIDTH)
    grow = lax.broadcasted_iota(jnp.int32, (groups, 1), 0)

    def group_last(ref, value):
        for c in range(LRU_WIDTH // LANES):
            ref[c] = value[:, c * LANES:(c + 1) * LANES]
        return jnp.concatenate([ref[c, pl.ds(SUBLANES - 1, groups, stride=SUBLANES), :]
                                for c in range(LRU_WIDTH // LANES)], axis=1)

    ag, hg = doubling(group_last(a_ref, a), group_last(u_ref, h), grow, groups, 0)
    state = hg + ag * h_ref[...]
    enter = jnp.where(grow >= 1, pltpu.roll(state, 1, 0), h_ref[...])
    h_ref[...] = state[groups - 1:groups, :]
    enter = jnp.broadcast_to(enter[:, None, :], (groups, SUBLANES, LRU_WIDTH)).reshape(t, LRU_WIDTH)
    h = h + a * enter

    y = jax.nn.gelu(gate_ref[...]) * h
    o_ref[...] = _rms_norm(y, g_ref[...]).astype(BF16)


def _rg_lru(lru, conv_w, conv_b, w_gates, b_gates, lam, g_lru, batch, seq, casts=()):
    n = lru.shape[0]
    steps = seq // LRU_TILE
    ratio = LRU_TILE // SUBLANES
    cast_specs, cast_shapes = _cast_specs(casts, lambda b, i: b * steps + i, batch * steps)
    return pl.pallas_call(
        _with_casts(_lru_kernel, 9, 1, len(casts)),
        grid=(batch, steps),
        in_specs=[pl.BlockSpec((LRU_TILE, LRU_WIDTH), lambda b, i: (b * steps + i, 0)),
                  pl.BlockSpec((LRU_TILE, LRU_WIDTH), lambda b, i: (b * steps + i, 1)),
                  pl.BlockSpec((SUBLANES, LRU_WIDTH),
                               lambda b, i: (b * steps * ratio + jnp.maximum(i * ratio - 1, 0), 0)),
                  _resident((CONV_WIDTH, LRU_WIDTH)),
                  _resident((1, LRU_WIDTH)),
                  _resident((LRU_WIDTH, 2 * LRU_WIDTH)),
                  _resident((1, 2 * LRU_WIDTH)),
                  _resident((1, LRU_WIDTH)),
                  _resident((1, LRU_WIDTH))] + cast_specs,
        out_specs=[pl.BlockSpec((LRU_TILE, LRU_WIDTH), lambda b, i: (b * steps + i, 0))] + cast_specs,
        out_shape=[jax.ShapeDtypeStruct((n, LRU_WIDTH), BF16)] + cast_shapes,
        scratch_shapes=[pltpu.VMEM((1, LRU_WIDTH), F32),
                        pltpu.VMEM((LRU_WIDTH // LANES, LRU_TILE, LANES), F32),
                        pltpu.VMEM((LRU_WIDTH // LANES, LRU_TILE, LANES), F32),
                        pltpu.VMEM((SUBLANES + LRU_TILE, LRU_WIDTH), F32)],
        compiler_params=_params(("arbitrary", "arbitrary")),
        name="rg_lru",
    )(lru, lru, lru, conv_w, conv_b, w_gates, b_gates, lam, g_lru, *casts)


def _mix_ln1(ya_ref, yl_ref, x_ref, wo_ref, g1_ref, b1_ref, alpha, rows=slice(None)):
    y = jnp.dot(ya_ref[rows, :], wo_ref[:ATTN_WIDTH, :], preferred_element_type=F32)
    y = y + jnp.dot(yl_ref[rows, :], wo_ref[ATTN_WIDTH:, :], preferred_element_type=F32)
    return _layer_norm(alpha * x_ref[rows, :] + y, g1_ref[...], b1_ref[...])


def _dense_block_kernel(ya_ref, yl_ref, x_ref, wo_ref, g1_ref, b1_ref, wg_ref, wu_ref, wd_ref,
                        g2_ref, b2_ref, o_ref, *, alpha, f_splits):
    x1 = _mix_ln1(ya_ref, yl_ref, x_ref, wo_ref, g1_ref, b1_ref, alpha)
    xb = x1.astype(BF16)
    f = None
    for lo, hi in zip(f_splits[:-1], f_splits[1:]):
        cols = slice(lo, hi)
        g = jnp.dot(xb, wg_ref[:, cols], preferred_element_type=F32)
        u = jnp.dot(xb, wu_ref[:, cols], preferred_element_type=F32)
        hid = (jax.nn.silu(g) * u).astype(BF16)
        part = jnp.dot(hid, wd_ref[cols, :], preferred_element_type=F32)
        f = part if f is None else f + part
    o_ref[...] = _layer_norm(alpha * x1 + f, g2_ref[...], b2_ref[...])


def _dense_block(ya, yl, x2d, wo, g1, b1, wg, wu, wd, g2, b2, alpha):
    n = x2d.shape[0]
    d_ff = wg.shape[1]
    tile = lambda w: pl.BlockSpec((TOKEN_TILE, w), lambda i: (i, 0))
    mxu_tiles = pl.cdiv(d_ff, MXU_TILE)
    f_splits = (0, (mxu_tiles + 1) // 2 * MXU_TILE, d_ff)
    return pl.pallas_call(
        functools.partial(_dense_block_kernel, alpha=alpha, f_splits=f_splits),
        grid=(n // TOKEN_TILE,),
        in_specs=[tile(ATTN_WIDTH), tile(LRU_WIDTH), tile(D_MODEL),
                  _resident((D_MODEL, D_MODEL)), _resident((1, D_MODEL)), _resident((1, D_MODEL)),
                  _resident((D_MODEL, d_ff)), _resident((D_MODEL, d_ff)), _resident((d_ff, D_MODEL)),
                  _resident((1, D_MODEL)), _resident((1, D_MODEL))],
        out_specs=tile(D_MODEL),
        out_shape=jax.ShapeDtypeStruct((n, D_MODEL), F32),
        compiler_params=_params(("parallel",)),
        name="dense_block",
    )(ya, yl, x2d, wo, g1, b1, wg, wu, wd, g2, b2)


_L_IDX, _L_W, _L_RANK = 0, 2, 4


def _router_kernel(ya_ref, yl_ref, x_ref, wo_ref, g1_ref, b1_ref, rw_ref,
                   x1_ref, route_ref, meta_ref, count_ref, carry_ref, *, alpha):
    @pl.when(pl.program_id(0) == 0)
    def _():
        carry_ref[...] = jnp.zeros_like(carry_ref)

    t = x_ref.shape[0] // ROUTER_SLABS
    carry = carry_ref[...]
    for s in range(ROUTER_SLABS):
        carry = _route_slab(s * t, t, carry, ya_ref, yl_ref, x_ref, wo_ref, g1_ref, b1_ref, rw_ref,
                            x1_ref, route_ref, meta_ref, alpha)
    carry_ref[...] = carry
    count_ref[...] = carry


def _route_slab(first, t, carry, ya_ref, yl_ref, x_ref, wo_ref, g1_ref, b1_ref, rw_ref,
                x1_ref, route_ref, meta_ref, alpha):
    x1 = _mix_ln1(ya_ref, yl_ref, x_ref, wo_ref, g1_ref, b1_ref, alpha, pl.ds(first, t))
    _store_row_tiles(x1_ref, x1, first)
    rw = rw_ref[...]
    x_hi = x1.astype(BF16)
    x_lo = (x1 - x_hi.astype(F32)).astype(BF16)
    w_hi = rw.astype(BF16)
    w_lo = (rw - w_hi.astype(F32)).astype(BF16)
    logits = (jnp.dot(x_hi, w_hi, preferred_element_type=F32)
              + jnp.dot(x_lo, w_hi, preferred_element_type=F32)
              + jnp.dot(x_hi, w_lo, preferred_element_type=F32))
    lane = lax.broadcasted_iota(jnp.int32, (t, LANES), 1).astype(F32)
    lowest = jnp.finfo(F32).min
    l1 = jnp.where(lane < N_EXPERTS, logits, lowest)
    m1 = jnp.max(l1, axis=-1, keepdims=True)
    i1 = jnp.min(jnp.where(l1 == m1, lane, float(LANES)), axis=-1, keepdims=True)
    l2 = jnp.where(lane == i1, lowest, l1)
    m2 = jnp.max(l2, axis=-1, keepdims=True)
    i2 = jnp.min(jnp.where(l2 == m2, lane, float(LANES)), axis=-1, keepdims=True)
    e = jnp.exp(m2 - m1)
    w1 = 1.0 / (1.0 + e)
    w2 = e / (1.0 + e)

    chosen = jnp.logical_or(lane == i1, lane == i2)
    sel = jnp.where(chosen, 1.0, 0.0).astype(BF16)
    r_i = lax.broadcasted_iota(jnp.int32, (t, t), 0)
    c_i = lax.broadcasted_iota(jnp.int32, (t, t), 1)
    earlier = jnp.where(c_i < r_i, 1.0, 0.0).astype(BF16)
    before = jnp.dot(earlier, sel, preferred_element_type=F32) + carry
    rank1 = jnp.sum(jnp.where(lane == i1, before, 0.0), axis=-1, keepdims=True)
    rank2 = jnp.sum(jnp.where(lane == i2, before, 0.0), axis=-1, keepdims=True)

    route = jnp.where(lane == _L_IDX, i1, 0.0)
    route = jnp.where(lane == _L_IDX + 1, i2, route)
    route = jnp.where(lane == _L_W, w1, route)
    route = jnp.where(lane == _L_W + 1, w2, route)
    route = jnp.where(lane == _L_RANK, rank1, route)
    route = jnp.where(lane == _L_RANK + 1, rank2, route)
    route_ref[pl.ds(first, t), :] = route
    meta_ref[:, pl.ds(first, t)] = route.T[:SUBLANES]
    return carry + jnp.sum(sel.astype(F32), axis=0, keepdims=True)


def _router_block(ya, yl, x2d, wo, g1, b1, rw, alpha):
    n = x2d.shape[0]
    tile = lambda w: pl.BlockSpec((TOKEN_TILE, w), lambda i: (i, 0))
    return pl.pallas_call(
        functools.partial(_router_kernel, alpha=alpha),
        grid=(n // TOKEN_TILE,),
        in_specs=[tile(ATTN_WIDTH), tile(LRU_WIDTH), tile(D_MODEL),
                  _resident((D_MODEL, D_MODEL)), _resident((1, D_MODEL)), _resident((1, D_MODEL)),
                  _resident((D_MODEL, LANES))],
        out_specs=[pl.BlockSpec((TOKEN_TILE * ROW_TILE, LANES), lambda i: (i, 0)), tile(LANES),
                   pl.BlockSpec((SUBLANES, TOKEN_TILE), lambda i: (0, i)),
                   pl.BlockSpec((1, LANES), lambda i: (0, 0))],
        out_shape=[jax.ShapeDtypeStruct((n * ROW_TILE, LANES), F32),
                   jax.ShapeDtypeStruct((n, LANES), F32),
                   jax.ShapeDtypeStruct((SUBLANES, n), F32),
                   jax.ShapeDtypeStruct((1, LANES), F32)],
        scratch_shapes=[pltpu.VMEM((1, LANES), F32)],
        compiler_params=_params(("arbitrary",)),
        name="router_block",
    )(ya, yl, x2d, wo, g1, b1, rw)


def _tile_copy(src_ref, src_row, dst_ref, dst_row, sem):
    src = src_ref.at[pl.ds(pl.multiple_of(src_row * ROW_TILE, ROW_TILE), ROW_TILE)]
    dst = dst_ref.at[pl.ds(pl.multiple_of(dst_row * ROW_TILE, ROW_TILE), ROW_TILE)]
    return pltpu.make_async_copy(src, dst, sem)


def _dispatch_kernel(pad_start_ref, pad_len_ref, pos0_ref, pos1_ref, x_ref, xs_ref, sem, pad_sem):
    def issue(t, carry):
        _tile_copy(x_ref, t, xs_ref, pos0_ref[t], sem).start(priority=0)
        _tile_copy(x_ref, t, xs_ref, pos1_ref[t], sem).start(priority=1)
        return carry

    lax.fori_loop(0, DISPATCH_TILE, issue, 0, unroll=DMA_UNROLL)
    for _ in range(2):
        pltpu.make_async_copy(x_ref, xs_ref.at[pl.ds(0, DISPATCH_TILE * ROW_TILE)], sem).wait()

    @pl.when(pl.program_id(0) == pl.num_programs(0) - 1)
    def _():
        for e in range(N_EXPERTS):
            def fill(r, carry):
                _tile_copy(x_ref, 0, xs_ref, pad_start_ref[e] + r, pad_sem).start()
                return carry

            def drain(r, carry):
                _tile_copy(x_ref, 0, xs_ref, 0, pad_sem).wait()
                return carry

            lax.fori_loop(0, pad_len_ref[e], fill, 0)
            lax.fori_loop(0, pad_len_ref[e], drain, 0)

        def idle_copy(r):
            first = pl.multiple_of((pad_start_ref[N_EXPERTS] + r * MOE_TILE) * ROW_TILE, MOE_TILE * ROW_TILE)
            return pltpu.make_async_copy(x_ref.at[pl.ds(0, MOE_TILE * ROW_TILE)],
                                         xs_ref.at[pl.ds(first, MOE_TILE * ROW_TILE)], pad_sem)

        def fill_idle(r, carry):
            idle_copy(r).start()
            return carry

        def drain_idle(r, carry):
            idle_copy(r).wait()
            return carry

        lax.fori_loop(0, pad_len_ref[N_EXPERTS], fill_idle, 0)
        lax.fori_loop(0, pad_len_ref[N_EXPERTS], drain_idle, 0)


def _dispatch(pad_start, pad_len, pos0, pos1, x1, sorted_rows):
    n = x1.shape[0] // ROW_TILE
    assert DISPATCH_TILE >= MOE_TILE
    smem_tile = pl.BlockSpec((DISPATCH_TILE,), lambda i, *_: (i,), memory_space=pltpu.SMEM)
    grid_spec = pltpu.PrefetchScalarGridSpec(
        num_scalar_prefetch=2,
        grid=(n // DISPATCH_TILE,),
        in_specs=[smem_tile, smem_tile,
                  pl.BlockSpec((DISPATCH_TILE * ROW_TILE, LANES), lambda i, *_: (i, 0))],
        out_specs=pl.BlockSpec(memory_space=pl.ANY),
        scratch_shapes=[pltpu.SemaphoreType.DMA(()), pltpu.SemaphoreType.DMA(())],
    )
    return pl.pallas_call(
        _dispatch_kernel,
        grid_spec=grid_spec,
        out_shape=jax.ShapeDtypeStruct((sorted_rows * ROW_TILE, LANES), F32),
        compiler_params=_params(("arbitrary",), disable_bounds_checks=True),
        name="moe_dispatch",
    )(pad_start, pad_len, pos0, pos1, x1)


def _experts_kernel(tile_expert_ref, n_used_ref, x_ref, wg_ref, wu_ref, wd_ref, y_ref, xb_ref, acc_ref):
    del tile_expert_ref
    i = pl.program_id(0)
    j = pl.program_id(1)
    last = pl.num_programs(1) - 1

    @pl.when(i < n_used_ref[0])
    def _():
        @pl.when(j == 0)
        def _():
            for c in range(ROW_TILE):
                xb_ref[:, c * LANES:(c + 1) * LANES] = (
                    x_ref[pl.ds(c, MOE_TILE, stride=ROW_TILE), :].astype(BF16))

        xb = xb_ref[...]
        g = jnp.dot(xb, wg_ref[...], preferred_element_type=F32)
        u = jnp.dot(xb, wu_ref[...], preferred_element_type=F32)
        hid = (jax.nn.silu(g) * u).astype(BF16)
        part = jnp.dot(hid, wd_ref[...], preferred_element_type=F32)

        @pl.when(j == 0)
        def _():
            acc_ref[...] = part

        @pl.when(jnp.logical_and(j > 0, j < last))
        def _():
            acc_ref[...] += part

        @pl.when(j == last)
        def _():
            _store_row_tiles(y_ref, acc_ref[...] + part)

    @pl.when(jnp.logical_and(i >= n_used_ref[0], j == 0))
    def _():
        y_ref[...] = jnp.zeros_like(y_ref)


def _experts(tile_expert, n_used, xs, wg, wu, wd):
    rows = xs.shape[0] // ROW_TILE
    d_ff = wg.shape[2]
    n_tiles = rows // MOE_TILE
    n_f = d_ff // MOE_F_TILE
    assert n_f >= 2

    def row_map(i, j, te, nu):
        return (jnp.minimum(i, nu[0] - 1), 0)

    def f_of(i, j, nu):
        return jnp.where(i < nu[0], j, n_f - 1)

    grid_spec = pltpu.PrefetchScalarGridSpec(
        num_scalar_prefetch=2,
        grid=(n_tiles, n_f),
        in_specs=[pl.BlockSpec((MOE_TILE * ROW_TILE, LANES), row_map),
                  pl.BlockSpec((None, D_MODEL, MOE_F_TILE), lambda i, j, te, nu: (te[i], 0, f_of(i, j, nu))),
                  pl.BlockSpec((None, D_MODEL, MOE_F_TILE), lambda i, j, te, nu: (te[i], 0, f_of(i, j, nu))),
                  pl.BlockSpec((None, MOE_F_TILE, D_MODEL), lambda i, j, te, nu: (te[i], f_of(i, j, nu), 0))],
        out_specs=pl.BlockSpec((MOE_TILE * ROW_TILE, LANES), lambda i, j, te, nu: (i, 0)),
        scratch_shapes=[pltpu.VMEM((MOE_TILE, D_MODEL), BF16), pltpu.VMEM((MOE_TILE, D_MODEL), F32)],
    )
    return pl.pallas_call(
        _experts_kernel,
        grid_spec=grid_spec,
        out_shape=jax.ShapeDtypeStruct((rows * ROW_TILE, LANES), F32),
        compiler_params=_params(("arbitrary", "arbitrary")),
        name="moe_experts",
    )(tile_expert, n_used, xs, wg, wu, wd)


def _combine_kernel(pos0_ref, pos1_ref, nxt0_ref, nxt1_ref, ys_ref, x1_ref, route_ref, g2_ref, b2_ref,
                    o_ref, buf_ref, sem, *, alpha):
    i = pl.program_id(0)
    slot = i % 2

    def issue(p0_ref, p1_ref, dst_slot):
        def body(t, carry):
            for k, p_ref in enumerate((p0_ref, p1_ref)):
                _tile_copy(ys_ref, p_ref[t], buf_ref.at[dst_slot, k], t, sem.at[dst_slot]).start(priority=k)
            return carry
        lax.fori_loop(0, COMBINE_TILE, body, 0, unroll=DMA_UNROLL)

    @pl.when(i == 0)
    def _():
        issue(pos0_ref, pos1_ref, 0)

    @pl.when(i + 1 < pl.num_programs(0))
    def _():
        issue(nxt0_ref, nxt1_ref, 1 - slot)

    for k in range(2):
        pltpu.make_async_copy(ys_ref.at[pl.ds(0, COMBINE_TILE * ROW_TILE)], buf_ref.at[slot, k],
                              sem.at[slot]).wait()
    route = route_ref[...]
    y0 = _load_row_tiles(buf_ref, COMBINE_TILE, (slot, 0))
    y1 = _load_row_tiles(buf_ref, COMBINE_TILE, (slot, 1))
    f = route[:, _L_W:_L_W + 1] * y0 + route[:, _L_W + 1:_L_W + 2] * y1
    x1 = _load_row_tiles(x1_ref, COMBINE_TILE)
    o_ref[...] = _layer_norm(alpha * x1 + f, g2_ref[...], b2_ref[...])


def _combine(pos0, pos1, ys, x1, route, g2, b2, alpha):
    n = x1.shape[0] // ROW_TILE
    steps = n // COMBINE_TILE
    tile = lambda w: pl.BlockSpec((COMBINE_TILE, w), lambda i: (i, 0))
    smem_cur = pl.BlockSpec((COMBINE_TILE,), lambda i: (i,), memory_space=pltpu.SMEM)
    smem_next = pl.BlockSpec((COMBINE_TILE,), lambda i: (jnp.minimum(i + 1, steps - 1),),
                             memory_space=pltpu.SMEM)
    return pl.pallas_call(
        functools.partial(_combine_kernel, alpha=alpha),
        grid=(steps,),
        in_specs=[smem_cur, smem_cur, smem_next, smem_next,
                  pl.BlockSpec(memory_space=pl.ANY),
                  pl.BlockSpec((COMBINE_TILE * ROW_TILE, LANES), lambda i: (i, 0)),
                  tile(LANES), _resident((1, D_MODEL)), _resident((1, D_MODEL))],
        out_specs=tile(D_MODEL),
        out_shape=jax.ShapeDtypeStruct((n, D_MODEL), F32),
        scratch_shapes=[pltpu.VMEM((2, 2, COMBINE_TILE * ROW_TILE, LANES), F32),
                        pltpu.SemaphoreType.DMA((2,))],
        compiler_params=_params(("arbitrary",), disable_bounds_checks=True),
        name="moe_combine",
    )(pos0, pos1, pos0, pos1, ys, x1, route, g2, b2)


def _moe_block(x1, route, meta, counts, wg, wu, wd, g2, b2, alpha):
    n = route.shape[0]
    n_tiles = (2 * n) // MOE_TILE + N_EXPERTS
    count = counts[0, :N_EXPERTS].astype(jnp.int32)
    tiles_per = (count + MOE_TILE - 1) // MOE_TILE
    tile_end = jnp.cumsum(tiles_per)
    group_start = (tile_end - tiles_per) * MOE_TILE
    n_used = tile_end[-1:]
    tile_id = jnp.minimum(jnp.arange(n_tiles, dtype=jnp.int32), n_used[0] - 1)
    tile_expert = jnp.sum((tile_id[:, None] >= tile_end[None, :]).astype(jnp.int32), axis=1)
    idx = meta[_L_IDX:_L_IDX + 2].astype(jnp.int32)
    rank = meta[_L_RANK:_L_RANK + 2].astype(jnp.int32)
    start = jnp.sum(jnp.where(idx[:, :, None] == jnp.arange(N_EXPERTS), group_start, 0), axis=-1)
    pos = start + rank

    pad_start = jnp.concatenate([group_start + count, n_used * MOE_TILE])
    pad_len = jnp.concatenate([tiles_per * MOE_TILE - count, n_tiles - n_used])
    xs = _dispatch(pad_start, pad_len, pos[0], pos[1], x1, n_tiles * MOE_TILE)
    ys = _experts(tile_expert, n_used, xs, wg, wu, wd)
    return _combine(pos[0], pos[1], ys, x1, route, g2, b2, alpha)


def _t5_bucket(rel):
    half = N_BUCKETS // 2
    max_exact = half // 2
    ret = jnp.where(rel > 0, half, 0)
    n = jnp.abs(rel)
    large = max_exact + (jnp.log(jnp.maximum(n, 1).astype(F32) / max_exact)
                         / math.log(MAX_DISTANCE / max_exact) * (half - max_exact)).astype(jnp.int32)
    large = jnp.minimum(large, half - 1)
    return ret + jnp.where(n < max_exact, n, large)


def _block_bias(rel_bias):
    qi = jnp.arange(Q_BLOCK)[:, None]
    kj = jnp.arange(K_BLOCK)[None, :]
    rel = (kj - WINDOW_CHUNKS * CHUNK) - qi
    hit = _t5_bucket(rel)[None, :, :, None] == jnp.arange(N_BUCKETS)
    bias = jnp.sum(jnp.where(hit, rel_bias.T[:, None, None, :], 0.0), axis=-1)
    first = (kj // CHUNK) - (qi // CHUNK)
    visible = jnp.logical_and(first >= 0, first <= WINDOW_CHUNKS)
    return jnp.where(visible[None], bias, NEG_INF).astype(F32) * LOG2E


def _in_weights(w_in):
    q = w_in[:, :ATTN_WIDTH] * (HEAD_DIM ** -0.5 * LOG2E)
    k = w_in[:, ATTN_WIDTH:ATTN_WIDTH + KV_COLS]
    v = w_in[:, ATTN_WIDTH + KV_COLS:ATTN_WIDTH + 2 * KV_COLS]
    rest = w_in[:, ATTN_WIDTH + 2 * KV_COLS:]
    swap = lambda w: jnp.concatenate([w[:, HEAD_DIM:], w[:, :HEAD_DIM]], axis=1)
    return jnp.concatenate([q, k, swap(k), v, swap(v), rest], axis=1).astype(BF16)


def _block_diag(w):
    nb, d, e = w.shape
    eye = jnp.eye(nb, dtype=w.dtype)
    return (eye[:, None, :, None] * w[:, :, None, :]).reshape(nb * d, nb * e)


def kernel(x, rel_bias, w_in, attn_sinks, conv_w, conv_b, gate_r_w, gate_r_b, gate_i_w, gate_i_b,
           lru_lambda, norm_attn_g, norm_lru_g, w_out, ln1_g, ln1_b, ffn_w_gate, ffn_w_up, ffn_w_down,
           router_w, exp_w_gate, exp_w_up, exp_w_down, ln2_g, ln2_b):
    batch, seq, d = x.shape
    depth = w_in.shape[0]
    alpha = float((2 * depth) ** 0.25)
    bias2 = _block_bias(rel_bias)
    row = lambda v: v.reshape(1, -1)

    rides = {}
    flat = lambda w: w.reshape(-1, w.shape[-1])
    for l in range(depth):
        if l % 2 == 0:
            rides[(l, "inproj")] = [ffn_w_gate[l // 2], ffn_w_up[l // 2], ffn_w_down[l // 2]]
        else:
            rides[(l - 1, "lru")] = [flat(exp_w_gate[l // 2])]
            rides[(l - 1, "attention")] = [flat(exp_w_up[l // 2])]
            rides[(l, "lru")] = [flat(exp_w_down[l // 2])]
    bf16_weights = {}

    h = x.reshape(batch * seq, d)
    for l in range(depth):
        qkv, lru, *done = _inproj(h, _in_weights(w_in[l]), rides.get((l, "inproj"), ()))
        bf16_weights[(l, "inproj")] = done
        ya, *done = _attention(qkv, attn_sinks[l] * LOG2E, bias2, row(norm_attn_g[l]), batch, seq,
                               rides.get((l, "attention"), ()))
        bf16_weights[(l, "attention")] = done
        w_gates = jnp.concatenate([_block_diag(gate_r_w[l]), _block_diag(gate_i_w[l])], axis=1).astype(BF16)
        b_gates = jnp.concatenate([gate_r_b[l], gate_i_b[l]]).reshape(1, -1)
        yl, *done = _rg_lru(lru, conv_w[l], row(conv_b[l]), w_gates, b_gates, row(lru_lambda[l]),
                            row(norm_lru_g[l]), batch, seq, rides.get((l, "lru"), ()))
        bf16_weights[(l, "lru")] = done
        wo = w_out[l].astype(BF16)
        if l % 2 == 0:
            wg, wu, wd = bf16_weights[(l, "inproj")]
            h = _dense_block(ya, yl, h, wo, row(ln1_g[l]), row(ln1_b[l]), wg, wu, wd,
                             row(ln2_g[l]), row(ln2_b[l]), alpha)
        else:
            (wg,), (wu,), (wd,) = (bf16_weights[(l - 1, "lru")], bf16_weights[(l - 1, "attention")],
                                   bf16_weights[(l, "lru")])
            rw = jnp.pad(router_w[l // 2], ((0, 0), (0, LANES - N_EXPERTS)))
            x1, route, meta, counts = _router_block(ya, yl, h, wo, row(ln1_g[l]), row(ln1_b[l]), rw, alpha)
            h = _moe_block(x1, route, meta, counts, wg.reshape(exp_w_gate.shape[1:]),
                           wu.reshape(exp_w_up.shape[1:]), wd.reshape(exp_w_down.shape[1:]),
                           row(ln2_g[l]), row(ln2_b[l]), alpha)
    return h.reshape(batch, seq, d)
```

```python
import functools
import math

import jax
import jax.numpy as jnp
from jax import lax
from jax.experimental import pallas as pl
from jax.experimental.pallas import tpu as pltpu

F32 = jnp.float32
BF16 = jnp.bfloat16

D_MODEL = 1024
CHUNK = 64
HEAD_DIM = 64
N_Q_HEADS = 8
N_KV_HEADS = 2
ATTN_WIDTH = N_Q_HEADS * HEAD_DIM
KV_COLS = N_KV_HEADS * HEAD_DIM
WINDOW_CHUNKS = 2
LRU_WIDTH = 512
LRU_BLOCKS = 8
CONV_WIDTH = 4
LRU_C = 8.0
N_BUCKETS = 32
MAX_DISTANCE = 128
N_EXPERTS = 8
NEG_INF = -1e30
LN_EPS = 1e-5
RMS_EPS = 1e-6
LOG2E = math.log2(math.e)

LANES = 128
SUBLANES = 8
VMEM_LIMIT = 56 * 1024 * 1024

QKV_COLS = ATTN_WIDTH + 4 * KV_COLS
Q_BLOCK = 2 * CHUNK
K_BLOCK = (WINDOW_CHUNKS + 2) * CHUNK
ATTN_TILE = 512
LRU_TILE = 512
TOKEN_TILE = 512
ROUTER_SLABS = 2
MXU_TILE = 256
MOE_TILE = 512
MOE_F_TILE = 7 * MXU_TILE
DISPATCH_TILE = 1024
COMBINE_TILE = 256
DMA_UNROLL = 8


def _params(sem, **kw):
    return pltpu.CompilerParams(dimension_semantics=sem, vmem_limit_bytes=VMEM_LIMIT, **kw)


def _resident(shape):
    nd = len(shape)
    return pl.BlockSpec(shape, lambda *_: (0,) * nd, pipeline_mode=pl.Buffered(1))


def _layer_norm(x, g, b):
    mu = jnp.mean(x, axis=-1, keepdims=True)
    xc = x - mu
    var = jnp.mean(xc * xc, axis=-1, keepdims=True)
    return xc * lax.rsqrt(var + LN_EPS) * g + b


def _rms_norm(x, g):
    return x * lax.rsqrt(jnp.mean(x * x, axis=-1, keepdims=True) + RMS_EPS) * g


ROW_TILE = D_MODEL // LANES


def _load_row_tiles(ref, rows, lead=()):
    return jnp.concatenate(
        [ref[lead + (pl.ds(c, rows, stride=ROW_TILE), slice(None))] for c in range(ROW_TILE)], axis=1)


def _store_row_tiles(ref, value, first_row=0):
    for c in range(ROW_TILE):
        ref[pl.ds(first_row * ROW_TILE + c, value.shape[0], stride=ROW_TILE), :] = (
            value[:, c * LANES:(c + 1) * LANES])


BF16_ROWS = 2 * SUBLANES


def _with_casts(kernel_fn, n_in, n_out, n_casts):
    def wrapped(*refs):
        ins, refs = refs[:n_in], refs[n_in:]
        srcs, refs = refs[:n_casts], refs[n_casts:]
        outs, refs = refs[:n_out], refs[n_out:]
        dsts, scratch = refs[:n_casts], refs[n_casts:]
        for src, dst in zip(srcs, dsts):
            dst[...] = src[...].astype(BF16)
        kernel_fn(*ins, *outs, *scratch)
    return wrapped


def _cast_specs(weights, linear_step, steps):
    specs, shapes = [], []
    for w in weights:
        rows, cols = w.shape
        per = next(p for p in range(BF16_ROWS, rows + 1, BF16_ROWS) if rows % p == 0 and p * steps >= rows)
        last = rows // per - 1
        specs.append(pl.BlockSpec((per, cols),
                                  lambda *g, last=last: (jnp.minimum(linear_step(*g), last), 0)))
        shapes.append(jax.ShapeDtypeStruct(w.shape, BF16))
    return specs, shapes


def _attn_kernel(sink_ref, q_ref, k_ref, ks_ref, v_ref, vs_ref, kp_ref, ksp_ref, vp_ref, vsp_ref,
                 bias_ref, g_ref, o_ref, kk_ref, vv_ref, acc_ref):
    step = pl.program_id(1)
    hist = Q_BLOCK
    lane = lax.broadcasted_iota(jnp.int32, (1, LANES), 1)
    lo = lane < HEAD_DIM
    zero = jnp.zeros((), BF16)

    def fill(dst, plain_prev, plain, swap_prev, swap):
        for rows, plain_ref, swap_ref in ((slice(0, hist), plain_prev, swap_prev),
                                          (slice(hist, hist + ATTN_TILE), plain, swap)):
            a = plain_ref[...]
            s = swap_ref[...]
            dst[0, rows, :] = jnp.where(lo, a, zero)
            dst[1, rows, :] = jnp.where(lo, zero, s)
            dst[2, rows, :] = jnp.where(lo, s, zero)
            dst[3, rows, :] = jnp.where(lo, zero, a)

    fill(kk_ref, kp_ref, k_ref, ksp_ref, ks_ref)
    fill(vv_ref, vp_ref, v_ref, vsp_ref, vs_ref)

    col = lax.broadcasted_iota(jnp.int32, (1, K_BLOCK), 1)
    dn = (((1,), (1,)), ((), ()))

    for s in range(ATTN_TILE // Q_BLOCK):
        q0 = s * Q_BLOCK
        start_mask = None
        if s == 0:
            before_start = jnp.logical_and(step == 0, col < hist)
            start_mask = jnp.where(before_start, NEG_INF, 0.0).astype(F32)
        for h in range(N_KV_HEADS):
            q2 = jnp.concatenate(
                [q_ref[pl.ds(q0, Q_BLOCK), pl.ds((2 * h + p) * LANES, LANES)] for p in range(2)], axis=0)
            probs = []
            inv_den = []
            for half in range(2):
                kk = kk_ref[2 * h + half, pl.ds(q0, K_BLOCK), :]
                sc = lax.dot_general(q2, kk, dn, preferred_element_type=F32)
                for p in range(2):
                    head = 4 * h + 2 * p + half
                    logit = sc[p * Q_BLOCK:(p + 1) * Q_BLOCK] + bias_ref[head]
                    if start_mask is not None:
                        logit = logit + start_mask
                    sink = sink_ref[head]
                    m = jnp.maximum(jnp.max(logit, axis=-1, keepdims=True), sink)
                    e = jnp.exp2(logit - m)
                    den = jnp.sum(e, axis=-1, keepdims=True) + jnp.exp2(sink - m)
                    probs.append((p, half, e.astype(BF16)))
                    inv_den.append((p, half, 1.0 / den))
            pmap = {(p, half): e for p, half, e in probs}
            dmap = {(p, half): d for p, half, d in inv_den}
            lhs = jnp.concatenate(
                [jnp.concatenate([pmap[(p, 0)], pmap[(p, 1)]], axis=1) for p in range(2)], axis=0)
            rhs = jnp.concatenate([vv_ref[2 * h, pl.ds(q0, K_BLOCK), :],
                                   vv_ref[2 * h + 1, pl.ds(q0, K_BLOCK), :]], axis=0)
            out = jnp.dot(lhs, rhs, preferred_element_type=F32)
            for p in range(2):
                scale = jnp.where(lo, dmap[(p, 0)], dmap[(p, 1)])
                acc_ref[pl.ds(q0, Q_BLOCK), pl.ds((2 * h + p) * LANES, LANES)] = (
                    out[p * Q_BLOCK:(p + 1) * Q_BLOCK] * scale)

    o_ref[...] = _rms_norm(acc_ref[...], g_ref[...]).astype(BF16)


def _attention(qkv, sinks, bias2, g_attn, batch, seq, casts=()):
    n = qkv.shape[0]
    steps = seq // ATTN_TILE
    ratio = ATTN_TILE // Q_BLOCK
    qcb = ATTN_WIDTH // LANES
    cast_specs, cast_shapes = _cast_specs(casts, lambda b, i: b * steps + i, batch * steps)

    def cur(cb):
        return pl.BlockSpec((ATTN_TILE, LANES), lambda b, i: (b * steps + i, cb))

    def prev(cb):
        return pl.BlockSpec((Q_BLOCK, LANES),
                            lambda b, i: (b * steps * ratio + jnp.maximum(i * ratio - 1, 0), cb))

    return pl.pallas_call(
        _with_casts(_attn_kernel, 12, 1, len(casts)),
        grid=(batch, steps),
        in_specs=[pl.BlockSpec(memory_space=pltpu.SMEM),
                  pl.BlockSpec((ATTN_TILE, ATTN_WIDTH), lambda b, i: (b * steps + i, 0)),
                  cur(qcb), cur(qcb + 1), cur(qcb + 2), cur(qcb + 3),
                  prev(qcb), prev(qcb + 1), prev(qcb + 2), prev(qcb + 3),
                  _resident((N_Q_HEADS, Q_BLOCK, K_BLOCK)),
                  _resident((1, ATTN_WIDTH))] + cast_specs,
        out_specs=[pl.BlockSpec((ATTN_TILE, ATTN_WIDTH), lambda b, i: (b * steps + i, 0))] + cast_specs,
        out_shape=[jax.ShapeDtypeStruct((n, ATTN_WIDTH), BF16)] + cast_shapes,
        scratch_shapes=[pltpu.VMEM((4, Q_BLOCK + ATTN_TILE, LANES), BF16),
                        pltpu.VMEM((4, Q_BLOCK + ATTN_TILE, LANES), BF16),
                        pltpu.VMEM((ATTN_TILE, ATTN_WIDTH), F32)],
        compiler_params=_params(("arbitrary", "arbitrary")),
        name="attention",
    )(sinks, qkv, qkv, qkv, qkv, qkv, qkv, qkv, qkv, qkv, bias2, g_attn, *casts)


def _lru_slab(c, x, gate, prev, h_in, cw_ref, cb_ref, wg_ref, bg_ref, lam_ref, a_ref, u_ref, ext_ref):
    t = x.shape[0]
    lanes = slice(c * LANES, (c + 1) * LANES)
    i_lanes = slice(LRU_WIDTH + c * LANES, LRU_WIDTH + (c + 1) * LANES)

    ext_ref[c, :SUBLANES, :] = prev
    ext_ref[c, SUBLANES:, :] = x
    xc = x * cw_ref[CONV_WIDTH - 1:CONV_WIDTH, lanes] + cb_ref[:, lanes]
    for k in range(1, CONV_WIDTH):
        xc = xc + (ext_ref[c, SUBLANES - k:SUBLANES - k + t, :]
                   * cw_ref[CONV_WIDTH - 1 - k:CONV_WIDTH - k, lanes])

    xcb = xc.astype(BF16)
    r = jax.nn.sigmoid(jnp.dot(xcb, wg_ref[lanes, lanes], preferred_element_type=F32) + bg_ref[:, lanes])
    i = jax.nn.sigmoid(jnp.dot(xcb, wg_ref[lanes, i_lanes], preferred_element_type=F32)
                       + bg_ref[:, i_lanes])
    neg_lam = -lam_ref[:, lanes]
    softplus = jnp.maximum(neg_lam, 0.0) + jnp.log1p(jnp.exp(-jnp.abs(neg_lam)))
    log_a = (-LRU_C) * r * softplus
    a = jnp.exp(log_a)
    h = jnp.sqrt(jnp.tanh(-log_a) * (1.0 + a * a)) * (i * xc)

    def doubling(av, hv, index, length, axis):
        d = 1
        while d < length:
            keep = index >= d
            a_sh = jnp.where(keep, pltpu.roll(av, d, axis), 1.0)
            h_sh = jnp.where(keep, pltpu.roll(hv, d, axis), 0.0)
            hv = av * h_sh + hv
            av = av * a_sh
            d *= 2
        return av, hv

    groups = t // SUBLANES
    grouped = (groups, SUBLANES, LANES)
    sub = lax.broadcasted_iota(jnp.int32, (1, SUBLANES, 1), 1)
    a, h = doubling(a.reshape(grouped), h.reshape(grouped), sub, SUBLANES, 1)
    a = a.reshape(t, LANES)
    h = h.reshape(t, LANES)
    grow = lax.broadcasted_iota(jnp.int32, (groups, 1), 0)

    def group_last(ref, value):
        ref[c] = value
        return ref[c, pl.ds(SUBLANES - 1, groups, stride=SUBLANES), :]

    ag, hg = doubling(group_last(a_ref, a), group_last(u_ref, h), grow, groups, 0)
    state = hg + ag * h_in
    enter = jnp.where(grow >= 1, pltpu.roll(state, 1, 0), h_in)
    enter = jnp.broadcast_to(enter[:, None, :], grouped).reshape(t, LANES)
    h = h + a * enter
    return jax.nn.gelu(gate) * h, state[groups - 1:groups, :]


def _inproj_lru_kernel(x_ref, w_ref, cw_ref, cb_ref, wg_ref, bg_ref, lam_ref, g_ref,
                       qkv_ref, o_ref, z0_ref, z1_ref, tail_ref, h_ref, a_ref, u_ref, ext_ref, y_ref, *,
                       tiles_per_seq):
    s = pl.program_id(0)

    @pl.when(s == 0)
    def _():
        z1_ref[...] = jnp.zeros_like(z1_ref)
        tail_ref[...] = jnp.zeros_like(tail_ref)
        h_ref[...] = jnp.zeros_like(h_ref)

    def step(z_new_ref, z_old_ref):
        xb = x_ref[...].astype(BF16)
        seq_start = lax.rem(s - 1, tiles_per_seq) == 0

        def inproj_chunk(j):
            cols = slice(j * MXU_TILE, (j + 1) * MXU_TILE)
            z = jnp.dot(xb, w_ref[:, cols], preferred_element_type=F32)
            if cols.stop <= QKV_COLS:
                qkv_ref[:, cols] = z.astype(BF16)
            else:
                z_new_ref[:, cols.start - QKV_COLS:cols.stop - QKV_COLS] = z

        slabs = LRU_WIDTH // LANES
        chunks_per_slab = (QKV_COLS + 2 * LRU_WIDTH) // MXU_TILE // slabs
        sq = None
        for c in range(slabs):
            lanes = slice(c * LANES, (c + 1) * LANES)
            inproj_chunk(chunks_per_slab * c)
            x = z_old_ref[:, lanes]
            prev = jnp.where(seq_start, 0.0, tail_ref[:, lanes])
            h_in = jnp.where(seq_start, 0.0, h_ref[:, lanes])
            y, h_out = _lru_slab(c, x, z_old_ref[:, LRU_WIDTH + c * LANES:LRU_WIDTH + (c + 1) * LANES],
                                 prev, h_in, cw_ref, cb_ref, wg_ref, bg_ref, lam_ref, a_ref, u_ref, ext_ref)
            y_ref[:, lanes] = y
            part = jnp.sum(y * y, axis=-1, keepdims=True)
            sq = part if sq is None else sq + part
            h_ref[:, lanes] = h_out
            tail_ref[:, lanes] = x[LRU_TILE - SUBLANES:, :]
            for j in range(chunks_per_slab * c + 1, chunks_per_slab * (c + 1)):
                inproj_chunk(j)
        scale = lax.rsqrt(sq * (1.0 / LRU_WIDTH) + RMS_EPS)
        o_ref[...] = (y_ref[...] * scale * g_ref[...]).astype(BF16)

    @pl.when(s % 2 == 0)
    def _():
        step(z0_ref, z1_ref)

    @pl.when(s % 2 == 1)
    def _():
        step(z1_ref, z0_ref)


def _inproj_lru(x2d, w, conv_w, conv_b, w_gates, b_gates, lam, g_lru, seq, casts=()):
    n = x2d.shape[0]
    tiles = n // LRU_TILE
    last = tiles - 1
    cast_specs, cast_shapes = _cast_specs(casts, lambda s: s, tiles + 1)
    return pl.pallas_call(
        _with_casts(functools.partial(_inproj_lru_kernel, tiles_per_seq=seq // LRU_TILE), 8, 2, len(casts)),
        grid=(tiles + 1,),
        in_specs=[pl.BlockSpec((LRU_TILE, D_MODEL), lambda s: (jnp.minimum(s, last), 0)),
                  _resident((D_MODEL, QKV_COLS + 2 * LRU_WIDTH)),
                  _resident((CONV_WIDTH, LRU_WIDTH)),
                  _resident((1, LRU_WIDTH)),
                  _resident((LRU_WIDTH, 2 * LRU_WIDTH)),
                  _resident((1, 2 * LRU_WIDTH)),
                  _resident((1, LRU_WIDTH)),
                  _resident((1, LRU_WIDTH))] + cast_specs,
        out_specs=[pl.BlockSpec((LRU_TILE, QKV_COLS), lambda s: (jnp.minimum(s, last), 0)),
                   pl.BlockSpec((LRU_TILE, LRU_WIDTH), lambda s: (jnp.maximum(s - 1, 0), 0))] + cast_specs,
        out_shape=[jax.ShapeDtypeStruct((n, QKV_COLS), BF16),
                   jax.ShapeDtypeStruct((n, LRU_WIDTH), BF16)] + cast_shapes,
        scratch_shapes=[pltpu.VMEM((LRU_TILE, 2 * LRU_WIDTH), F32),
                        pltpu.VMEM((LRU_TILE, 2 * LRU_WIDTH), F32),
                        pltpu.VMEM((SUBLANES, LRU_WIDTH), F32),
                        pltpu.VMEM((1, LRU_WIDTH), F32),
                        pltpu.VMEM((LRU_WIDTH // LANES, LRU_TILE, LANES), F32),
                        pltpu.VMEM((LRU_WIDTH // LANES, LRU_TILE, LANES), F32),
                        pltpu.VMEM((LRU_WIDTH // LANES, SUBLANES + LRU_TILE, LANES), F32),
                        pltpu.VMEM((LRU_TILE, LRU_WIDTH), F32)],
        compiler_params=_params(("arbitrary",)),
        name="inproj_lru",
    )(x2d, w, conv_w, conv_b, w_gates, b_gates, lam, g_lru, *casts)


def _mix_ln1(ya_ref, yl_ref, x_ref, wo_ref, g1_ref, b1_ref, alpha, rows=slice(None)):
    y = jnp.dot(ya_ref[rows, :], wo_ref[:ATTN_WIDTH, :], preferred_element_type=F32)
    y = y + jnp.dot(yl_ref[rows, :], wo_ref[ATTN_WIDTH:, :], preferred_element_type=F32)
    return _layer_norm(alpha * x_ref[rows, :] + y, g1_ref[...], b1_ref[...])


def _dense_block_kernel(ya_ref, yl_ref, x_ref, wo_ref, g1_ref, b1_ref, wg_ref, wu_ref, wd_ref,
                        g2_ref, b2_ref, o_ref, *, alpha, f_splits):
    x1 = _mix_ln1(ya_ref, yl_ref, x_ref, wo_ref, g1_ref, b1_ref, alpha)
    xb = x1.astype(BF16)
    f = None
    for lo, hi in zip(f_splits[:-1], f_splits[1:]):
        cols = slice(lo, hi)
        g = jnp.dot(xb, wg_ref[:, cols], preferred_element_type=F32)
        u = jnp.dot(xb, wu_ref[:, cols], preferred_element_type=F32)
        hid = (jax.nn.silu(g) * u).astype(BF16)
        part = jnp.dot(hid, wd_ref[cols, :], preferred_element_type=F32)
        f = part if f is None else f + part
    o_ref[...] = _layer_norm(alpha * x1 + f, g2_ref[...], b2_ref[...])


def _dense_block(ya, yl, x2d, wo, g1, b1, wg, wu, wd, g2, b2, alpha):
    n = x2d.shape[0]
    d_ff = wg.shape[1]
    tile = lambda w: pl.BlockSpec((TOKEN_TILE, w), lambda i: (i, 0))
    mxu_tiles = pl.cdiv(d_ff, MXU_TILE)
    f_splits = (0, (mxu_tiles + 1) // 2 * MXU_TILE, d_ff)
    return pl.pallas_call(
        functools.partial(_dense_block_kernel, alpha=alpha, f_splits=f_splits),
        grid=(n // TOKEN_TILE,),
        in_specs=[tile(ATTN_WIDTH), tile(LRU_WIDTH), tile(D_MODEL),
                  _resident((D_MODEL, D_MODEL)), _resident((1, D_MODEL)), _resident((1, D_MODEL)),
                  _resident((D_MODEL, d_ff)), _resident((D_MODEL, d_ff)), _resident((d_ff, D_MODEL)),
                  _resident((1, D_MODEL)), _resident((1, D_MODEL))],
        out_specs=tile(D_MODEL),
        out_shape=jax.ShapeDtypeStruct((n, D_MODEL), F32),
        compiler_params=_params(("parallel",)),
        name="dense_block",
    )(ya, yl, x2d, wo, g1, b1, wg, wu, wd, g2, b2)


_L_IDX, _L_W, _L_RANK = 0, 2, 4


def _router_kernel(ya_ref, yl_ref, x_ref, wo_ref, g1_ref, b1_ref, rw_ref,
                   x1_ref, route_ref, meta_ref, count_ref, carry_ref, *, alpha):
    @pl.when(pl.program_id(0) == 0)
    def _():
        carry_ref[...] = jnp.zeros_like(carry_ref)

    t = x_ref.shape[0] // ROUTER_SLABS
    carry = carry_ref[...]
    for s in range(ROUTER_SLABS):
        carry = _route_slab(s * t, t, carry, ya_ref, yl_ref, x_ref, wo_ref, g1_ref, b1_ref, rw_ref,
                            x1_ref, route_ref, meta_ref, alpha)
    carry_ref[...] = carry
    count_ref[...] = carry


def _route_slab(first, t, carry, ya_ref, yl_ref, x_ref, wo_ref, g1_ref, b1_ref, rw_ref,
                x1_ref, route_ref, meta_ref, alpha):
    x1 = _mix_ln1(ya_ref, yl_ref, x_ref, wo_ref, g1_ref, b1_ref, alpha, pl.ds(first, t))
    _store_row_tiles(x1_ref, x1, first)
    rw = rw_ref[...]
    x_hi = x1.astype(BF16)
    x_lo = (x1 - x_hi.astype(F32)).astype(BF16)
    w_hi = rw.astype(BF16)
    w_lo = (rw - w_hi.astype(F32)).astype(BF16)
    logits = (jnp.dot(x_hi, w_hi, preferred_element_type=F32)
              + jnp.dot(x_lo, w_hi, preferred_element_type=F32)
              + jnp.dot(x_hi, w_lo, preferred_element_type=F32))
    lane = lax.broadcasted_iota(jnp.int32, (t, LANES), 1).astype(F32)
    lowest = jnp.finfo(F32).min
    l1 = jnp.where(lane < N_EXPERTS, logits, lowest)
    m1 = jnp.max(l1, axis=-1, keepdims=True)
    i1 = jnp.min(jnp.where(l1 == m1, lane, float(LANES)), axis=-1, keepdims=True)
    l2 = jnp.where(lane == i1, lowest, l1)
    m2 = jnp.max(l2, axis=-1, keepdims=True)
    i2 = jnp.min(jnp.where(l2 == m2, lane, float(LANES)), axis=-1, keepdims=True)
    e = jnp.exp(m2 - m1)
    w1 = 1.0 / (1.0 + e)
    w2 = e / (1.0 + e)

    chosen = jnp.logical_or(lane == i1, lane == i2)
    sel = jnp.where(chosen, 1.0, 0.0).astype(BF16)
    r_i = lax.broadcasted_iota(jnp.int32, (t, t), 0)
    c_i = lax.broadcasted_iota(jnp.int32, (t, t), 1)
    earlier = jnp.where(c_i < r_i, 1.0, 0.0).astype(BF16)
    before = jnp.dot(earlier, sel, preferred_element_type=F32) + carry
    rank1 = jnp.sum(jnp.where(lane == i1, before, 0.0), axis=-1, keepdims=True)
    rank2 = jnp.sum(jnp.where(lane == i2, before, 0.0), axis=-1, keepdims=True)

    route = jnp.where(lane == _L_IDX, i1, 0.0)
    route = jnp.where(lane == _L_IDX + 1, i2, route)
    route = jnp.where(lane == _L_W, w1, route)
    route = jnp.where(lane == _L_W + 1, w2, route)
    route = jnp.where(lane == _L_RANK, rank1, route)
    route = jnp.where(lane == _L_RANK + 1, rank2, route)
    route_ref[pl.ds(first, t), :] = route
    meta_ref[:, pl.ds(first, t)] = route.T[:SUBLANES]
    return carry + jnp.sum(sel.astype(F32), axis=0, keepdims=True)


def _router_block(ya, yl, x2d, wo, g1, b1, rw, alpha):
    n = x2d.shape[0]
    tile = lambda w: pl.BlockSpec((TOKEN_TILE, w), lambda i: (i, 0))
    return pl.pallas_call(
        functools.partial(_router_kernel, alpha=alpha),
        grid=(n // TOKEN_TILE,),
        in_specs=[tile(ATTN_WIDTH), tile(LRU_WIDTH), tile(D_MODEL),
                  _resident((D_MODEL, D_MODEL)), _resident((1, D_MODEL)), _resident((1, D_MODEL)),
                  _resident((D_MODEL, LANES))],
        out_specs=[pl.BlockSpec((TOKEN_TILE * ROW_TILE, LANES), lambda i: (i, 0)), tile(LANES),
                   pl.BlockSpec((SUBLANES, TOKEN_TILE), lambda i: (0, i)),
                   pl.BlockSpec((1, LANES), lambda i: (0, 0))],
        out_shape=[jax.ShapeDtypeStruct((n * ROW_TILE, LANES), F32),
                   jax.ShapeDtypeStruct((n, LANES), F32),
                   jax.ShapeDtypeStruct((SUBLANES, n), F32),
                   jax.ShapeDtypeStruct((1, LANES), F32)],
        scratch_shapes=[pltpu.VMEM((1, LANES), F32)],
        compiler_params=_params(("arbitrary",)),
        name="router_block",
    )(ya, yl, x2d, wo, g1, b1, rw)


def _tile_copy(src_ref, src_row, dst_ref, dst_row, sem):
    src = src_ref.at[pl.ds(pl.multiple_of(src_row * ROW_TILE, ROW_TILE), ROW_TILE)]
    dst = dst_ref.at[pl.ds(pl.multiple_of(dst_row * ROW_TILE, ROW_TILE), ROW_TILE)]
    return pltpu.make_async_copy(src, dst, sem)


def _dispatch_kernel(pad_start_ref, pad_len_ref, pos0_ref, pos1_ref, x_ref, xs_ref, sem, pad_sem):
    def issue(t, carry):
        _tile_copy(x_ref, t, xs_ref, pos0_ref[t], sem).start(priority=0)
        _tile_copy(x_ref, t, xs_ref, pos1_ref[t], sem).start(priority=1)
        return carry

    lax.fori_loop(0, DISPATCH_TILE, issue, 0, unroll=DMA_UNROLL)
    for _ in range(2):
        pltpu.make_async_copy(x_ref, xs_ref.at[pl.ds(0, DISPATCH_TILE * ROW_TILE)], sem).wait()

    @pl.when(pl.program_id(0) == pl.num_programs(0) - 1)
    def _():
        for e in range(N_EXPERTS):
            def fill(r, carry):
                _tile_copy(x_ref, 0, xs_ref, pad_start_ref[e] + r, pad_sem).start()
                return carry

            def drain(r, carry):
                _tile_copy(x_ref, 0, xs_ref, 0, pad_sem).wait()
                return carry

            lax.fori_loop(0, pad_len_ref[e], fill, 0)
            lax.fori_loop(0, pad_len_ref[e], drain, 0)

        def idle_copy(r):
            first = pl.multiple_of((pad_start_ref[N_EXPERTS] + r * MOE_TILE) * ROW_TILE, MOE_TILE * ROW_TILE)
            return pltpu.make_async_copy(x_ref.at[pl.ds(0, MOE_TILE * ROW_TILE)],
                                         xs_ref.at[pl.ds(first, MOE_TILE * ROW_TILE)], pad_sem)

        def fill_idle(r, carry):
            idle_copy(r).start()
            return carry

        def drain_idle(r, carry):
            idle_copy(r).wait()
            return carry

        lax.fori_loop(0, pad_len_ref[N_EXPERTS], fill_idle, 0)
        lax.fori_loop(0, pad_len_ref[N_EXPERTS], drain_idle, 0)


def _dispatch(pad_start, pad_len, pos0, pos1, x1, sorted_rows):
    n = x1.shape[0] // ROW_TILE
    assert DISPATCH_TILE >= MOE_TILE
    smem_tile = pl.BlockSpec((DISPATCH_TILE,), lambda i, *_: (i,), memory_space=pltpu.SMEM)
    grid_spec = pltpu.PrefetchScalarGridSpec(
        num_scalar_prefetch=2,
        grid=(n // DISPATCH_TILE,),
        in_specs=[smem_tile, smem_tile,
                  pl.BlockSpec((DISPATCH_TILE * ROW_TILE, LANES), lambda i, *_: (i, 0))],
        out_specs=pl.BlockSpec(memory_space=pl.ANY),
        scratch_shapes=[pltpu.SemaphoreType.DMA(()), pltpu.SemaphoreType.DMA(())],
    )
    return pl.pallas_call(
        _dispatch_kernel,
        grid_spec=grid_spec,
        out_shape=jax.ShapeDtypeStruct((sorted_rows * ROW_TILE, LANES), F32),
        compiler_params=_params(("arbitrary",), disable_bounds_checks=True),
        name="moe_dispatch",
    )(pad_start, pad_len, pos0, pos1, x1)


def _experts_kernel(tile_expert_ref, n_used_ref, x_ref, wg_ref, wu_ref, wd_ref, y_ref, xb_ref, acc_ref):
    del tile_expert_ref
    i = pl.program_id(0)
    j = pl.program_id(1)
    last = pl.num_programs(1) - 1

    @pl.when(i < n_used_ref[0])
    def _():
        @pl.when(j == 0)
        def _():
            for c in range(ROW_TILE):
                xb_ref[:, c * LANES:(c + 1) * LANES] = (
                    x_ref[pl.ds(c, MOE_TILE, stride=ROW_TILE), :].astype(BF16))

        xb = xb_ref[...]
        g = jnp.dot(xb, wg_ref[...], preferred_element_type=F32)
        u = jnp.dot(xb, wu_ref[...], preferred_element_type=F32)
        hid = (jax.nn.silu(g) * u).astype(BF16)
        part = jnp.dot(hid, wd_ref[...], preferred_element_type=F32)

        @pl.when(j == 0)
        def _():
            acc_ref[...] = part

        @pl.when(jnp.logical_and(j > 0, j < last))
        def _():
            acc_ref[...] += part

        @pl.when(j == last)
        def _():
            _store_row_tiles(y_ref, acc_ref[...] + part)

    @pl.when(jnp.logical_and(i >= n_used_ref[0], j == 0))
    def _():
        y_ref[...] = jnp.zeros_like(y_ref)


def _experts(tile_expert, n_used, xs, wg, wu, wd):
    rows = xs.shape[0] // ROW_TILE
    d_ff = wg.shape[2]
    n_tiles = rows // MOE_TILE
    n_f = d_ff // MOE_F_TILE
    assert n_f >= 2

    def row_map(i, j, te, nu):
        return (jnp.minimum(i, nu[0] - 1), 0)

    def f_of(i, j, nu):
        return jnp.where(i < nu[0], j, n_f - 1)

    grid_spec = pltpu.PrefetchScalarGridSpec(
        num_scalar_prefetch=2,
        grid=(n_tiles, n_f),
        in_specs=[pl.BlockSpec((MOE_TILE * ROW_TILE, LANES), row_map),
                  pl.BlockSpec((None, D_MODEL, MOE_F_TILE), lambda i, j, te, nu: (te[i], 0, f_of(i, j, nu))),
                  pl.BlockSpec((None, D_MODEL, MOE_F_TILE), lambda i, j, te, nu: (te[i], 0, f_of(i, j, nu))),
                  pl.BlockSpec((None, MOE_F_TILE, D_MODEL), lambda i, j, te, nu: (te[i], f_of(i, j, nu), 0))],
        out_specs=pl.BlockSpec((MOE_TILE * ROW_TILE, LANES), lambda i, j, te, nu: (i, 0)),
        scratch_shapes=[pltpu.VMEM((MOE_TILE, D_MODEL), BF16), pltpu.VMEM((MOE_TILE, D_MODEL), F32)],
    )
    return pl.pallas_call(
        _experts_kernel,
        grid_spec=grid_spec,
        out_shape=jax.ShapeDtypeStruct((rows * ROW_TILE, LANES), F32),
        compiler_params=_params(("arbitrary", "arbitrary")),
        name="moe_experts",
    )(tile_expert, n_used, xs, wg, wu, wd)


def _combine_kernel(pos0_ref, pos1_ref, nxt0_ref, nxt1_ref, ys_ref, x1_ref, route_ref, g2_ref, b2_ref,
                    o_ref, buf_ref, sem, *, alpha):
    i = pl.program_id(0)
    slot = i % 2

    def issue(p0_ref, p1_ref, dst_slot):
        def body(t, carry):
            for k, p_ref in enumerate((p0_ref, p1_ref)):
                _tile_copy(ys_ref, p_ref[t], buf_ref.at[dst_slot, k], t, sem.at[dst_slot]).start(priority=k)
            return carry
        lax.fori_loop(0, COMBINE_TILE, body, 0, unroll=DMA_UNROLL)

    @pl.when(i == 0)
    def _():
        issue(pos0_ref, pos1_ref, 0)

    @pl.when(i + 1 < pl.num_programs(0))
    def _():
        issue(nxt0_ref, nxt1_ref, 1 - slot)

    for k in range(2):
        pltpu.make_async_copy(ys_ref.at[pl.ds(0, COMBINE_TILE * ROW_TILE)], buf_ref.at[slot, k],
                              sem.at[slot]).wait()
    route = route_ref[...]
    y0 = _load_row_tiles(buf_ref, COMBINE_TILE, (slot, 0))
    y1 = _load_row_tiles(buf_ref, COMBINE_TILE, (slot, 1))
    f = route[:, _L_W:_L_W + 1] * y0 + route[:, _L_W + 1:_L_W + 2] * y1
    x1 = _load_row_tiles(x1_ref, COMBINE_TILE)
    o_ref[...] = _layer_norm(alpha * x1 + f, g2_ref[...], b2_ref[...])


def _combine(pos0, pos1, ys, x1, route, g2, b2, alpha):
    n = x1.shape[0] // ROW_TILE
    steps = n // COMBINE_TILE
    tile = lambda w: pl.BlockSpec((COMBINE_TILE, w), lambda i: (i, 0))
    smem_cur = pl.BlockSpec((COMBINE_TILE,), lambda i: (i,), memory_space=pltpu.SMEM)
    smem_next = pl.BlockSpec((COMBINE_TILE,), lambda i: (jnp.minimum(i + 1, steps - 1),),
                             memory_space=pltpu.SMEM)
    return pl.pallas_call(
        functools.partial(_combine_kernel, alpha=alpha),
        grid=(steps,),
        in_specs=[smem_cur, smem_cur, smem_next, smem_next,
                  pl.BlockSpec(memory_space=pl.ANY),
                  pl.BlockSpec((COMBINE_TILE * ROW_TILE, LANES), lambda i: (i, 0)),
                  tile(LANES), _resident((1, D_MODEL)), _resident((1, D_MODEL))],
        out_specs=tile(D_MODEL),
        out_shape=jax.ShapeDtypeStruct((n, D_MODEL), F32),
        scratch_shapes=[pltpu.VMEM((2, 2, COMBINE_TILE * ROW_TILE, LANES), F32),
                        pltpu.SemaphoreType.DMA((2,))],
        compiler_params=_params(("arbitrary",), disable_bounds_checks=True),
        name="moe_combine",
    )(pos0, pos1, pos0, pos1, ys, x1, route, g2, b2)


def _moe_block(x1, route, meta, counts, wg, wu, wd, g2, b2, alpha):
    n = route.shape[0]
    n_tiles = (2 * n) // MOE_TILE + N_EXPERTS
    count = counts[0, :N_EXPERTS].astype(jnp.int32)
    tiles_per = (count + MOE_TILE - 1) // MOE_TILE
    tile_end = jnp.cumsum(tiles_per)
    group_start = (tile_end - tiles_per) * MOE_TILE
    n_used = tile_end[-1:]
    tile_id = jnp.minimum(jnp.arange(n_tiles, dtype=jnp.int32), n_used[0] - 1)
    tile_expert = jnp.sum((tile_id[:, None] >= tile_end[None, :]).astype(jnp.int32), axis=1)
    idx = meta[_L_IDX:_L_IDX + 2].astype(jnp.int32)
    rank = meta[_L_RANK:_L_RANK + 2].astype(jnp.int32)
    start = jnp.sum(jnp.where(idx[:, :, None] == jnp.arange(N_EXPERTS), group_start, 0), axis=-1)
    pos = start + rank

    pad_start = jnp.concatenate([group_start + count, n_used * MOE_TILE])
    pad_len = jnp.concatenate([tiles_per * MOE_TILE - count, n_tiles - n_used])
    xs = _dispatch(pad_start, pad_len, pos[0], pos[1], x1, n_tiles * MOE_TILE)
    ys = _experts(tile_expert, n_used, xs, wg, wu, wd)
    return _combine(pos[0], pos[1], ys, x1, route, g2, b2, alpha)


def _t5_bucket(rel):
    half = N_BUCKETS // 2
    max_exact = half // 2
    ret = jnp.where(rel > 0, half, 0)
    n = jnp.abs(rel)
    large = max_exact + (jnp.log(jnp.maximum(n, 1).astype(F32) / max_exact)
                         / math.log(MAX_DISTANCE / max_exact) * (half - max_exact)).astype(jnp.int32)
    large = jnp.minimum(large, half - 1)
    return ret + jnp.where(n < max_exact, n, large)


def _block_bias(rel_bias):
    qi = jnp.arange(Q_BLOCK)[:, None]
    kj = jnp.arange(K_BLOCK)[None, :]
    rel = (kj - WINDOW_CHUNKS * CHUNK) - qi
    hit = _t5_bucket(rel)[None, :, :, None] == jnp.arange(N_BUCKETS)
    bias = jnp.sum(jnp.where(hit, rel_bias.T[:, None, None, :], 0.0), axis=-1)
    first = (kj // CHUNK) - (qi // CHUNK)
    visible = jnp.logical_and(first >= 0, first <= WINDOW_CHUNKS)
    return jnp.where(visible[None], bias, NEG_INF).astype(F32) * LOG2E


def _in_weights(w_in):
    q = w_in[:, :ATTN_WIDTH] * (HEAD_DIM ** -0.5 * LOG2E)
    k = w_in[:, ATTN_WIDTH:ATTN_WIDTH + KV_COLS]
    v = w_in[:, ATTN_WIDTH + KV_COLS:ATTN_WIDTH + 2 * KV_COLS]
    rest = w_in[:, ATTN_WIDTH + 2 * KV_COLS:]
    swap = lambda w: jnp.concatenate([w[:, HEAD_DIM:], w[:, :HEAD_DIM]], axis=1)
    return jnp.concatenate([q, k, swap(k), v, swap(v), rest], axis=1).astype(BF16)


def _block_diag(w):
    nb, d, e = w.shape
    eye = jnp.eye(nb, dtype=w.dtype)
    return (eye[:, None, :, None] * w[:, :, None, :]).reshape(nb * d, nb * e)


def kernel(x, rel_bias, w_in, attn_sinks, conv_w, conv_b, gate_r_w, gate_r_b, gate_i_w, gate_i_b,
           lru_lambda, norm_attn_g, norm_lru_g, w_out, ln1_g, ln1_b, ffn_w_gate, ffn_w_up, ffn_w_down,
           router_w, exp_w_gate, exp_w_up, exp_w_down, ln2_g, ln2_b):
    batch, seq, d = x.shape
    depth = w_in.shape[0]
    alpha = float((2 * depth) ** 0.25)
    bias2 = _block_bias(rel_bias)
    row = lambda v: v.reshape(1, -1)

    rides = {}
    flat = lambda w: w.reshape(-1, w.shape[-1])
    for l in range(depth):
        if l % 2 == 0:
            rides[(l, "lru")] = [ffn_w_gate[l // 2], ffn_w_up[l // 2], ffn_w_down[l // 2]]
        else:
            rides[(l - 1, "lru")] = rides.get((l - 1, "lru"), []) + [flat(exp_w_gate[l // 2])]
            rides[(l - 1, "attention")] = [flat(exp_w_up[l // 2])]
            rides[(l, "lru")] = [flat(exp_w_down[l // 2])]
    bf16_weights = {}

    h = x.reshape(batch * seq, d)
    for l in range(depth):
        w_gates = jnp.concatenate([_block_diag(gate_r_w[l]), _block_diag(gate_i_w[l])], axis=1).astype(BF16)
        b_gates = jnp.concatenate([gate_r_b[l], gate_i_b[l]]).reshape(1, -1)
        qkv, yl, *done = _inproj_lru(h, _in_weights(w_in[l]), conv_w[l], row(conv_b[l]), w_gates, b_gates,
                                     row(lru_lambda[l]), row(norm_lru_g[l]), seq, rides.get((l, "lru"), ()))
        bf16_weights[(l, "lru")] = done
        ya, *done = _attention(qkv, attn_sinks[l] * LOG2E, bias2, row(norm_attn_g[l]), batch, seq,
                               rides.get((l, "attention"), ()))
        bf16_weights[(l, "attention")] = done
        wo = w_out[l].astype(BF16)
        if l % 2 == 0:
            wg, wu, wd = bf16_weights[(l, "lru")][:3]
            h = _dense_block(ya, yl, h, wo, row(ln1_g[l]), row(ln1_b[l]), wg, wu, wd,
                             row(ln2_g[l]), row(ln2_b[l]), alpha)
        else:
            wg, (wu,), (wd,) = (bf16_weights[(l - 1, "lru")][-1], bf16_weights[(l - 1, "attention")],
                                bf16_weights[(l, "lru")])
            rw = jnp.pad(router_w[l // 2], ((0, 0), (0, LANES - N_EXPERTS)))
            x1, route, meta, counts = _router_block(ya, yl, h, wo, row(ln1_g[l]), row(ln1_b[l]), rw, alpha)
            h = _moe_block(x1, route, meta, counts, wg.reshape(exp_w_gate.shape[1:]),
                           wu.reshape(exp_w_up.shape[1:]), wd.reshape(exp_w_down.shape[1:]),
                           row(ln2_g[l]), row(ln2_b[l]), alpha)
    return h.reshape(batch, seq, d)
```

```python
import functools
import math

import jax
import jax.numpy as jnp
from jax import lax
from jax.experimental import pallas as pl
from jax.experimental.pallas import tpu as pltpu

F32 = jnp.float32
BF16 = jnp.bfloat16

D_MODEL = 1024
CHUNK = 64
HEAD_DIM = 64
N_Q_HEADS = 8
N_KV_HEADS = 2
ATTN_WIDTH = N_Q_HEADS * HEAD_DIM
KV_COLS = N_KV_HEADS * HEAD_DIM
WINDOW_CHUNKS = 2
LRU_WIDTH = 512
LRU_BLOCKS = 8
CONV_WIDTH = 4
LRU_C = 8.0
N_BUCKETS = 32
MAX_DISTANCE = 128
N_EXPERTS = 8
NEG_INF = -1e30
LN_EPS = 1e-5
RMS_EPS = 1e-6
LOG2E = math.log2(math.e)

LANES = 128
SUBLANES = 8
VMEM_LIMIT = 56 * 1024 * 1024

QKV_COLS = ATTN_WIDTH + 4 * KV_COLS
Q_BLOCK = 2 * CHUNK
K_BLOCK = (WINDOW_CHUNKS + 2) * CHUNK
ATTN_TILE = 512
LRU_TILE = 512
TOKEN_TILE = 512
ROUTER_SLABS = 2
MXU_TILE = 256
MOE_TILE = 512
MOE_F_TILE = 7 * MXU_TILE
DISPATCH_TILE = 1024
COMBINE_TILE = 256
DMA_UNROLL = 8


def _params(sem, **kw):
    return pltpu.CompilerParams(dimension_semantics=sem, vmem_limit_bytes=VMEM_LIMIT, **kw)


def _resident(shape):
    nd = len(shape)
    return pl.BlockSpec(shape, lambda *_: (0,) * nd, pipeline_mode=pl.Buffered(1))


def _layer_norm(x, g, b):
    mu = jnp.mean(x, axis=-1, keepdims=True)
    xc = x - mu
    var = jnp.mean(xc * xc, axis=-1, keepdims=True)
    return xc * lax.rsqrt(var + LN_EPS) * g + b


def _rms_norm(x, g):
    return x * lax.rsqrt(jnp.mean(x * x, axis=-1, keepdims=True) + RMS_EPS) * g


ROW_TILE = D_MODEL // LANES


def _load_row_tiles(ref, rows, lead=()):
    return jnp.concatenate(
        [ref[lead + (pl.ds(c, rows, stride=ROW_TILE), slice(None))] for c in range(ROW_TILE)], axis=1)


def _store_row_tiles(ref, value, first_row=0):
    for c in range(ROW_TILE):
        ref[pl.ds(first_row * ROW_TILE + c, value.shape[0], stride=ROW_TILE), :] = (
            value[:, c * LANES:(c + 1) * LANES])


BF16_ROWS = 2 * SUBLANES


def _with_casts(kernel_fn, n_in, n_out, n_casts):
    def wrapped(*refs):
        ins, refs = refs[:n_in], refs[n_in:]
        srcs, refs = refs[:n_casts], refs[n_casts:]
        outs, refs = refs[:n_out], refs[n_out:]
        dsts, scratch = refs[:n_casts], refs[n_casts:]
        for src, dst in zip(srcs, dsts):
            dst[...] = src[...].astype(BF16)
        kernel_fn(*ins, *outs, *scratch)
    return wrapped


def _cast_specs(weights, linear_step, steps):
    specs, shapes = [], []
    for w in weights:
        rows, cols = w.shape
        per = next(p for p in range(BF16_ROWS, rows + 1, BF16_ROWS) if rows % p == 0 and p * steps >= rows)
        last = rows // per - 1
        specs.append(pl.BlockSpec((per, cols),
                                  lambda *g, last=last: (jnp.minimum(linear_step(*g), last), 0)))
        shapes.append(jax.ShapeDtypeStruct(w.shape, BF16))
    return specs, shapes


ATTN_HIST = Q_BLOCK
K_COL = ATTN_WIDTH


def _attn_fill(kk_ref, vv_ref, qkv_ref):
    lo = lax.broadcasted_iota(jnp.int32, (1, LANES), 1) < HEAD_DIM
    zero = jnp.zeros((), BF16)
    rows = slice(ATTN_HIST, ATTN_HIST + ATTN_TILE)
    for dst, col in ((kk_ref, K_COL), (vv_ref, K_COL + 2 * LANES)):
        for j in range(4):
            dst[j, :ATTN_HIST, :] = dst[j, ATTN_TILE:, :]
        a = qkv_ref[:, col:col + LANES]
        s = qkv_ref[:, col + LANES:col + 2 * LANES]
        dst[0, rows, :] = jnp.where(lo, a, zero)
        dst[1, rows, :] = jnp.where(lo, zero, s)
        dst[2, rows, :] = jnp.where(lo, s, zero)
        dst[3, rows, :] = jnp.where(lo, zero, a)


def _attn_unit(sub, h, seq_start, qkv_ref, kk_ref, vv_ref, bias_ref, sink_ref, acc_ref):
    q0 = sub * Q_BLOCK
    lo = lax.broadcasted_iota(jnp.int32, (1, LANES), 1) < HEAD_DIM
    dn = (((1,), (1,)), ((), ()))
    start_mask = None
    if sub == 0:
        col = lax.broadcasted_iota(jnp.int32, (1, K_BLOCK), 1)
        start_mask = jnp.where(jnp.logical_and(seq_start, col < ATTN_HIST), NEG_INF, 0.0).astype(F32)
    q2 = jnp.concatenate(
        [qkv_ref[pl.ds(q0, Q_BLOCK), pl.ds((2 * h + p) * LANES, LANES)] for p in range(2)], axis=0)
    probs = {}
    inv_den = {}
    for half in range(2):
        kk = kk_ref[2 * h + half, pl.ds(q0, K_BLOCK), :]
        sc = lax.dot_general(q2, kk, dn, preferred_element_type=F32)
        for p in range(2):
            head = 4 * h + 2 * p + half
            logit = sc[p * Q_BLOCK:(p + 1) * Q_BLOCK] + bias_ref[head]
            if start_mask is not None:
                logit = logit + start_mask
            sink = sink_ref[head]
            m = jnp.maximum(jnp.max(logit, axis=-1, keepdims=True), sink)
            e = jnp.exp2(logit - m)
            den = jnp.sum(e, axis=-1, keepdims=True) + jnp.exp2(sink - m)
            probs[(p, half)] = e.astype(BF16)
            inv_den[(p, half)] = 1.0 / den
    lhs = jnp.concatenate(
        [jnp.concatenate([probs[(p, 0)], probs[(p, 1)]], axis=1) for p in range(2)], axis=0)
    rhs = jnp.concatenate([vv_ref[2 * h, pl.ds(q0, K_BLOCK), :],
                           vv_ref[2 * h + 1, pl.ds(q0, K_BLOCK), :]], axis=0)
    out = jnp.dot(lhs, rhs, preferred_element_type=F32)
    for p in range(2):
        scale = jnp.where(lo, inv_den[(p, 0)], inv_den[(p, 1)])
        acc_ref[pl.ds(q0, Q_BLOCK), pl.ds((2 * h + p) * LANES, LANES)] = (
            out[p * Q_BLOCK:(p + 1) * Q_BLOCK] * scale)


def _lru_slab(c, x, gate, prev, h_in, cw_ref, cb_ref, wg_ref, bg_ref, lam_ref, a_ref, u_ref, ext_ref):
    t = x.shape[0]
    lanes = slice(c * LANES, (c + 1) * LANES)
    i_lanes = slice(LRU_WIDTH + c * LANES, LRU_WIDTH + (c + 1) * LANES)

    ext_ref[c, :SUBLANES, :] = prev
    ext_ref[c, SUBLANES:, :] = x
    xc = x * cw_ref[CONV_WIDTH - 1:CONV_WIDTH, lanes] + cb_ref[:, lanes]
    for k in range(1, CONV_WIDTH):
        xc = xc + (ext_ref[c, SUBLANES - k:SUBLANES - k + t, :]
                   * cw_ref[CONV_WIDTH - 1 - k:CONV_WIDTH - k, lanes])

    xcb = xc.astype(BF16)
    r = jax.nn.sigmoid(jnp.dot(xcb, wg_ref[lanes, lanes], preferred_element_type=F32) + bg_ref[:, lanes])
    i = jax.nn.sigmoid(jnp.dot(xcb, wg_ref[lanes, i_lanes], preferred_element_type=F32)
                       + bg_ref[:, i_lanes])
    neg_lam = -lam_ref[:, lanes]
    softplus = jnp.maximum(neg_lam, 0.0) + jnp.log1p(jnp.exp(-jnp.abs(neg_lam)))
    log_a = (-LRU_C) * r * softplus
    a = jnp.exp(log_a)
    h = jnp.sqrt(jnp.tanh(-log_a) * (1.0 + a * a)) * (i * xc)

    def doubling(av, hv, index, length, axis):
        d = 1
        while d < length:
            keep = index >= d
            a_sh = jnp.where(keep, pltpu.roll(av, d, axis), 1.0)
            h_sh = jnp.where(keep, pltpu.roll(hv, d, axis), 0.0)
            hv = av * h_sh + hv
            av = av * a_sh
            d *= 2
        return av, hv

    groups = t // SUBLANES
    grouped = (groups, SUBLANES, LANES)
    sub = lax.broadcasted_iota(jnp.int32, (1, SUBLANES, 1), 1)
    a, h = doubling(a.reshape(grouped), h.reshape(grouped), sub, SUBLANES, 1)
    a = a.reshape(t, LANES)
    h = h.reshape(t, LANES)
    grow = lax.broadcasted_iota(jnp.int32, (groups, 1), 0)

    def group_last(ref, value):
        ref[c] = value
        return ref[c, pl.ds(SUBLANES - 1, groups, stride=SUBLANES), :]

    ag, hg = doubling(group_last(a_ref, a), group_last(u_ref, h), grow, groups, 0)
    state = hg + ag * h_in
    enter = jnp.where(grow >= 1, pltpu.roll(state, 1, 0), h_in)
    enter = jnp.broadcast_to(enter[:, None, :], grouped).reshape(t, LANES)
    h = h + a * enter
    return jax.nn.gelu(gate) * h, state[groups - 1:groups, :]


def _mixer_kernel(sink_ref, x_ref, w_ref, cw_ref, cb_ref, wg_ref, bg_ref, lam_ref, gl_ref, bias_ref, ga_ref,
                  ya_ref, yl_ref,
                  q0_ref, q1_ref, z0_ref, z1_ref, kk_ref, vv_ref, acc_ref,
                  tail_ref, h_ref, a_ref, u_ref, ext_ref, y_ref, *, tiles_per_seq):
    s = pl.program_id(0)

    @pl.when(s == 0)
    def _():
        for ref in (q1_ref, z1_ref, kk_ref, vv_ref, tail_ref, h_ref):
            ref[...] = jnp.zeros_like(ref)

    def step(q_new_ref, z_new_ref, q_old_ref, z_old_ref):
        xb = x_ref[...].astype(BF16)
        seq_start = lax.rem(s - 1, tiles_per_seq) == 0

        def inproj_chunk(j):
            cols = slice(j * MXU_TILE, (j + 1) * MXU_TILE)
            z = jnp.dot(xb, w_ref[:, cols], preferred_element_type=F32)
            if cols.stop <= QKV_COLS:
                q_new_ref[:, cols] = z.astype(BF16)
            else:
                z_new_ref[:, cols.start - QKV_COLS:cols.stop - QKV_COLS] = z

        _attn_fill(kk_ref, vv_ref, q_old_ref)
        attn_units = [(sub, h) for sub in range(ATTN_TILE // Q_BLOCK) for h in range(N_KV_HEADS)]
        slabs = LRU_WIDTH // LANES
        chunks_per_slab = (QKV_COLS + 2 * LRU_WIDTH) // MXU_TILE // slabs
        units_per_slab = len(attn_units) // slabs
        sq = None
        for c in range(slabs):
            lanes = slice(c * LANES, (c + 1) * LANES)
            inproj_chunk(chunks_per_slab * c)
            x = z_old_ref[:, lanes]
            prev = jnp.where(seq_start, 0.0, tail_ref[:, lanes])
            h_in = jnp.where(seq_start, 0.0, h_ref[:, lanes])
            y, h_out = _lru_slab(c, x, z_old_ref[:, LRU_WIDTH + c * LANES:LRU_WIDTH + (c + 1) * LANES],
                                 prev, h_in, cw_ref, cb_ref, wg_ref, bg_ref, lam_ref, a_ref, u_ref, ext_ref)
            y_ref[:, lanes] = y
            part = jnp.sum(y * y, axis=-1, keepdims=True)
            sq = part if sq is None else sq + part
            h_ref[:, lanes] = h_out
            tail_ref[:, lanes] = x[LRU_TILE - SUBLANES:, :]
            for k in range(max(chunks_per_slab - 1, units_per_slab)):
                if k < units_per_slab:
                    sub, h = attn_units[units_per_slab * c + k]
                    _attn_unit(sub, h, seq_start, q_old_ref, kk_ref, vv_ref, bias_ref, sink_ref, acc_ref)
                if k < chunks_per_slab - 1:
                    inproj_chunk(chunks_per_slab * c + 1 + k)
        scale = lax.rsqrt(sq * (1.0 / LRU_WIDTH) + RMS_EPS)
        yl_ref[...] = (y_ref[...] * scale * gl_ref[...]).astype(BF16)
        ya_ref[...] = _rms_norm(acc_ref[...], ga_ref[...]).astype(BF16)

    @pl.when(s % 2 == 0)
    def _():
        step(q0_ref, z0_ref, q1_ref, z1_ref)

    @pl.when(s % 2 == 1)
    def _():
        step(q1_ref, z1_ref, q0_ref, z0_ref)


def _mixer(x2d, w, sinks, bias2, g_attn, conv_w, conv_b, w_gates, b_gates, lam, g_lru, seq, casts=()):
    n = x2d.shape[0]
    assert ATTN_TILE == LRU_TILE
    tiles = n // LRU_TILE
    last = tiles - 1
    cast_specs, cast_shapes = _cast_specs(casts, lambda s: s, tiles + 1)
    out_tile = lambda width: pl.BlockSpec((LRU_TILE, width), lambda s: (jnp.maximum(s - 1, 0), 0))
    return pl.pallas_call(
        _with_casts(functools.partial(_mixer_kernel, tiles_per_seq=seq // LRU_TILE), 11, 2, len(casts)),
        grid=(tiles + 1,),
        in_specs=[pl.BlockSpec(memory_space=pltpu.SMEM),
                  pl.BlockSpec((LRU_TILE, D_MODEL), lambda s: (jnp.minimum(s, last), 0)),
                  _resident((D_MODEL, QKV_COLS + 2 * LRU_WIDTH)),
                  _resident((CONV_WIDTH, LRU_WIDTH)),
                  _resident((1, LRU_WIDTH)),
                  _resident((LRU_WIDTH, 2 * LRU_WIDTH)),
                  _resident((1, 2 * LRU_WIDTH)),
                  _resident((1, LRU_WIDTH)),
                  _resident((1, LRU_WIDTH)),
                  _resident((N_Q_HEADS, Q_BLOCK, K_BLOCK)),
                  _resident((1, ATTN_WIDTH))] + cast_specs,
        out_specs=[out_tile(ATTN_WIDTH), out_tile(LRU_WIDTH)] + cast_specs,
        out_shape=[jax.ShapeDtypeStruct((n, ATTN_WIDTH), BF16),
                   jax.ShapeDtypeStruct((n, LRU_WIDTH), BF16)] + cast_shapes,
        scratch_shapes=[pltpu.VMEM((ATTN_TILE, QKV_COLS), BF16),
                        pltpu.VMEM((ATTN_TILE, QKV_COLS), BF16),
                        pltpu.VMEM((LRU_TILE, 2 * LRU_WIDTH), F32),
                        pltpu.VMEM((LRU_TILE, 2 * LRU_WIDTH), F32),
                        pltpu.VMEM((4, ATTN_HIST + ATTN_TILE, LANES), BF16),
                        pltpu.VMEM((4, ATTN_HIST + ATTN_TILE, LANES), BF16),
                        pltpu.VMEM((ATTN_TILE, ATTN_WIDTH), F32),
                        pltpu.VMEM((SUBLANES, LRU_WIDTH), F32),
                        pltpu.VMEM((1, LRU_WIDTH), F32),
                        pltpu.VMEM((LRU_WIDTH // LANES, LRU_TILE, LANES), F32),
                        pltpu.VMEM((LRU_WIDTH // LANES, LRU_TILE, LANES), F32),
                        pltpu.VMEM((LRU_WIDTH // LANES, SUBLANES + LRU_TILE, LANES), F32),
                        pltpu.VMEM((LRU_TILE, LRU_WIDTH), F32)],
        compiler_params=_params(("arbitrary",)),
        name="mixer",
    )(sinks, x2d, w, conv_w, conv_b, w_gates, b_gates, lam, g_lru, bias2, g_attn, *casts)


def _mix_ln1(ya_ref, yl_ref, x_ref, wo_ref, g1_ref, b1_ref, alpha, rows=slice(None)):
    y = jnp.dot(ya_ref[rows, :], wo_ref[:ATTN_WIDTH, :], preferred_element_type=F32)
    y = y + jnp.dot(yl_ref[rows, :], wo_ref[ATTN_WIDTH:, :], preferred_element_type=F32)
    return _layer_norm(alpha * x_ref[rows, :] + y, g1_ref[...], b1_ref[...])


def _dense_block_kernel(ya_ref, yl_ref, x_ref, wo_ref, g1_ref, b1_ref, wg_ref, wu_ref, wd_ref,
                        g2_ref, b2_ref, o_ref, *, alpha, f_splits):
    x1 = _mix_ln1(ya_ref, yl_ref, x_ref, wo_ref, g1_ref, b1_ref, alpha)
    xb = x1.astype(BF16)
    f = None
    for lo, hi in zip(f_splits[:-1], f_splits[1:]):
        cols = slice(lo, hi)
        g = jnp.dot(xb, wg_ref[:, cols], preferred_element_type=F32)
        u = jnp.dot(xb, wu_ref[:, cols], preferred_element_type=F32)
        hid = (jax.nn.silu(g) * u).astype(BF16)
        part = jnp.dot(hid, wd_ref[cols, :], preferred_element_type=F32)
        f = part if f is None else f + part
    o_ref[...] = _layer_norm(alpha * x1 + f, g2_ref[...], b2_ref[...])


def _dense_block(ya, yl, x2d, wo, g1, b1, wg, wu, wd, g2, b2, alpha):
    n = x2d.shape[0]
    d_ff = wg.shape[1]
    tile = lambda w: pl.BlockSpec((TOKEN_TILE, w), lambda i: (i, 0))
    mxu_tiles = pl.cdiv(d_ff, MXU_TILE)
    f_splits = (0, (mxu_tiles + 1) // 2 * MXU_TILE, d_ff)
    return pl.pallas_call(
        functools.partial(_dense_block_kernel, alpha=alpha, f_splits=f_splits),
        grid=(n // TOKEN_TILE,),
        in_specs=[tile(ATTN_WIDTH), tile(LRU_WIDTH), tile(D_MODEL),
                  _resident((D_MODEL, D_MODEL)), _resident((1, D_MODEL)), _resident((1, D_MODEL)),
                  _resident((D_MODEL, d_ff)), _resident((D_MODEL, d_ff)), _resident((d_ff, D_MODEL)),
                  _resident((1, D_MODEL)), _resident((1, D_MODEL))],
        out_specs=tile(D_MODEL),
        out_shape=jax.ShapeDtypeStruct((n, D_MODEL), F32),
        compiler_params=_params(("parallel",)),
        name="dense_block",
    )(ya, yl, x2d, wo, g1, b1, wg, wu, wd, g2, b2)


_L_IDX, _L_W, _L_RANK = 0, 2, 4


def _router_kernel(ya_ref, yl_ref, x_ref, wo_ref, g1_ref, b1_ref, rw_ref,
                   x1_ref, route_ref, meta_ref, count_ref, carry_ref, *, alpha):
    @pl.when(pl.program_id(0) == 0)
    def _():
        carry_ref[...] = jnp.zeros_like(carry_ref)

    t = x_ref.shape[0] // ROUTER_SLABS
    carry = carry_ref[:, :1]
    for s in range(ROUTER_SLABS):
        carry = _route_slab(s * t, t, carry, ya_ref, yl_ref, x_ref, wo_ref, g1_ref, b1_ref, rw_ref,
                            x1_ref, route_ref, meta_ref, alpha)
    carry_ref[...] = jnp.broadcast_to(carry, carry_ref.shape)
    count_ref[...] = jnp.broadcast_to(carry, count_ref.shape)


def _route_slab(first, t, carry, ya_ref, yl_ref, x_ref, wo_ref, g1_ref, b1_ref, rw_ref,
                x1_ref, route_ref, meta_ref, alpha):
    x1 = _mix_ln1(ya_ref, yl_ref, x_ref, wo_ref, g1_ref, b1_ref, alpha, pl.ds(first, t))
    _store_row_tiles(x1_ref, x1, first)
    x_hi = x1.astype(BF16)
    x_lo = (x1 - x_hi.astype(F32)).astype(BF16)
    both = jnp.dot(x_hi, rw_ref[...], preferred_element_type=F32)
    logits = both[:, :LANES] + both[:, LANES:] + jnp.dot(x_lo, rw_ref[:, :LANES], preferred_element_type=F32)

    lt = logits.T[:N_EXPERTS]
    row = lax.broadcasted_iota(jnp.int32, (N_EXPERTS, t), 0).astype(F32)
    lowest = jnp.finfo(F32).min
    m1 = jnp.max(lt, axis=0, keepdims=True)
    i1 = jnp.min(jnp.where(lt == m1, row, float(N_EXPERTS)), axis=0, keepdims=True)
    l2 = jnp.where(row == i1, lowest, lt)
    m2 = jnp.max(l2, axis=0, keepdims=True)
    i2 = jnp.min(jnp.where(l2 == m2, row, float(N_EXPERTS)), axis=0, keepdims=True)
    e = jnp.exp(m2 - m1)
    w1 = 1.0 / (1.0 + e)
    w2 = e / (1.0 + e)

    sel = jnp.where(jnp.logical_or(row == i1, row == i2), 1.0, 0.0)
    r_i = lax.broadcasted_iota(jnp.int32, (t, t), 0)
    c_i = lax.broadcasted_iota(jnp.int32, (t, t), 1)
    earlier = jnp.where(r_i < c_i, 1.0, 0.0).astype(BF16)
    before = jnp.dot(sel.astype(BF16), earlier, preferred_element_type=F32) + carry
    rank1 = jnp.sum(jnp.where(row == i1, before, 0.0), axis=0, keepdims=True)
    rank2 = jnp.sum(jnp.where(row == i2, before, 0.0), axis=0, keepdims=True)

    meta = jnp.where(row == _L_IDX, i1, 0.0)
    meta = jnp.where(row == _L_IDX + 1, i2, meta)
    meta = jnp.where(row == _L_W, w1, meta)
    meta = jnp.where(row == _L_W + 1, w2, meta)
    meta = jnp.where(row == _L_RANK, rank1, meta)
    meta = jnp.where(row == _L_RANK + 1, rank2, meta)
    meta_ref[:, pl.ds(first, t)] = meta
    padded = jnp.concatenate([meta, jnp.zeros((LANES - N_EXPERTS, t), F32)], axis=0)
    route_ref[pl.ds(first, t), :] = padded.T
    return carry + jnp.sum(sel, axis=1, keepdims=True)


def _router_block(ya, yl, x2d, wo, g1, b1, router_w, alpha):
    n = x2d.shape[0]
    assert router_w.shape[1] == N_EXPERTS == SUBLANES
    rw = jnp.pad(router_w, ((0, 0), (0, LANES - N_EXPERTS)))
    w_hi = rw.astype(BF16)
    w_lo = (rw - w_hi.astype(F32)).astype(BF16)
    tile = lambda w: pl.BlockSpec((TOKEN_TILE, w), lambda i: (i, 0))
    return pl.pallas_call(
        functools.partial(_router_kernel, alpha=alpha),
        grid=(n // TOKEN_TILE,),
        in_specs=[tile(ATTN_WIDTH), tile(LRU_WIDTH), tile(D_MODEL),
                  _resident((D_MODEL, D_MODEL)), _resident((1, D_MODEL)), _resident((1, D_MODEL)),
                  _resident((D_MODEL, 2 * LANES))],
        out_specs=[pl.BlockSpec((TOKEN_TILE * ROW_TILE, LANES), lambda i: (i, 0)), tile(LANES),
                   pl.BlockSpec((SUBLANES, TOKEN_TILE), lambda i: (0, i)),
                   pl.BlockSpec((N_EXPERTS, LANES), lambda i: (0, 0))],
        out_shape=[jax.ShapeDtypeStruct((n * ROW_TILE, LANES), F32),
                   jax.ShapeDtypeStruct((n, LANES), F32),
                   jax.ShapeDtypeStruct((SUBLANES, n), F32),
                   jax.ShapeDtypeStruct((N_EXPERTS, LANES), F32)],
        scratch_shapes=[pltpu.VMEM((N_EXPERTS, LANES), F32)],
        compiler_params=_params(("arbitrary",)),
        name="router_block",
    )(ya, yl, x2d, wo, g1, b1, jnp.concatenate([w_hi, w_lo], axis=1))


def _tile_copy(src_ref, src_row, dst_ref, dst_row, sem):
    src = src_ref.at[pl.ds(pl.multiple_of(src_row * ROW_TILE, ROW_TILE), ROW_TILE)]
    dst = dst_ref.at[pl.ds(pl.multiple_of(dst_row * ROW_TILE, ROW_TILE), ROW_TILE)]
    return pltpu.make_async_copy(src, dst, sem)


def _dispatch_kernel(pad_start_ref, pad_len_ref, pos0_ref, pos1_ref, x_ref, xs_ref, sem, pad_sem):
    def issue(t, carry):
        _tile_copy(x_ref, t, xs_ref, pos0_ref[t], sem).start(priority=0)
        _tile_copy(x_ref, t, xs_ref, pos1_ref[t], sem).start(priority=1)
        return carry

    lax.fori_loop(0, DISPATCH_TILE, issue, 0, unroll=DMA_UNROLL)
    for _ in range(2):
        pltpu.make_async_copy(x_ref, xs_ref.at[pl.ds(0, DISPATCH_TILE * ROW_TILE)], sem).wait()

    @pl.when(pl.program_id(0) == pl.num_programs(0) - 1)
    def _():
        for e in range(N_EXPERTS):
            def fill(r, carry):
                _tile_copy(x_ref, 0, xs_ref, pad_start_ref[e] + r, pad_sem).start()
                return carry

            def drain(r, carry):
                _tile_copy(x_ref, 0, xs_ref, 0, pad_sem).wait()
                return carry

            lax.fori_loop(0, pad_len_ref[e], fill, 0)
            lax.fori_loop(0, pad_len_ref[e], drain, 0)

        def idle_copy(r):
            first = pl.multiple_of((pad_start_ref[N_EXPERTS] + r * MOE_TILE) * ROW_TILE, MOE_TILE * ROW_TILE)
            return pltpu.make_async_copy(x_ref.at[pl.ds(0, MOE_TILE * ROW_TILE)],
                                         xs_ref.at[pl.ds(first, MOE_TILE * ROW_TILE)], pad_sem)

        def fill_idle(r, carry):
            idle_copy(r).start()
            return carry

        def drain_idle(r, carry):
            idle_copy(r).wait()
            return carry

        lax.fori_loop(0, pad_len_ref[N_EXPERTS], fill_idle, 0)
        lax.fori_loop(0, pad_len_ref[N_EXPERTS], drain_idle, 0)


def _dispatch(pad_start, pad_len, pos0, pos1, x1, sorted_rows):
    n = x1.shape[0] // ROW_TILE
    assert DISPATCH_TILE >= MOE_TILE
    smem_tile = pl.BlockSpec((DISPATCH_TILE,), lambda i, *_: (i,), memory_space=pltpu.SMEM)
    grid_spec = pltpu.PrefetchScalarGridSpec(
        num_scalar_prefetch=2,
        grid=(n // DISPATCH_TILE,),
        in_specs=[smem_tile, smem_tile,
                  pl.BlockSpec((DISPATCH_TILE * ROW_TILE, LANES), lambda i, *_: (i, 0))],
        out_specs=pl.BlockSpec(memory_space=pl.ANY),
        scratch_shapes=[pltpu.SemaphoreType.DMA(()), pltpu.SemaphoreType.DMA(())],
    )
    return pl.pallas_call(
        _dispatch_kernel,
        grid_spec=grid_spec,
        out_shape=jax.ShapeDtypeStruct((sorted_rows * ROW_TILE, LANES), F32),
        compiler_params=_params(("arbitrary",), disable_bounds_checks=True),
        name="moe_dispatch",
    )(pad_start, pad_len, pos0, pos1, x1)


def _experts_kernel(tile_expert_ref, n_used_ref, x_ref, wg_ref, wu_ref, wd_ref, y_ref, xb_ref, acc_ref):
    del tile_expert_ref
    i = pl.program_id(0)
    j = pl.program_id(1)
    last = pl.num_programs(1) - 1

    @pl.when(i < n_used_ref[0])
    def _():
        @pl.when(j == 0)
        def _():
            for c in range(ROW_TILE):
                xb_ref[:, c * LANES:(c + 1) * LANES] = (
                    x_ref[pl.ds(c, MOE_TILE, stride=ROW_TILE), :].astype(BF16))

        xb = xb_ref[...]
        g = jnp.dot(xb, wg_ref[...], preferred_element_type=F32)
        u = jnp.dot(xb, wu_ref[...], preferred_element_type=F32)
        hid = (jax.nn.silu(g) * u).astype(BF16)
        part = jnp.dot(hid, wd_ref[...], preferred_element_type=F32)

        @pl.when(j == 0)
        def _():
            acc_ref[...] = part

        @pl.when(jnp.logical_and(j > 0, j < last))
        def _():
            acc_ref[...] += part

        @pl.when(j == last)
        def _():
            _store_row_tiles(y_ref, acc_ref[...] + part)

    @pl.when(jnp.logical_and(i >= n_used_ref[0], j == 0))
    def _():
        y_ref[...] = jnp.zeros_like(y_ref)


def _experts(tile_expert, n_used, xs, wg, wu, wd):
    rows = xs.shape[0] // ROW_TILE
    d_ff = wg.shape[2]
    n_tiles = rows // MOE_TILE
    n_f = d_ff // MOE_F_TILE
    assert n_f >= 2

    def row_map(i, j, te, nu):
        return (jnp.minimum(i, nu[0] - 1), 0)

    def f_of(i, j, nu):
        return jnp.where(i < nu[0], j, n_f - 1)

    grid_spec = pltpu.PrefetchScalarGridSpec(
        num_scalar_prefetch=2,
        grid=(n_tiles, n_f),
        in_specs=[pl.BlockSpec((MOE_TILE * ROW_TILE, LANES), row_map),
                  pl.BlockSpec((None, D_MODEL, MOE_F_TILE), lambda i, j, te, nu: (te[i], 0, f_of(i, j, nu))),
                  pl.BlockSpec((None, D_MODEL, MOE_F_TILE), lambda i, j, te, nu: (te[i], 0, f_of(i, j, nu))),
                  pl.BlockSpec((None, MOE_F_TILE, D_MODEL), lambda i, j, te, nu: (te[i], f_of(i, j, nu), 0))],
        out_specs=pl.BlockSpec((MOE_TILE * ROW_TILE, LANES), lambda i, j, te, nu: (i, 0)),
        scratch_shapes=[pltpu.VMEM((MOE_TILE, D_MODEL), BF16), pltpu.VMEM((MOE_TILE, D_MODEL), F32)],
    )
    return pl.pallas_call(
        _experts_kernel,
        grid_spec=grid_spec,
        out_shape=jax.ShapeDtypeStruct((rows * ROW_TILE, LANES), F32),
        compiler_params=_params(("arbitrary", "arbitrary")),
        name="moe_experts",
    )(tile_expert, n_used, xs, wg, wu, wd)


def _combine_kernel(pos0_ref, pos1_ref, nxt0_ref, nxt1_ref, ys_ref, x1_ref, route_ref, g2_ref, b2_ref,
                    o_ref, buf_ref, sem, *, alpha):
    i = pl.program_id(0)
    slot = i % 2

    def issue(p0_ref, p1_ref, dst_slot):
        def body(t, carry):
            for k, p_ref in enumerate((p0_ref, p1_ref)):
                _tile_copy(ys_ref, p_ref[t], buf_ref.at[dst_slot, k], t, sem.at[dst_slot]).start(priority=k)
            return carry
        lax.fori_loop(0, COMBINE_TILE, body, 0, unroll=DMA_UNROLL)

    @pl.when(i == 0)
    def _():
        issue(pos0_ref, pos1_ref, 0)

    @pl.when(i + 1 < pl.num_programs(0))
    def _():
        issue(nxt0_ref, nxt1_ref, 1 - slot)

    for k in range(2):
        pltpu.make_async_copy(ys_ref.at[pl.ds(0, COMBINE_TILE * ROW_TILE)], buf_ref.at[slot, k],
                              sem.at[slot]).wait()
    route = route_ref[...]
    y0 = _load_row_tiles(buf_ref, COMBINE_TILE, (slot, 0))
    y1 = _load_row_tiles(buf_ref, COMBINE_TILE, (slot, 1))
    f = route[:, _L_W:_L_W + 1] * y0 + route[:, _L_W + 1:_L_W + 2] * y1
    x1 = _load_row_tiles(x1_ref, COMBINE_TILE)
    o_ref[...] = _layer_norm(alpha * x1 + f, g2_ref[...], b2_ref[...])


def _combine(pos0, pos1, ys, x1, route, g2, b2, alpha):
    n = x1.shape[0] // ROW_TILE
    steps = n // COMBINE_TILE
    tile = lambda w: pl.BlockSpec((COMBINE_TILE, w), lambda i: (i, 0))
    smem_cur = pl.BlockSpec((COMBINE_TILE,), lambda i: (i,), memory_space=pltpu.SMEM)
    smem_next = pl.BlockSpec((COMBINE_TILE,), lambda i: (jnp.minimum(i + 1, steps - 1),),
                             memory_space=pltpu.SMEM)
    return pl.pallas_call(
        functools.partial(_combine_kernel, alpha=alpha),
        grid=(steps,),
        in_specs=[smem_cur, smem_cur, smem_next, smem_next,
                  pl.BlockSpec(memory_space=pl.ANY),
                  pl.BlockSpec((COMBINE_TILE * ROW_TILE, LANES), lambda i: (i, 0)),
                  tile(LANES), _resident((1, D_MODEL)), _resident((1, D_MODEL))],
        out_specs=tile(D_MODEL),
        out_shape=jax.ShapeDtypeStruct((n, D_MODEL), F32),
        scratch_shapes=[pltpu.VMEM((2, 2, COMBINE_TILE * ROW_TILE, LANES), F32),
                        pltpu.SemaphoreType.DMA((2,))],
        compiler_params=_params(("arbitrary",), disable_bounds_checks=True),
        name="moe_combine",
    )(pos0, pos1, pos0, pos1, ys, x1, route, g2, b2)


def _moe_block(x1, route, meta, counts, wg, wu, wd, g2, b2, alpha):
    n = route.shape[0]
    n_tiles = (2 * n) // MOE_TILE + N_EXPERTS
    count = counts[:, 0].astype(jnp.int32)
    tiles_per = (count + MOE_TILE - 1) // MOE_TILE
    tile_end = jnp.cumsum(tiles_per)
    group_start = (tile_end - tiles_per) * MOE_TILE
    n_used = tile_end[-1:]
    tile_id = jnp.minimum(jnp.arange(n_tiles, dtype=jnp.int32), n_used[0] - 1)
    tile_expert = jnp.sum((tile_id[:, None] >= tile_end[None, :]).astype(jnp.int32), axis=1)
    idx = meta[_L_IDX:_L_IDX + 2].astype(jnp.int32)
    rank = meta[_L_RANK:_L_RANK + 2].astype(jnp.int32)
    start = jnp.sum(jnp.where(idx[:, :, None] == jnp.arange(N_EXPERTS), group_start, 0), axis=-1)
    pos = start + rank

    pad_start = jnp.concatenate([group_start + count, n_used * MOE_TILE])
    pad_len = jnp.concatenate([tiles_per * MOE_TILE - count, n_tiles - n_used])
    xs = _dispatch(pad_start, pad_len, pos[0], pos[1], x1, n_tiles * MOE_TILE)
    ys = _experts(tile_expert, n_used, xs, wg, wu, wd)
    return _combine(pos[0], pos[1], ys, x1, route, g2, b2, alpha)


def _t5_bucket(rel):
    half = N_BUCKETS // 2
    max_exact = half // 2
    ret = jnp.where(rel > 0, half, 0)
    n = jnp.abs(rel)
    large = max_exact + (jnp.log(jnp.maximum(n, 1).astype(F32) / max_exact)
                         / math.log(MAX_DISTANCE / max_exact) * (half - max_exact)).astype(jnp.int32)
    large = jnp.minimum(large, half - 1)
    return ret + jnp.where(n < max_exact, n, large)


def _block_bias(rel_bias):
    qi = jnp.arange(Q_BLOCK)[:, None]
    kj = jnp.arange(K_BLOCK)[None, :]
    rel = (kj - WINDOW_CHUNKS * CHUNK) - qi
    hit = _t5_bucket(rel)[None, :, :, None] == jnp.arange(N_BUCKETS)
    bias = jnp.sum(jnp.where(hit, rel_bias.T[:, None, None, :], 0.0), axis=-1)
    first = (kj // CHUNK) - (qi // CHUNK)
    visible = jnp.logical_and(first >= 0, first <= WINDOW_CHUNKS)
    return jnp.where(visible[None], bias, NEG_INF).astype(F32) * LOG2E


def _in_weights(w_in):
    q = w_in[:, :ATTN_WIDTH] * (HEAD_DIM ** -0.5 * LOG2E)
    k = w_in[:, ATTN_WIDTH:ATTN_WIDTH + KV_COLS]
    v = w_in[:, ATTN_WIDTH + KV_COLS:ATTN_WIDTH + 2 * KV_COLS]
    rest = w_in[:, ATTN_WIDTH + 2 * KV_COLS:]
    swap = lambda w: jnp.concatenate([w[:, HEAD_DIM:], w[:, :HEAD_DIM]], axis=1)
    return jnp.concatenate([q, k, swap(k), v, swap(v), rest], axis=1).astype(BF16)


def _block_diag(w):
    nb, d, e = w.shape
    eye = jnp.eye(nb, dtype=w.dtype)
    return (eye[:, None, :, None] * w[:, :, None, :]).reshape(nb * d, nb * e)


def kernel(x, rel_bias, w_in, attn_sinks, conv_w, conv_b, gate_r_w, gate_r_b, gate_i_w, gate_i_b,
           lru_lambda, norm_attn_g, norm_lru_g, w_out, ln1_g, ln1_b, ffn_w_gate, ffn_w_up, ffn_w_down,
           router_w, exp_w_gate, exp_w_up, exp_w_down, ln2_g, ln2_b):
    batch, seq, d = x.shape
    depth = w_in.shape[0]
    alpha = float((2 * depth) ** 0.25)
    bias2 = _block_bias(rel_bias)
    row = lambda v: v.reshape(1, -1)

    rides = {l: [] for l in range(depth)}
    flat = lambda w: w.reshape(-1, w.shape[-1])
    for l in range(depth):
        if l % 2 == 0:
            rides[l] += [ffn_w_gate[l // 2], ffn_w_up[l // 2], ffn_w_down[l // 2]]
        else:
            rides[l - 1] += [flat(exp_w_gate[l // 2]), flat(exp_w_up[l // 2])]
            rides[l] += [flat(exp_w_down[l // 2])]
    bf16_weights = {}

    h = x.reshape(batch * seq, d)
    for l in range(depth):
        w_gates = jnp.concatenate([_block_diag(gate_r_w[l]), _block_diag(gate_i_w[l])], axis=1).astype(BF16)
        b_gates = jnp.concatenate([gate_r_b[l], gate_i_b[l]]).reshape(1, -1)
        ya, yl, *bf16_weights[l] = _mixer(h, _in_weights(w_in[l]), attn_sinks[l] * LOG2E, bias2,
                                          row(norm_attn_g[l]), conv_w[l], row(conv_b[l]), w_gates, b_gates,
                                          row(lru_lambda[l]), row(norm_lru_g[l]), seq, rides[l])
        wo = w_out[l].astype(BF16)
        if l % 2 == 0:
            wg, wu, wd = bf16_weights[l][:3]
            h = _dense_block(ya, yl, h, wo, row(ln1_g[l]), row(ln1_b[l]), wg, wu, wd,
                             row(ln2_g[l]), row(ln2_b[l]), alpha)
        else:
            (wg, wu), wd = bf16_weights[l - 1][-2:], bf16_weights[l][-1]
            x1, route, meta, counts = _router_block(ya, yl, h, wo, row(ln1_g[l]), row(ln1_b[l]),
                                                    router_w[l // 2], alpha)
            h = _moe_block(x1, route, meta, counts, wg.reshape(exp_w_gate.shape[1:]),
                           wu.reshape(exp_w_up.shape[1:]), wd.reshape(exp_w_down.shape[1:]),
                           row(ln2_g[l]), row(ln2_b[l]), alpha)
    return h.reshape(batch, seq, d)
```

```python
import functools
import math

import jax
import jax.numpy as jnp
from jax import lax
from jax.experimental import pallas as pl
from jax.experimental.pallas import tpu as pltpu

F32 = jnp.float32
BF16 = jnp.bfloat16

D_MODEL = 1024
CHUNK = 64
HEAD_DIM = 64
N_Q_HEADS = 8
N_KV_HEADS = 2
ATTN_WIDTH = N_Q_HEADS * HEAD_DIM
KV_COLS = N_KV_HEADS * HEAD_DIM
WINDOW_CHUNKS = 2
LRU_WIDTH = 512
LRU_BLOCKS = 8
CONV_WIDTH = 4
LRU_C = 8.0
N_BUCKETS = 32
MAX_DISTANCE = 128
N_EXPERTS = 8
NEG_INF = -1e30
LN_EPS = 1e-5
RMS_EPS = 1e-6
LOG2E = math.log2(math.e)

LANES = 128
SUBLANES = 8
VMEM_LIMIT = 56 * 1024 * 1024

QKV_COLS = ATTN_WIDTH + 4 * KV_COLS
Q_BLOCK = 2 * CHUNK
K_BLOCK = (WINDOW_CHUNKS + 2) * CHUNK
ATTN_TILE = 512
LRU_TILE = 512
TOKEN_TILE = 512
ROUTER_SLABS = 2
MXU_TILE = 256
MOE_TILE = 512
MOE_F_TILE = 7 * MXU_TILE
DISPATCH_TILE = 1024
COMBINE_TILE = 256
DMA_UNROLL = 8


def _params(sem, **kw):
    return pltpu.CompilerParams(dimension_semantics=sem, vmem_limit_bytes=VMEM_LIMIT, **kw)


def _resident(shape):
    nd = len(shape)
    return pl.BlockSpec(shape, lambda *_: (0,) * nd, pipeline_mode=pl.Buffered(1))


def _layer_norm(x, g, b):
    mu = jnp.mean(x, axis=-1, keepdims=True)
    xc = x - mu
    var = jnp.mean(xc * xc, axis=-1, keepdims=True)
    return xc * lax.rsqrt(var + LN_EPS) * g + b


def _rms_norm(x, g):
    return x * lax.rsqrt(jnp.mean(x * x, axis=-1, keepdims=True) + RMS_EPS) * g


HALF = D_MODEL // 2
ROW_TILE = HALF // LANES
U32 = jnp.uint32
HIGH_HALF = 0xFFFF0000


def _load_row_tiles(ref, rows, lead=()):
    words = jnp.concatenate(
        [ref[lead + (pl.ds(c, rows, stride=ROW_TILE), slice(None))] for c in range(ROW_TILE)], axis=1)
    low = lax.bitcast_convert_type(lax.shift_left(words, U32(16)), F32)
    high = lax.bitcast_convert_type(jnp.bitwise_and(words, U32(HIGH_HALF)), F32)
    return jnp.concatenate([low, high], axis=1)


def _store_row_tiles(ref, value, first_row=0):
    bits = lax.bitcast_convert_type(value.astype(BF16).astype(F32), U32)
    words = jnp.bitwise_or(lax.shift_right_logical(bits[:, :HALF], U32(16)),
                           jnp.bitwise_and(bits[:, HALF:], U32(HIGH_HALF)))
    for c in range(ROW_TILE):
        ref[pl.ds(first_row * ROW_TILE + c, value.shape[0], stride=ROW_TILE), :] = (
            words[:, c * LANES:(c + 1) * LANES])


BF16_ROWS = 2 * SUBLANES


def _with_casts(kernel_fn, n_in, n_out, n_casts):
    def wrapped(*refs):
        ins, refs = refs[:n_in], refs[n_in:]
        srcs, refs = refs[:n_casts], refs[n_casts:]
        outs, refs = refs[:n_out], refs[n_out:]
        dsts, scratch = refs[:n_casts], refs[n_casts:]
        for src, dst in zip(srcs, dsts):
            dst[...] = src[...].astype(BF16)
        kernel_fn(*ins, *outs, *scratch)
    return wrapped


def _cast_specs(weights, linear_step, steps):
    specs, shapes = [], []
    for w in weights:
        rows, cols = w.shape
        per = next(p for p in range(BF16_ROWS, rows + 1, BF16_ROWS) if rows % p == 0 and p * steps >= rows)
        last = rows // per - 1
        specs.append(pl.BlockSpec((per, cols),
                                  lambda *g, last=last: (jnp.minimum(linear_step(*g), last), 0)))
        shapes.append(jax.ShapeDtypeStruct(w.shape, BF16))
    return specs, shapes


ATTN_HIST = Q_BLOCK
K_COL = ATTN_WIDTH


def _attn_fill(kk_ref, vv_ref, qkv_ref):
    lo = lax.broadcasted_iota(jnp.int32, (1, LANES), 1) < HEAD_DIM
    zero = jnp.zeros((), BF16)
    rows = slice(ATTN_HIST, ATTN_HIST + ATTN_TILE)
    for dst, col in ((kk_ref, K_COL), (vv_ref, K_COL + 2 * LANES)):
        for j in range(4):
            dst[j, :ATTN_HIST, :] = dst[j, ATTN_TILE:, :]
        a = qkv_ref[:, col:col + LANES]
        s = qkv_ref[:, col + LANES:col + 2 * LANES]
        dst[0, rows, :] = jnp.where(lo, a, zero)
        dst[1, rows, :] = jnp.where(lo, zero, s)
        dst[2, rows, :] = jnp.where(lo, s, zero)
        dst[3, rows, :] = jnp.where(lo, zero, a)


def _attn_unit(sub, h, seq_start, qkv_ref, kk_ref, vv_ref, bias_ref, sink_ref, acc_ref):
    q0 = sub * Q_BLOCK
    lo = lax.broadcasted_iota(jnp.int32, (1, LANES), 1) < HEAD_DIM
    dn = (((1,), (1,)), ((), ()))
    start_mask = None
    if sub == 0:
        col = lax.broadcasted_iota(jnp.int32, (1, K_BLOCK), 1)
        start_mask = jnp.where(jnp.logical_and(seq_start, col < ATTN_HIST), NEG_INF, 0.0).astype(F32)
    q2 = jnp.concatenate(
        [qkv_ref[pl.ds(q0, Q_BLOCK), pl.ds((2 * h + p) * LANES, LANES)] for p in range(2)], axis=0)
    probs = {}
    inv_den = {}
    for half in range(2):
        kk = kk_ref[2 * h + half, pl.ds(q0, K_BLOCK), :]
        sc = lax.dot_general(q2, kk, dn, preferred_element_type=F32)
        for p in range(2):
            head = 4 * h + 2 * p + half
            logit = sc[p * Q_BLOCK:(p + 1) * Q_BLOCK] + bias_ref[head]
            if start_mask is not None:
                logit = logit + start_mask
            sink = sink_ref[head]
            m = jnp.maximum(jnp.max(logit, axis=-1, keepdims=True), sink)
            e = jnp.exp2(logit - m)
            den = jnp.sum(e, axis=-1, keepdims=True) + jnp.exp2(sink - m)
            probs[(p, half)] = e.astype(BF16)
            inv_den[(p, half)] = 1.0 / den
    lhs = jnp.concatenate(
        [jnp.concatenate([probs[(p, 0)], probs[(p, 1)]], axis=1) for p in range(2)], axis=0)
    rhs = jnp.concatenate([vv_ref[2 * h, pl.ds(q0, K_BLOCK), :],
                           vv_ref[2 * h + 1, pl.ds(q0, K_BLOCK), :]], axis=0)
    out = jnp.dot(lhs, rhs, preferred_element_type=F32)
    for p in range(2):
        scale = jnp.where(lo, inv_den[(p, 0)], inv_den[(p, 1)])
        acc_ref[pl.ds(q0, Q_BLOCK), pl.ds((2 * h + p) * LANES, LANES)] = (
            out[p * Q_BLOCK:(p + 1) * Q_BLOCK] * scale)


def _lru_slab(c, x, gate, prev, h_in, cw_ref, cb_ref, wg_ref, bg_ref, lam_ref, a_ref, u_ref, ext_ref):
    t = x.shape[0]
    lanes = slice(c * LANES, (c + 1) * LANES)
    i_lanes = slice(LRU_WIDTH + c * LANES, LRU_WIDTH + (c + 1) * LANES)

    ext_ref[c, :SUBLANES, :] = prev
    ext_ref[c, SUBLANES:, :] = x
    xc = x * cw_ref[CONV_WIDTH - 1:CONV_WIDTH, lanes] + cb_ref[:, lanes]
    for k in range(1, CONV_WIDTH):
        xc = xc + (ext_ref[c, SUBLANES - k:SUBLANES - k + t, :]
                   * cw_ref[CONV_WIDTH - 1 - k:CONV_WIDTH - k, lanes])

    xcb = xc.astype(BF16)
    r = jax.nn.sigmoid(jnp.dot(xcb, wg_ref[lanes, lanes], preferred_element_type=F32) + bg_ref[:, lanes])
    i = jax.nn.sigmoid(jnp.dot(xcb, wg_ref[lanes, i_lanes], preferred_element_type=F32)
                       + bg_ref[:, i_lanes])
    neg_lam = -lam_ref[:, lanes]
    softplus = jnp.maximum(neg_lam, 0.0) + jnp.log1p(jnp.exp(-jnp.abs(neg_lam)))
    log_a = (-LRU_C) * r * softplus
    a = jnp.exp(log_a)
    h = jnp.sqrt(jnp.tanh(-log_a) * (1.0 + a * a)) * (i * xc)

    def doubling(av, hv, index, length, axis):
        d = 1
        while d < length:
            keep = index >= d
            a_sh = jnp.where(keep, pltpu.roll(av, d, axis), 1.0)
            h_sh = jnp.where(keep, pltpu.roll(hv, d, axis), 0.0)
            hv = av * h_sh + hv
            av = av * a_sh
            d *= 2
        return av, hv

    groups = t // SUBLANES
    grouped = (groups, SUBLANES, LANES)
    sub = lax.broadcasted_iota(jnp.int32, (1, SUBLANES, 1), 1)
    a, h = doubling(a.reshape(grouped), h.reshape(grouped), sub, SUBLANES, 1)
    a = a.reshape(t, LANES)
    h = h.reshape(t, LANES)
    grow = lax.broadcasted_iota(jnp.int32, (groups, 1), 0)

    def group_last(ref, value):
        ref[c] = value
        return ref[c, pl.ds(SUBLANES - 1, groups, stride=SUBLANES), :]

    ag, hg = doubling(group_last(a_ref, a), group_last(u_ref, h), grow, groups, 0)
    state = hg + ag * h_in
    enter = jnp.where(grow >= 1, pltpu.roll(state, 1, 0), h_in)
    enter = jnp.broadcast_to(enter[:, None, :], grouped).reshape(t, LANES)
    h = h + a * enter
    return jax.nn.gelu(gate) * h, state[groups - 1:groups, :]


def _mixer_kernel(sink_ref, x_ref, w_ref, cw_ref, cb_ref, wg_ref, bg_ref, lam_ref, gl_ref, bias_ref, ga_ref,
                  ya_ref, yl_ref,
                  q0_ref, q1_ref, z0_ref, z1_ref, kk_ref, vv_ref, acc_ref,
                  tail_ref, h_ref, a_ref, u_ref, ext_ref, y_ref, *, tiles_per_seq):
    s = pl.program_id(0)

    @pl.when(s == 0)
    def _():
        for ref in (q1_ref, z1_ref, kk_ref, vv_ref, tail_ref, h_ref):
            ref[...] = jnp.zeros_like(ref)

    def step(q_new_ref, z_new_ref, q_old_ref, z_old_ref):
        xb = x_ref[...].astype(BF16)
        seq_start = lax.rem(s - 1, tiles_per_seq) == 0

        def inproj_chunk(j):
            cols = slice(j * MXU_TILE, (j + 1) * MXU_TILE)
            z = jnp.dot(xb, w_ref[:, cols], preferred_element_type=F32)
            if cols.stop <= QKV_COLS:
                q_new_ref[:, cols] = z.astype(BF16)
            else:
                z_new_ref[:, cols.start - QKV_COLS:cols.stop - QKV_COLS] = z

        _attn_fill(kk_ref, vv_ref, q_old_ref)
        attn_units = [(sub, h) for sub in range(ATTN_TILE // Q_BLOCK) for h in range(N_KV_HEADS)]
        slabs = LRU_WIDTH // LANES
        chunks_per_slab = (QKV_COLS + 2 * LRU_WIDTH) // MXU_TILE // slabs
        units_per_slab = len(attn_units) // slabs
        sq = None
        for c in range(slabs):
            lanes = slice(c * LANES, (c + 1) * LANES)
            inproj_chunk(chunks_per_slab * c)
            x = z_old_ref[:, lanes]
            prev = jnp.where(seq_start, 0.0, tail_ref[:, lanes])
            h_in = jnp.where(seq_start, 0.0, h_ref[:, lanes])
            y, h_out = _lru_slab(c, x, z_old_ref[:, LRU_WIDTH + c * LANES:LRU_WIDTH + (c + 1) * LANES],
                                 prev, h_in, cw_ref, cb_ref, wg_ref, bg_ref, lam_ref, a_ref, u_ref, ext_ref)
            y_ref[:, lanes] = y
            part = jnp.sum(y * y, axis=-1, keepdims=True)
            sq = part if sq is None else sq + part
            h_ref[:, lanes] = h_out
            tail_ref[:, lanes] = x[LRU_TILE - SUBLANES:, :]
            for k in range(max(chunks_per_slab - 1, units_per_slab)):
                if k < units_per_slab:
                    sub, h = attn_units[units_per_slab * c + k]
                    _attn_unit(sub, h, seq_start, q_old_ref, kk_ref, vv_ref, bias_ref, sink_ref, acc_ref)
                if k < chunks_per_slab - 1:
                    inproj_chunk(chunks_per_slab * c + 1 + k)
        scale = lax.rsqrt(sq * (1.0 / LRU_WIDTH) + RMS_EPS)
        yl_ref[...] = (y_ref[...] * scale * gl_ref[...]).astype(BF16)
        ya_ref[...] = _rms_norm(acc_ref[...], ga_ref[...]).astype(BF16)

    @pl.when(s % 2 == 0)
    def _():
        step(q0_ref, z0_ref, q1_ref, z1_ref)

    @pl.when(s % 2 == 1)
    def _():
        step(q1_ref, z1_ref, q0_ref, z0_ref)


def _mixer(x2d, w, sinks, bias2, g_attn, conv_w, conv_b, w_gates, b_gates, lam, g_lru, seq, casts=()):
    n = x2d.shape[0]
    assert ATTN_TILE == LRU_TILE
    tiles = n // LRU_TILE
    last = tiles - 1
    cast_specs, cast_shapes = _cast_specs(casts, lambda s: s, tiles + 1)
    out_tile = lambda width: pl.BlockSpec((LRU_TILE, width), lambda s: (jnp.maximum(s - 1, 0), 0))
    return pl.pallas_call(
        _with_casts(functools.partial(_mixer_kernel, tiles_per_seq=seq // LRU_TILE), 11, 2, len(casts)),
        grid=(tiles + 1,),
        in_specs=[pl.BlockSpec(memory_space=pltpu.SMEM),
                  pl.BlockSpec((LRU_TILE, D_MODEL), lambda s: (jnp.minimum(s, last), 0)),
                  _resident((D_MODEL, QKV_COLS + 2 * LRU_WIDTH)),
                  _resident((CONV_WIDTH, LRU_WIDTH)),
                  _resident((1, LRU_WIDTH)),
                  _resident((LRU_WIDTH, 2 * LRU_WIDTH)),
                  _resident((1, 2 * LRU_WIDTH)),
                  _resident((1, LRU_WIDTH)),
                  _resident((1, LRU_WIDTH)),
                  _resident((N_Q_HEADS, Q_BLOCK, K_BLOCK)),
                  _resident((1, ATTN_WIDTH))] + cast_specs,
        out_specs=[out_tile(ATTN_WIDTH), out_tile(LRU_WIDTH)] + cast_specs,
        out_shape=[jax.ShapeDtypeStruct((n, ATTN_WIDTH), BF16),
                   jax.ShapeDtypeStruct((n, LRU_WIDTH), BF16)] + cast_shapes,
        scratch_shapes=[pltpu.VMEM((ATTN_TILE, QKV_COLS), BF16),
                        pltpu.VMEM((ATTN_TILE, QKV_COLS), BF16),
                        pltpu.VMEM((LRU_TILE, 2 * LRU_WIDTH), F32),
                        pltpu.VMEM((LRU_TILE, 2 * LRU_WIDTH), F32),
                        pltpu.VMEM((4, ATTN_HIST + ATTN_TILE, LANES), BF16),
                        pltpu.VMEM((4, ATTN_HIST + ATTN_TILE, LANES), BF16),
                        pltpu.VMEM((ATTN_TILE, ATTN_WIDTH), F32),
                        pltpu.VMEM((SUBLANES, LRU_WIDTH), F32),
                        pltpu.VMEM((1, LRU_WIDTH), F32),
                        pltpu.VMEM((LRU_WIDTH // LANES, LRU_TILE, LANES), F32),
                        pltpu.VMEM((LRU_WIDTH // LANES, LRU_TILE, LANES), F32),
                        pltpu.VMEM((LRU_WIDTH // LANES, SUBLANES + LRU_TILE, LANES), F32),
                        pltpu.VMEM((LRU_TILE, LRU_WIDTH), F32)],
        compiler_params=_params(("arbitrary",)),
        name="mixer",
    )(sinks, x2d, w, conv_w, conv_b, w_gates, b_gates, lam, g_lru, bias2, g_attn, *casts)


def _mix_ln1(ya_ref, yl_ref, x_ref, wo_ref, g1_ref, b1_ref, alpha, rows=slice(None)):
    y = jnp.dot(ya_ref[rows, :], wo_ref[:ATTN_WIDTH, :], preferred_element_type=F32)
    y = y + jnp.dot(yl_ref[rows, :], wo_ref[ATTN_WIDTH:, :], preferred_element_type=F32)
    return _layer_norm(alpha * x_ref[rows, :] + y, g1_ref[...], b1_ref[...])


def _dense_block_kernel(ya_ref, yl_ref, x_ref, wo_ref, g1_ref, b1_ref, wg_ref, wu_ref, wd_ref,
                        g2_ref, b2_ref, o_ref, *, alpha, f_splits):
    x1 = _mix_ln1(ya_ref, yl_ref, x_ref, wo_ref, g1_ref, b1_ref, alpha)
    xb = x1.astype(BF16)
    f = None
    for lo, hi in zip(f_splits[:-1], f_splits[1:]):
        cols = slice(lo, hi)
        g = jnp.dot(xb, wg_ref[:, cols], preferred_element_type=F32)
        u = jnp.dot(xb, wu_ref[:, cols], preferred_element_type=F32)
        hid = (jax.nn.silu(g) * u).astype(BF16)
        part = jnp.dot(hid, wd_ref[cols, :], preferred_element_type=F32)
        f = part if f is None else f + part
    o_ref[...] = _layer_norm(alpha * x1 + f, g2_ref[...], b2_ref[...])


def _dense_block(ya, yl, x2d, wo, g1, b1, wg, wu, wd, g2, b2, alpha):
    n = x2d.shape[0]
    d_ff = wg.shape[1]
    tile = lambda w: pl.BlockSpec((TOKEN_TILE, w), lambda i: (i, 0))
    mxu_tiles = pl.cdiv(d_ff, MXU_TILE)
    f_splits = (0, (mxu_tiles + 1) // 2 * MXU_TILE, d_ff)
    return pl.pallas_call(
        functools.partial(_dense_block_kernel, alpha=alpha, f_splits=f_splits),
        grid=(n // TOKEN_TILE,),
        in_specs=[tile(ATTN_WIDTH), tile(LRU_WIDTH), tile(D_MODEL),
                  _resident((D_MODEL, D_MODEL)), _resident((1, D_MODEL)), _resident((1, D_MODEL)),
                  _resident((D_MODEL, d_ff)), _resident((D_MODEL, d_ff)), _resident((d_ff, D_MODEL)),
                  _resident((1, D_MODEL)), _resident((1, D_MODEL))],
        out_specs=tile(D_MODEL),
        out_shape=jax.ShapeDtypeStruct((n, D_MODEL), F32),
        compiler_params=_params(("parallel",)),
        name="dense_block",
    )(ya, yl, x2d, wo, g1, b1, wg, wu, wd, g2, b2)


_L_IDX, _L_W, _L_RANK = 0, 2, 4


def _router_kernel(ya_ref, yl_ref, x_ref, wo_ref, g1_ref, b1_ref, rw_ref,
                   x1_ref, x1p_ref, route_ref, meta_ref, count_ref, carry_ref, *, alpha):
    @pl.when(pl.program_id(0) == 0)
    def _():
        carry_ref[...] = jnp.zeros_like(carry_ref)

    t = x_ref.shape[0] // ROUTER_SLABS
    carry = carry_ref[:, :1]
    for s in range(ROUTER_SLABS):
        carry = _route_slab(s * t, t, carry, ya_ref, yl_ref, x_ref, wo_ref, g1_ref, b1_ref, rw_ref,
                            x1_ref, x1p_ref, route_ref, meta_ref, alpha)
    carry_ref[...] = jnp.broadcast_to(carry, carry_ref.shape)
    count_ref[...] = jnp.broadcast_to(carry, count_ref.shape)


def _route_slab(first, t, carry, ya_ref, yl_ref, x_ref, wo_ref, g1_ref, b1_ref, rw_ref,
                x1_ref, x1p_ref, route_ref, meta_ref, alpha):
    x1 = _mix_ln1(ya_ref, yl_ref, x_ref, wo_ref, g1_ref, b1_ref, alpha, pl.ds(first, t))
    x1_ref[pl.ds(first, t), :] = x1
    _store_row_tiles(x1p_ref, x1, first)
    x_hi = x1.astype(BF16)
    x_lo = (x1 - x_hi.astype(F32)).astype(BF16)
    both = jnp.dot(x_hi, rw_ref[...], preferred_element_type=F32)
    logits = both[:, :LANES] + both[:, LANES:] + jnp.dot(x_lo, rw_ref[:, :LANES], preferred_element_type=F32)

    lt = logits.T[:N_EXPERTS]
    row = lax.broadcasted_iota(jnp.int32, (N_EXPERTS, t), 0).astype(F32)
    lowest = jnp.finfo(F32).min
    m1 = jnp.max(lt, axis=0, keepdims=True)
    i1 = jnp.min(jnp.where(lt == m1, row, float(N_EXPERTS)), axis=0, keepdims=True)
    l2 = jnp.where(row == i1, lowest, lt)
    m2 = jnp.max(l2, axis=0, keepdims=True)
    i2 = jnp.min(jnp.where(l2 == m2, row, float(N_EXPERTS)), axis=0, keepdims=True)
    e = jnp.exp(m2 - m1)
    w1 = 1.0 / (1.0 + e)
    w2 = e / (1.0 + e)

    sel = jnp.where(jnp.logical_or(row == i1, row == i2), 1.0, 0.0)
    r_i = lax.broadcasted_iota(jnp.int32, (t, t), 0)
    c_i = lax.broadcasted_iota(jnp.int32, (t, t), 1)
    earlier = jnp.where(r_i < c_i, 1.0, 0.0).astype(BF16)
    before = jnp.dot(sel.astype(BF16), earlier, preferred_element_type=F32) + carry
    rank1 = jnp.sum(jnp.where(row == i1, before, 0.0), axis=0, keepdims=True)
    rank2 = jnp.sum(jnp.where(row == i2, before, 0.0), axis=0, keepdims=True)

    meta = jnp.where(row == _L_IDX, i1, 0.0)
    meta = jnp.where(row == _L_IDX + 1, i2, meta)
    meta = jnp.where(row == _L_W, w1, meta)
    meta = jnp.where(row == _L_W + 1, w2, meta)
    meta = jnp.where(row == _L_RANK, rank1, meta)
    meta = jnp.where(row == _L_RANK + 1, rank2, meta)
    meta_ref[:, pl.ds(first, t)] = meta
    padded = jnp.concatenate([meta, jnp.zeros((LANES - N_EXPERTS, t), F32)], axis=0)
    route_ref[pl.ds(first, t), :] = padded.T
    return carry + jnp.sum(sel, axis=1, keepdims=True)


def _router_block(ya, yl, x2d, wo, g1, b1, router_w, alpha):
    n = x2d.shape[0]
    assert router_w.shape[1] == N_EXPERTS == SUBLANES
    rw = jnp.pad(router_w, ((0, 0), (0, LANES - N_EXPERTS)))
    w_hi = rw.astype(BF16)
    w_lo = (rw - w_hi.astype(F32)).astype(BF16)
    tile = lambda w: pl.BlockSpec((TOKEN_TILE, w), lambda i: (i, 0))
    return pl.pallas_call(
        functools.partial(_router_kernel, alpha=alpha),
        grid=(n // TOKEN_TILE,),
        in_specs=[tile(ATTN_WIDTH), tile(LRU_WIDTH), tile(D_MODEL),
                  _resident((D_MODEL, D_MODEL)), _resident((1, D_MODEL)), _resident((1, D_MODEL)),
                  _resident((D_MODEL, 2 * LANES))],
        out_specs=[tile(D_MODEL), pl.BlockSpec((TOKEN_TILE * ROW_TILE, LANES), lambda i: (i, 0)), tile(LANES),
                   pl.BlockSpec((SUBLANES, TOKEN_TILE), lambda i: (0, i)),
                   pl.BlockSpec((N_EXPERTS, LANES), lambda i: (0, 0))],
        out_shape=[jax.ShapeDtypeStruct((n, D_MODEL), F32),
                   jax.ShapeDtypeStruct((n * ROW_TILE, LANES), U32),
                   jax.ShapeDtypeStruct((n, LANES), F32),
                   jax.ShapeDtypeStruct((SUBLANES, n), F32),
                   jax.ShapeDtypeStruct((N_EXPERTS, LANES), F32)],
        scratch_shapes=[pltpu.VMEM((N_EXPERTS, LANES), F32)],
        compiler_params=_params(("arbitrary",)),
        name="router_block",
    )(ya, yl, x2d, wo, g1, b1, jnp.concatenate([w_hi, w_lo], axis=1))


def _tile_copy(src_ref, src_row, dst_ref, dst_row, sem):
    src = src_ref.at[pl.ds(pl.multiple_of(src_row * ROW_TILE, ROW_TILE), ROW_TILE)]
    dst = dst_ref.at[pl.ds(pl.multiple_of(dst_row * ROW_TILE, ROW_TILE), ROW_TILE)]
    return pltpu.make_async_copy(src, dst, sem)


def _dispatch_kernel(pad_start_ref, pad_len_ref, pos0_ref, pos1_ref, x_ref, xs_ref, sem, pad_sem):
    def issue(t, carry):
        _tile_copy(x_ref, t, xs_ref, pos0_ref[t], sem).start(priority=0)
        _tile_copy(x_ref, t, xs_ref, pos1_ref[t], sem).start(priority=1)
        return carry

    lax.fori_loop(0, DISPATCH_TILE, issue, 0, unroll=DMA_UNROLL)
    for _ in range(2):
        pltpu.make_async_copy(x_ref, xs_ref.at[pl.ds(0, DISPATCH_TILE * ROW_TILE)], sem).wait()

    @pl.when(pl.program_id(0) == pl.num_programs(0) - 1)
    def _():
        for e in range(N_EXPERTS):
            def fill(r, carry):
                _tile_copy(x_ref, 0, xs_ref, pad_start_ref[e] + r, pad_sem).start()
                return carry

            def drain(r, carry):
                _tile_copy(x_ref, 0, xs_ref, 0, pad_sem).wait()
                return carry

            lax.fori_loop(0, pad_len_ref[e], fill, 0)
            lax.fori_loop(0, pad_len_ref[e], drain, 0)

        def idle_copy(r):
            first = pl.multiple_of((pad_start_ref[N_EXPERTS] + r * MOE_TILE) * ROW_TILE, MOE_TILE * ROW_TILE)
            return pltpu.make_async_copy(x_ref.at[pl.ds(0, MOE_TILE * ROW_TILE)],
                                         xs_ref.at[pl.ds(first, MOE_TILE * ROW_TILE)], pad_sem)

        def fill_idle(r, carry):
            idle_copy(r).start()
            return carry

        def drain_idle(r, carry):
            idle_copy(r).wait()
            return carry

        lax.fori_loop(0, pad_len_ref[N_EXPERTS], fill_idle, 0)
        lax.fori_loop(0, pad_len_ref[N_EXPERTS], drain_idle, 0)


def _dispatch(pad_start, pad_len, pos0, pos1, x1, sorted_rows):
    n = x1.shape[0] // ROW_TILE
    assert DISPATCH_TILE >= MOE_TILE
    smem_tile = pl.BlockSpec((DISPATCH_TILE,), lambda i, *_: (i,), memory_space=pltpu.SMEM)
    grid_spec = pltpu.PrefetchScalarGridSpec(
        num_scalar_prefetch=2,
        grid=(n // DISPATCH_TILE,),
        in_specs=[smem_tile, smem_tile,
                  pl.BlockSpec((DISPATCH_TILE * ROW_TILE, LANES), lambda i, *_: (i, 0))],
        out_specs=pl.BlockSpec(memory_space=pl.ANY),
        scratch_shapes=[pltpu.SemaphoreType.DMA(()), pltpu.SemaphoreType.DMA(())],
    )
    return pl.pallas_call(
        _dispatch_kernel,
        grid_spec=grid_spec,
        out_shape=jax.ShapeDtypeStruct((sorted_rows * ROW_TILE, LANES), U32),
        compiler_params=_params(("arbitrary",), disable_bounds_checks=True),
        name="moe_dispatch",
    )(pad_start, pad_len, pos0, pos1, x1)


def _experts_kernel(tile_expert_ref, n_used_ref, x_ref, wg_ref, wu_ref, wd_ref, y_ref, xb_ref, acc_ref):
    del tile_expert_ref
    i = pl.program_id(0)
    j = pl.program_id(1)
    last = pl.num_programs(1) - 1

    @pl.when(i < n_used_ref[0])
    def _():
        @pl.when(j == 0)
        def _():
            xb_ref[...] = _load_row_tiles(x_ref, MOE_TILE).astype(BF16)

        xb = xb_ref[...]
        g = jnp.dot(xb, wg_ref[...], preferred_element_type=F32)
        u = jnp.dot(xb, wu_ref[...], preferred_element_type=F32)
        hid = (jax.nn.silu(g) * u).astype(BF16)
        part = jnp.dot(hid, wd_ref[...], preferred_element_type=F32)

        @pl.when(j == 0)
        def _():
            acc_ref[...] = part

        @pl.when(jnp.logical_and(j > 0, j < last))
        def _():
            acc_ref[...] += part

        @pl.when(j == last)
        def _():
            _store_row_tiles(y_ref, acc_ref[...] + part)

    @pl.when(jnp.logical_and(i >= n_used_ref[0], j == 0))
    def _():
        y_ref[...] = jnp.zeros_like(y_ref)


def _experts(tile_expert, n_used, xs, wg, wu, wd):
    rows = xs.shape[0] // ROW_TILE
    d_ff = wg.shape[2]
    n_tiles = rows // MOE_TILE
    n_f = d_ff // MOE_F_TILE
    assert n_f >= 2

    def row_map(i, j, te, nu):
        return (jnp.minimum(i, nu[0] - 1), 0)

    def f_of(i, j, nu):
        return jnp.where(i < nu[0], j, n_f - 1)

    grid_spec = pltpu.PrefetchScalarGridSpec(
        num_scalar_prefetch=2,
        grid=(n_tiles, n_f),
        in_specs=[pl.BlockSpec((MOE_TILE * ROW_TILE, LANES), row_map),
                  pl.BlockSpec((None, D_MODEL, MOE_F_TILE), lambda i, j, te, nu: (te[i], 0, f_of(i, j, nu))),
                  pl.BlockSpec((None, D_MODEL, MOE_F_TILE), lambda i, j, te, nu: (te[i], 0, f_of(i, j, nu))),
                  pl.BlockSpec((None, MOE_F_TILE, D_MODEL), lambda i, j, te, nu: (te[i], f_of(i, j, nu), 0))],
        out_specs=pl.BlockSpec((MOE_TILE * ROW_TILE, LANES), lambda i, j, te, nu: (i, 0)),
        scratch_shapes=[pltpu.VMEM((MOE_TILE, D_MODEL), BF16), pltpu.VMEM((MOE_TILE, D_MODEL), F32)],
    )
    return pl.pallas_call(
        _experts_kernel,
        grid_spec=grid_spec,
        out_shape=jax.ShapeDtypeStruct((rows * ROW_TILE, LANES), U32),
        compiler_params=_params(("arbitrary", "arbitrary")),
        name="moe_experts",
    )(tile_expert, n_used, xs, wg, wu, wd)


def _combine_kernel(pos0_ref, pos1_ref, nxt0_ref, nxt1_ref, ys_ref, x1_ref, route_ref, g2_ref, b2_ref,
                    o_ref, buf_ref, sem, *, alpha):
    i = pl.program_id(0)
    slot = i % 2

    def issue(p0_ref, p1_ref, dst_slot):
        def body(t, carry):
            for k, p_ref in enumerate((p0_ref, p1_ref)):
                _tile_copy(ys_ref, p_ref[t], buf_ref.at[dst_slot, k], t, sem.at[dst_slot]).start(priority=k)
            return carry
        lax.fori_loop(0, COMBINE_TILE, body, 0, unroll=DMA_UNROLL)

    @pl.when(i == 0)
    def _():
        issue(pos0_ref, pos1_ref, 0)

    @pl.when(i + 1 < pl.num_programs(0))
    def _():
        issue(nxt0_ref, nxt1_ref, 1 - slot)

    for k in range(2):
        pltpu.make_async_copy(ys_ref.at[pl.ds(0, COMBINE_TILE * ROW_TILE)], buf_ref.at[slot, k],
                              sem.at[slot]).wait()
    route = route_ref[...]
    y0 = _load_row_tiles(buf_ref, COMBINE_TILE, (slot, 0))
    y1 = _load_row_tiles(buf_ref, COMBINE_TILE, (slot, 1))
    f = route[:, _L_W:_L_W + 1] * y0 + route[:, _L_W + 1:_L_W + 2] * y1
    o_ref[...] = _layer_norm(alpha * x1_ref[...] + f, g2_ref[...], b2_ref[...])


def _combine(pos0, pos1, ys, x1, route, g2, b2, alpha):
    n = x1.shape[0]
    steps = n // COMBINE_TILE
    tile = lambda w: pl.BlockSpec((COMBINE_TILE, w), lambda i: (i, 0))
    smem_cur = pl.BlockSpec((COMBINE_TILE,), lambda i: (i,), memory_space=pltpu.SMEM)
    smem_next = pl.BlockSpec((COMBINE_TILE,), lambda i: (jnp.minimum(i + 1, steps - 1),),
                             memory_space=pltpu.SMEM)
    return pl.pallas_call(
        functools.partial(_combine_kernel, alpha=alpha),
        grid=(steps,),
        in_specs=[smem_cur, smem_cur, smem_next, smem_next,
                  pl.BlockSpec(memory_space=pl.ANY),
                  tile(D_MODEL), tile(LANES), _resident((1, D_MODEL)), _resident((1, D_MODEL))],
        out_specs=tile(D_MODEL),
        out_shape=jax.ShapeDtypeStruct((n, D_MODEL), F32),
        scratch_shapes=[pltpu.VMEM((2, 2, COMBINE_TILE * ROW_TILE, LANES), U32),
                        pltpu.SemaphoreType.DMA((2,))],
        compiler_params=_params(("arbitrary",), disable_bounds_checks=True),
        name="moe_combine",
    )(pos0, pos1, pos0, pos1, ys, x1, route, g2, b2)


def _moe_block(x1, x1_rows, route, meta, counts, wg, wu, wd, g2, b2, alpha):
    n = route.shape[0]
    n_tiles = (2 * n) // MOE_TILE + N_EXPERTS
    count = counts[:, 0].astype(jnp.int32)
    tiles_per = (count + MOE_TILE - 1) // MOE_TILE
    tile_end = jnp.cumsum(tiles_per)
    group_start = (tile_end - tiles_per) * MOE_TILE
    n_used = tile_end[-1:]
    tile_id = jnp.minimum(jnp.arange(n_tiles, dtype=jnp.int32), n_used[0] - 1)
    tile_expert = jnp.sum((tile_id[:, None] >= tile_end[None, :]).astype(jnp.int32), axis=1)
    idx = meta[_L_IDX:_L_IDX + 2].astype(jnp.int32)
    rank = meta[_L_RANK:_L_RANK + 2].astype(jnp.int32)
    start = jnp.sum(jnp.where(idx[:, :, None] == jnp.arange(N_EXPERTS), group_start, 0), axis=-1)
    pos = start + rank

    pad_start = jnp.concatenate([group_start + count, n_used * MOE_TILE])
    pad_len = jnp.concatenate([tiles_per * MOE_TILE - count, n_tiles - n_used])
    xs = _dispatch(pad_start, pad_len, pos[0], pos[1], x1_rows, n_tiles * MOE_TILE)
    ys = _experts(tile_expert, n_used, xs, wg, wu, wd)
    return _combine(pos[0], pos[1], ys, x1, route, g2, b2, alpha)


def _t5_bucket(rel):
    half = N_BUCKETS // 2
    max_exact = half // 2
    ret = jnp.where(rel > 0, half, 0)
    n = jnp.abs(rel)
    large = max_exact + (jnp.log(jnp.maximum(n, 1).astype(F32) / max_exact)
                         / math.log(MAX_DISTANCE / max_exact) * (half - max_exact)).astype(jnp.int32)
    large = jnp.minimum(large, half - 1)
    return ret + jnp.where(n < max_exact, n, large)


def _block_bias(rel_bias):
    qi = jnp.arange(Q_BLOCK)[:, None]
    kj = jnp.arange(K_BLOCK)[None, :]
    rel = (kj - WINDOW_CHUNKS * CHUNK) - qi
    hit = _t5_bucket(rel)[None, :, :, None] == jnp.arange(N_BUCKETS)
    bias = jnp.sum(jnp.where(hit, rel_bias.T[:, None, None, :], 0.0), axis=-1)
    first = (kj // CHUNK) - (qi // CHUNK)
    visible = jnp.logical_and(first >= 0, first <= WINDOW_CHUNKS)
    return jnp.where(visible[None], bias, NEG_INF).astype(F32) * LOG2E


def _in_weights(w_in):
    q = w_in[:, :ATTN_WIDTH] * (HEAD_DIM ** -0.5 * LOG2E)
    k = w_in[:, ATTN_WIDTH:ATTN_WIDTH + KV_COLS]
    v = w_in[:, ATTN_WIDTH + KV_COLS:ATTN_WIDTH + 2 * KV_COLS]
    rest = w_in[:, ATTN_WIDTH + 2 * KV_COLS:]
    swap = lambda w: jnp.concatenate([w[:, HEAD_DIM:], w[:, :HEAD_DIM]], axis=1)
    return jnp.concatenate([q, k, swap(k), v, swap(v), rest], axis=1).astype(BF16)


def _block_diag(w):
    nb, d, e = w.shape
    eye = jnp.eye(nb, dtype=w.dtype)
    return (eye[:, None, :, None] * w[:, :, None, :]).reshape(nb * d, nb * e)


def kernel(x, rel_bias, w_in, attn_sinks, conv_w, conv_b, gate_r_w, gate_r_b, gate_i_w, gate_i_b,
           lru_lambda, norm_attn_g, norm_lru_g, w_out, ln1_g, ln1_b, ffn_w_gate, ffn_w_up, ffn_w_down,
           router_w, exp_w_gate, exp_w_up, exp_w_down, ln2_g, ln2_b):
    batch, seq, d = x.shape
    depth = w_in.shape[0]
    alpha = float((2 * depth) ** 0.25)
    bias2 = _block_bias(rel_bias)
    row = lambda v: v.reshape(1, -1)

    rides = {l: [] for l in range(depth)}
    flat = lambda w: w.reshape(-1, w.shape[-1])
    for l in range(depth):
        if l % 2 == 0:
            rides[l] += [ffn_w_gate[l // 2], ffn_w_up[l // 2], ffn_w_down[l // 2]]
        else:
            rides[l - 1] += [flat(exp_w_gate[l // 2]), flat(exp_w_up[l // 2])]
            rides[l] += [flat(exp_w_down[l // 2])]
    bf16_weights = {}

    h = x.reshape(batch * seq, d)
    for l in range(depth):
        w_gates = jnp.concatenate([_block_diag(gate_r_w[l]), _block_diag(gate_i_w[l])], axis=1).astype(BF16)
        b_gates = jnp.concatenate([gate_r_b[l], gate_i_b[l]]).reshape(1, -1)
        ya, yl, *bf16_weights[l] = _mixer(h, _in_weights(w_in[l]), attn_sinks[l] * LOG2E, bias2,
                                          row(norm_attn_g[l]), conv_w[l], row(conv_b[l]), w_gates, b_gates,
                                          row(lru_lambda[l]), row(norm_lru_g[l]), seq, rides[l])
        wo = w_out[l].astype(BF16)
        if l % 2 == 0:
            wg, wu, wd = bf16_weights[l][:3]
            h = _dense_block(ya, yl, h, wo, row(ln1_g[l]), row(ln1_b[l]), wg, wu, wd,
                             row(ln2_g[l]), row(ln2_b[l]), alpha)
        else:
            (wg, wu), wd = bf16_weights[l - 1][-2:], bf16_weights[l][-1]
            x1, x1_rows, route, meta, counts = _router_block(ya, yl, h, wo, row(ln1_g[l]), row(ln1_b[l]),
                                                             router_w[l // 2], alpha)
            h = _moe_block(x1, x1_rows, route, meta, counts, wg.reshape(exp_w_gate.shape[1:]),
                           wu.reshape(exp_w_up.shape[1:]), wd.reshape(exp_w_down.shape[1:]),
                           row(ln2_g[l]), row(ln2_b[l]), alpha)
    return h.reshape(batch, seq, d)
```

```python
import functools
import math

import jax
import jax.numpy as jnp
from jax import lax
from jax.experimental import pallas as pl
from jax.experimental.pallas import tpu as pltpu

F32 = jnp.float32
BF16 = jnp.bfloat16

D_MODEL = 1024
CHUNK = 64
HEAD_DIM = 64
N_Q_HEADS = 8
N_KV_HEADS = 2
ATTN_WIDTH = N_Q_HEADS * HEAD_DIM
KV_COLS = N_KV_HEADS * HEAD_DIM
WINDOW_CHUNKS = 2
LRU_WIDTH = 512
LRU_BLOCKS = 8
CONV_WIDTH = 4
LRU_C = 8.0
N_BUCKETS = 32
MAX_DISTANCE = 128
N_EXPERTS = 8
NEG_INF = -1e30
LN_EPS = 1e-5
RMS_EPS = 1e-6
LOG2E = math.log2(math.e)

LANES = 128
SUBLANES = 8
VMEM_LIMIT = 56 * 1024 * 1024

QKV_COLS = ATTN_WIDTH + 4 * KV_COLS
Q_BLOCK = 2 * CHUNK
K_BLOCK = (WINDOW_CHUNKS + 2) * CHUNK
ATTN_TILE = 512
LRU_TILE = 512
TOKEN_TILE = 512
ROUTER_SLABS = 2
MXU_TILE = 256
MOE_TILE = 512
MOE_F_TILE = 7 * MXU_TILE
DISPATCH_TILE = 4096
COMBINE_TILE = 512
DMA_UNROLL = 8


def _params(sem, **kw):
    return pltpu.CompilerParams(dimension_semantics=sem, vmem_limit_bytes=VMEM_LIMIT, **kw)


def _resident(shape):
    nd = len(shape)
    return pl.BlockSpec(shape, lambda *_: (0,) * nd, pipeline_mode=pl.Buffered(1))


def _layer_norm(x, g, b):
    mu = jnp.mean(x, axis=-1, keepdims=True)
    xc = x - mu
    var = jnp.mean(xc * xc, axis=-1, keepdims=True)
    return xc * lax.rsqrt(var + LN_EPS) * g + b


def _rms_norm(x, g):
    return x * lax.rsqrt(jnp.mean(x * x, axis=-1, keepdims=True) + RMS_EPS) * g


HALF = D_MODEL // 2
ROW_TILE = HALF // LANES
U32 = jnp.uint32
HIGH_HALF = 0xFFFF0000


def _load_row_tiles(ref, rows, lead=()):
    words = jnp.concatenate(
        [ref[lead + (pl.ds(c, rows, stride=ROW_TILE), slice(None))] for c in range(ROW_TILE)], axis=1)
    low = lax.bitcast_convert_type(lax.shift_left(words, U32(16)), F32)
    high = lax.bitcast_convert_type(jnp.bitwise_and(words, U32(HIGH_HALF)), F32)
    return jnp.concatenate([low, high], axis=1)


def _store_row_tiles(ref, value, first_row=0):
    bits = lax.bitcast_convert_type(value.astype(BF16).astype(F32), U32)
    words = jnp.bitwise_or(lax.shift_right_logical(bits[:, :HALF], U32(16)),
                           jnp.bitwise_and(bits[:, HALF:], U32(HIGH_HALF)))
    for c in range(ROW_TILE):
        ref[pl.ds(first_row * ROW_TILE + c, value.shape[0], stride=ROW_TILE), :] = (
            words[:, c * LANES:(c + 1) * LANES])


BF16_ROWS = 2 * SUBLANES


def _with_casts(kernel_fn, n_in, n_out, n_casts):
    def wrapped(*refs):
        ins, refs = refs[:n_in], refs[n_in:]
        srcs, refs = refs[:n_casts], refs[n_casts:]
        outs, refs = refs[:n_out], refs[n_out:]
        dsts, scratch = refs[:n_casts], refs[n_casts:]
        for src, dst in zip(srcs, dsts):
            dst[...] = src[...].astype(BF16)
        kernel_fn(*ins, *outs, *scratch)
    return wrapped


def _cast_specs(weights, linear_step, steps):
    specs, shapes = [], []
    for w in weights:
        rows, cols = w.shape
        per = next(p for p in range(BF16_ROWS, rows + 1, BF16_ROWS) if rows % p == 0 and p * steps >= rows)
        last = rows // per - 1
        specs.append(pl.BlockSpec((per, cols),
                                  lambda *g, last=last: (jnp.minimum(linear_step(*g), last), 0)))
        shapes.append(jax.ShapeDtypeStruct(w.shape, BF16))
    return specs, shapes


ATTN_HIST = Q_BLOCK
K_COL = ATTN_WIDTH


def _attn_fill(kk_ref, vv_ref, qkv_ref):
    lo = lax.broadcasted_iota(jnp.int32, (1, LANES), 1) < HEAD_DIM
    zero = jnp.zeros((), BF16)
    rows = slice(ATTN_HIST, ATTN_HIST + ATTN_TILE)
    for dst, col in ((kk_ref, K_COL), (vv_ref, K_COL + 2 * LANES)):
        for j in range(4):
            dst[j, :ATTN_HIST, :] = dst[j, ATTN_TILE:, :]
        a = qkv_ref[:, col:col + LANES]
        s = qkv_ref[:, col + LANES:col + 2 * LANES]
        dst[0, rows, :] = jnp.where(lo, a, zero)
        dst[1, rows, :] = jnp.where(lo, zero, s)
        dst[2, rows, :] = jnp.where(lo, s, zero)
        dst[3, rows, :] = jnp.where(lo, zero, a)


def _attn_unit(sub, h, seq_start, qkv_ref, kk_ref, vv_ref, bias_ref, sink_ref, acc_ref):
    q0 = sub * Q_BLOCK
    lo = lax.broadcasted_iota(jnp.int32, (1, LANES), 1) < HEAD_DIM
    dn = (((1,), (1,)), ((), ()))
    start_mask = None
    if sub == 0:
        col = lax.broadcasted_iota(jnp.int32, (1, K_BLOCK), 1)
        start_mask = jnp.where(jnp.logical_and(seq_start, col < ATTN_HIST), NEG_INF, 0.0).astype(F32)
    q2 = jnp.concatenate(
        [qkv_ref[pl.ds(q0, Q_BLOCK), pl.ds((2 * h + p) * LANES, LANES)] for p in range(2)], axis=0)
    probs = {}
    inv_den = {}
    for half in range(2):
        kk = kk_ref[2 * h + half, pl.ds(q0, K_BLOCK), :]
        sc = lax.dot_general(q2, kk, dn, preferred_element_type=F32)
        for p in range(2):
            head = 4 * h + 2 * p + half
            logit = sc[p * Q_BLOCK:(p + 1) * Q_BLOCK] + bias_ref[head]
            if start_mask is not None:
                logit = logit + start_mask
            sink = sink_ref[head]
            m = jnp.maximum(jnp.max(logit, axis=-1, keepdims=True), sink)
            e = jnp.exp2(logit - m)
            den = jnp.sum(e, axis=-1, keepdims=True) + jnp.exp2(sink - m)
            probs[(p, half)] = e.astype(BF16)
            inv_den[(p, half)] = 1.0 / den
    lhs = jnp.concatenate(
        [jnp.concatenate([probs[(p, 0)], probs[(p, 1)]], axis=1) for p in range(2)], axis=0)
    rhs = jnp.concatenate([vv_ref[2 * h, pl.ds(q0, K_BLOCK), :],
                           vv_ref[2 * h + 1, pl.ds(q0, K_BLOCK), :]], axis=0)
    out = jnp.dot(lhs, rhs, preferred_element_type=F32)
    for p in range(2):
        scale = jnp.where(lo, inv_den[(p, 0)], inv_den[(p, 1)])
        acc_ref[pl.ds(q0, Q_BLOCK), pl.ds((2 * h + p) * LANES, LANES)] = (
            out[p * Q_BLOCK:(p + 1) * Q_BLOCK] * scale)


def _lru_slab(c, x, gate, prev, h_in, cw_ref, cb_ref, wg_ref, bg_ref, lam_ref, a_ref, u_ref, ext_ref):
    t = x.shape[0]
    lanes = slice(c * LANES, (c + 1) * LANES)
    i_lanes = slice(LRU_WIDTH + c * LANES, LRU_WIDTH + (c + 1) * LANES)

    ext_ref[c, :SUBLANES, :] = prev
    ext_ref[c, SUBLANES:, :] = x
    xc = x * cw_ref[CONV_WIDTH - 1:CONV_WIDTH, lanes] + cb_ref[:, lanes]
    for k in range(1, CONV_WIDTH):
        xc = xc + (ext_ref[c, SUBLANES - k:SUBLANES - k + t, :]
                   * cw_ref[CONV_WIDTH - 1 - k:CONV_WIDTH - k, lanes])

    xcb = xc.astype(BF16)
    r = jax.nn.sigmoid(jnp.dot(xcb, wg_ref[lanes, lanes], preferred_element_type=F32) + bg_ref[:, lanes])
    i = jax.nn.sigmoid(jnp.dot(xcb, wg_ref[lanes, i_lanes], preferred_element_type=F32)
                       + bg_ref[:, i_lanes])
    neg_lam = -lam_ref[:, lanes]
    softplus = jnp.maximum(neg_lam, 0.0) + jnp.log1p(jnp.exp(-jnp.abs(neg_lam)))
    log_a = (-LRU_C) * r * softplus
    a = jnp.exp(log_a)
    h = jnp.sqrt(jnp.tanh(-log_a) * (1.0 + a * a)) * (i * xc)

    def doubling(av, hv, index, length, axis):
        d = 1
        while d < length:
            keep = index >= d
            a_sh = jnp.where(keep, pltpu.roll(av, d, axis), 1.0)
            h_sh = jnp.where(keep, pltpu.roll(hv, d, axis), 0.0)
            hv = av * h_sh + hv
            av = av * a_sh
            d *= 2
        return av, hv

    groups = t // SUBLANES
    grouped = (groups, SUBLANES, LANES)
    sub = lax.broadcasted_iota(jnp.int32, (1, SUBLANES, 1), 1)
    a, h = doubling(a.reshape(grouped), h.reshape(grouped), sub, SUBLANES, 1)
    a = a.reshape(t, LANES)
    h = h.reshape(t, LANES)
    grow = lax.broadcasted_iota(jnp.int32, (groups, 1), 0)

    def group_last(ref, value):
        ref[c] = value
        return ref[c, pl.ds(SUBLANES - 1, groups, stride=SUBLANES), :]

    ag, hg = doubling(group_last(a_ref, a), group_last(u_ref, h), grow, groups, 0)
    state = hg + ag * h_in
    enter = jnp.where(grow >= 1, pltpu.roll(state, 1, 0), h_in)
    enter = jnp.broadcast_to(enter[:, None, :], grouped).reshape(t, LANES)
    h = h + a * enter
    return jax.nn.gelu(gate) * h, state[groups - 1:groups, :]


def _mixer_kernel(sink_ref, x_ref, w_ref, cw_ref, cb_ref, wg_ref, bg_ref, lam_ref, gl_ref, bias_ref, ga_ref,
                  ya_ref, yl_ref,
                  q0_ref, q1_ref, z0_ref, z1_ref, kk_ref, vv_ref, acc_ref,
                  tail_ref, h_ref, a_ref, u_ref, ext_ref, y_ref, *, tiles_per_seq):
    s = pl.program_id(0)

    @pl.when(s == 0)
    def _():
        for ref in (q1_ref, z1_ref, kk_ref, vv_ref, tail_ref, h_ref):
            ref[...] = jnp.zeros_like(ref)

    def step(q_new_ref, z_new_ref, q_old_ref, z_old_ref):
        xb = x_ref[...].astype(BF16)
        seq_start = lax.rem(s - 1, tiles_per_seq) == 0

        def inproj_chunk(j):
            cols = slice(j * MXU_TILE, (j + 1) * MXU_TILE)
            z = jnp.dot(xb, w_ref[:, cols], preferred_element_type=F32)
            if cols.stop <= QKV_COLS:
                q_new_ref[:, cols] = z.astype(BF16)
            else:
                z_new_ref[:, cols.start - QKV_COLS:cols.stop - QKV_COLS] = z

        _attn_fill(kk_ref, vv_ref, q_old_ref)
        attn_units = [(sub, h) for sub in range(ATTN_TILE // Q_BLOCK) for h in range(N_KV_HEADS)]
        slabs = LRU_WIDTH // LANES
        chunks_per_slab = (QKV_COLS + 2 * LRU_WIDTH) // MXU_TILE // slabs
        units_per_slab = len(attn_units) // slabs
        sq = None
        for c in range(slabs):
            lanes = slice(c * LANES, (c + 1) * LANES)
            inproj_chunk(chunks_per_slab * c)
            x = z_old_ref[:, lanes]
            prev = jnp.where(seq_start, 0.0, tail_ref[:, lanes])
            h_in = jnp.where(seq_start, 0.0, h_ref[:, lanes])
            y, h_out = _lru_slab(c, x, z_old_ref[:, LRU_WIDTH + c * LANES:LRU_WIDTH + (c + 1) * LANES],
                                 prev, h_in, cw_ref, cb_ref, wg_ref, bg_ref, lam_ref, a_ref, u_ref, ext_ref)
            y_ref[:, lanes] = y
            part = jnp.sum(y * y, axis=-1, keepdims=True)
            sq = part if sq is None else sq + part
            h_ref[:, lanes] = h_out
            tail_ref[:, lanes] = x[LRU_TILE - SUBLANES:, :]
            for k in range(max(chunks_per_slab - 1, units_per_slab)):
                if k < units_per_slab:
                    sub, h = attn_units[units_per_slab * c + k]
                    _attn_unit(sub, h, seq_start, q_old_ref, kk_ref, vv_ref, bias_ref, sink_ref, acc_ref)
                if k < chunks_per_slab - 1:
                    inproj_chunk(chunks_per_slab * c + 1 + k)
        scale = lax.rsqrt(sq * (1.0 / LRU_WIDTH) + RMS_EPS)
        yl_ref[...] = (y_ref[...] * scale * gl_ref[...]).astype(BF16)
        ya_ref[...] = _rms_norm(acc_ref[...], ga_ref[...]).astype(BF16)

    @pl.when(s % 2 == 0)
    def _():
        step(q0_ref, z0_ref, q1_ref, z1_ref)

    @pl.when(s % 2 == 1)
    def _():
        step(q1_ref, z1_ref, q0_ref, z0_ref)


def _mixer(x2d, w, sinks, bias2, g_attn, conv_w, conv_b, w_gates, b_gates, lam, g_lru, seq, casts=()):
    n = x2d.shape[0]
    assert ATTN_TILE == LRU_TILE
    tiles = n // LRU_TILE
    last = tiles - 1
    cast_specs, cast_shapes = _cast_specs(casts, lambda s: s, tiles + 1)
    out_tile = lambda width: pl.BlockSpec((LRU_TILE, width), lambda s: (jnp.maximum(s - 1, 0), 0))
    return pl.pallas_call(
        _with_casts(functools.partial(_mixer_kernel, tiles_per_seq=seq // LRU_TILE), 11, 2, len(casts)),
        grid=(tiles + 1,),
        in_specs=[pl.BlockSpec(memory_space=pltpu.SMEM),
                  pl.BlockSpec((LRU_TILE, D_MODEL), lambda s: (jnp.minimum(s, last), 0)),
                  _resident((D_MODEL, QKV_COLS + 2 * LRU_WIDTH)),
                  _resident((CONV_WIDTH, LRU_WIDTH)),
                  _resident((1, LRU_WIDTH)),
                  _resident((LRU_WIDTH, 2 * LRU_WIDTH)),
                  _resident((1, 2 * LRU_WIDTH)),
                  _resident((1, LRU_WIDTH)),
                  _resident((1, LRU_WIDTH)),
                  _resident((N_Q_HEADS, Q_BLOCK, K_BLOCK)),
                  _resident((1, ATTN_WIDTH))] + cast_specs,
        out_specs=[out_tile(ATTN_WIDTH), out_tile(LRU_WIDTH)] + cast_specs,
        out_shape=[jax.ShapeDtypeStruct((n, ATTN_WIDTH), BF16),
                   jax.ShapeDtypeStruct((n, LRU_WIDTH), BF16)] + cast_shapes,
        scratch_shapes=[pltpu.VMEM((ATTN_TILE, QKV_COLS), BF16),
                        pltpu.VMEM((ATTN_TILE, QKV_COLS), BF16),
                        pltpu.VMEM((LRU_TILE, 2 * LRU_WIDTH), F32),
                        pltpu.VMEM((LRU_TILE, 2 * LRU_WIDTH), F32),
                        pltpu.VMEM((4, ATTN_HIST + ATTN_TILE, LANES), BF16),
                        pltpu.VMEM((4, ATTN_HIST + ATTN_TILE, LANES), BF16),
                        pltpu.VMEM((ATTN_TILE, ATTN_WIDTH), F32),
                        pltpu.VMEM((SUBLANES, LRU_WIDTH), F32),
                        pltpu.VMEM((1, LRU_WIDTH), F32),
                        pltpu.VMEM((LRU_WIDTH // LANES, LRU_TILE, LANES), F32),
                        pltpu.VMEM((LRU_WIDTH // LANES, LRU_TILE, LANES), F32),
                        pltpu.VMEM((LRU_WIDTH // LANES, SUBLANES + LRU_TILE, LANES), F32),
                        pltpu.VMEM((LRU_TILE, LRU_WIDTH), F32)],
        compiler_params=_params(("arbitrary",)),
        name="mixer",
    )(sinks, x2d, w, conv_w, conv_b, w_gates, b_gates, lam, g_lru, bias2, g_attn, *casts)


def _mix_ln1(ya_ref, yl_ref, x_ref, wo_ref, g1_ref, b1_ref, alpha, rows=slice(None)):
    y = jnp.dot(ya_ref[rows, :], wo_ref[:ATTN_WIDTH, :], preferred_element_type=F32)
    y = y + jnp.dot(yl_ref[rows, :], wo_ref[ATTN_WIDTH:, :], preferred_element_type=F32)
    return _layer_norm(alpha * x_ref[rows, :] + y, g1_ref[...], b1_ref[...])


def _dense_block_kernel(ya_ref, yl_ref, x_ref, wo_ref, g1_ref, b1_ref, wg_ref, wu_ref, wd_ref,
                        g2_ref, b2_ref, o_ref, *, alpha, f_splits):
    x1 = _mix_ln1(ya_ref, yl_ref, x_ref, wo_ref, g1_ref, b1_ref, alpha)
    xb = x1.astype(BF16)
    f = None
    for lo, hi in zip(f_splits[:-1], f_splits[1:]):
        cols = slice(lo, hi)
        g = jnp.dot(xb, wg_ref[:, cols], preferred_element_type=F32)
        u = jnp.dot(xb, wu_ref[:, cols], preferred_element_type=F32)
        hid = (jax.nn.silu(g) * u).astype(BF16)
        part = jnp.dot(hid, wd_ref[cols, :], preferred_element_type=F32)
        f = part if f is None else f + part
    o_ref[...] = _layer_norm(alpha * x1 + f, g2_ref[...], b2_ref[...])


def _dense_block(ya, yl, x2d, wo, g1, b1, wg, wu, wd, g2, b2, alpha):
    n = x2d.shape[0]
    d_ff = wg.shape[1]
    tile = lambda w: pl.BlockSpec((TOKEN_TILE, w), lambda i: (i, 0))
    mxu_tiles = pl.cdiv(d_ff, MXU_TILE)
    f_splits = (0, (mxu_tiles + 1) // 2 * MXU_TILE, d_ff)
    return pl.pallas_call(
        functools.partial(_dense_block_kernel, alpha=alpha, f_splits=f_splits),
        grid=(n // TOKEN_TILE,),
        in_specs=[tile(ATTN_WIDTH), tile(LRU_WIDTH), tile(D_MODEL),
                  _resident((D_MODEL, D_MODEL)), _resident((1, D_MODEL)), _resident((1, D_MODEL)),
                  _resident((D_MODEL, d_ff)), _resident((D_MODEL, d_ff)), _resident((d_ff, D_MODEL)),
                  _resident((1, D_MODEL)), _resident((1, D_MODEL))],
        out_specs=tile(D_MODEL),
        out_shape=jax.ShapeDtypeStruct((n, D_MODEL), F32),
        compiler_params=_params(("parallel",)),
        name="dense_block",
    )(ya, yl, x2d, wo, g1, b1, wg, wu, wd, g2, b2)


_L_IDX, _L_W, _L_RANK = 0, 2, 4


def _router_kernel(ya_ref, yl_ref, x_ref, wo_ref, g1_ref, b1_ref, rw_ref,
                   x1_ref, x1p_ref, route_ref, meta_ref, count_ref, carry_ref, *, alpha):
    @pl.when(pl.program_id(0) == 0)
    def _():
        carry_ref[...] = jnp.zeros_like(carry_ref)

    t = x_ref.shape[0] // ROUTER_SLABS
    carry = carry_ref[:, :1]
    for s in range(ROUTER_SLABS):
        carry = _route_slab(s * t, t, carry, ya_ref, yl_ref, x_ref, wo_ref, g1_ref, b1_ref, rw_ref,
                            x1_ref, x1p_ref, route_ref, meta_ref, alpha)
    carry_ref[...] = jnp.broadcast_to(carry, carry_ref.shape)
    count_ref[...] = jnp.broadcast_to(carry, count_ref.shape)


def _route_slab(first, t, carry, ya_ref, yl_ref, x_ref, wo_ref, g1_ref, b1_ref, rw_ref,
                x1_ref, x1p_ref, route_ref, meta_ref, alpha):
    x1 = _mix_ln1(ya_ref, yl_ref, x_ref, wo_ref, g1_ref, b1_ref, alpha, pl.ds(first, t))
    x1_ref[pl.ds(first, t), :] = x1
    _store_row_tiles(x1p_ref, x1, first)
    x_hi = x1.astype(BF16)
    x_lo = (x1 - x_hi.astype(F32)).astype(BF16)
    both = jnp.dot(x_hi, rw_ref[...], preferred_element_type=F32)
    logits = both[:, :LANES] + both[:, LANES:] + jnp.dot(x_lo, rw_ref[:, :LANES], preferred_element_type=F32)

    lt = logits.T[:N_EXPERTS]
    row = lax.broadcasted_iota(jnp.int32, (N_EXPERTS, t), 0).astype(F32)
    lowest = jnp.finfo(F32).min
    m1 = jnp.max(lt, axis=0, keepdims=True)
    i1 = jnp.min(jnp.where(lt == m1, row, float(N_EXPERTS)), axis=0, keepdims=True)
    l2 = jnp.where(row == i1, lowest, lt)
    m2 = jnp.max(l2, axis=0, keepdims=True)
    i2 = jnp.min(jnp.where(l2 == m2, row, float(N_EXPERTS)), axis=0, keepdims=True)
    e = jnp.exp(m2 - m1)
    w1 = 1.0 / (1.0 + e)
    w2 = e / (1.0 + e)

    sel = jnp.where(jnp.logical_or(row == i1, row == i2), 1.0, 0.0)
    r_i = lax.broadcasted_iota(jnp.int32, (t, t), 0)
    c_i = lax.broadcasted_iota(jnp.int32, (t, t), 1)
    earlier = jnp.where(r_i < c_i, 1.0, 0.0).astype(BF16)
    before = jnp.dot(sel.astype(BF16), earlier, preferred_element_type=F32) + carry
    rank1 = jnp.sum(jnp.where(row == i1, before, 0.0), axis=0, keepdims=True)
    rank2 = jnp.sum(jnp.where(row == i2, before, 0.0), axis=0, keepdims=True)

    meta = jnp.where(row == _L_IDX, i1, 0.0)
    meta = jnp.where(row == _L_IDX + 1, i2, meta)
    meta = jnp.where(row == _L_W, w1, meta)
    meta = jnp.where(row == _L_W + 1, w2, meta)
    meta = jnp.where(row == _L_RANK, rank1, meta)
    meta = jnp.where(row == _L_RANK + 1, rank2, meta)
    meta_ref[:, pl.ds(first, t)] = meta
    padded = jnp.concatenate([meta, jnp.zeros((LANES - N_EXPERTS, t), F32)], axis=0)
    route_ref[pl.ds(first, t), :] = padded.T
    return carry + jnp.sum(sel, axis=1, keepdims=True)


def _router_block(ya, yl, x2d, wo, g1, b1, router_w, alpha):
    n = x2d.shape[0]
    assert router_w.shape[1] == N_EXPERTS == SUBLANES
    rw = jnp.pad(router_w, ((0, 0), (0, LANES - N_EXPERTS)))
    w_hi = rw.astype(BF16)
    w_lo = (rw - w_hi.astype(F32)).astype(BF16)
    tile = lambda w: pl.BlockSpec((TOKEN_TILE, w), lambda i: (i, 0))
    return pl.pallas_call(
        functools.partial(_router_kernel, alpha=alpha),
        grid=(n // TOKEN_TILE,),
        in_specs=[tile(ATTN_WIDTH), tile(LRU_WIDTH), tile(D_MODEL),
                  _resident((D_MODEL, D_MODEL)), _resident((1, D_MODEL)), _resident((1, D_MODEL)),
                  _resident((D_MODEL, 2 * LANES))],
        out_specs=[tile(D_MODEL), pl.BlockSpec((TOKEN_TILE * ROW_TILE, LANES), lambda i: (i, 0)), tile(LANES),
                   pl.BlockSpec((SUBLANES, TOKEN_TILE), lambda i: (0, i)),
                   pl.BlockSpec((N_EXPERTS, LANES), lambda i: (0, 0))],
        out_shape=[jax.ShapeDtypeStruct((n, D_MODEL), F32),
                   jax.ShapeDtypeStruct((n * ROW_TILE, LANES), U32),
                   jax.ShapeDtypeStruct((n, LANES), F32),
                   jax.ShapeDtypeStruct((SUBLANES, n), F32),
                   jax.ShapeDtypeStruct((N_EXPERTS, LANES), F32)],
        scratch_shapes=[pltpu.VMEM((N_EXPERTS, LANES), F32)],
        compiler_params=_params(("arbitrary",)),
        name="router_block",
    )(ya, yl, x2d, wo, g1, b1, jnp.concatenate([w_hi, w_lo], axis=1))


def _tile_copy(src_ref, src_row, dst_ref, dst_row, sem):
    src = src_ref.at[pl.ds(pl.multiple_of(src_row * ROW_TILE, ROW_TILE), ROW_TILE)]
    dst = dst_ref.at[pl.ds(pl.multiple_of(dst_row * ROW_TILE, ROW_TILE), ROW_TILE)]
    return pltpu.make_async_copy(src, dst, sem)


def _dispatch_kernel(pad_start_ref, pad_len_ref, pos0_ref, pos1_ref, x_ref, xs_ref, sem, pad_sem):
    def issue(t, carry):
        _tile_copy(x_ref, t, xs_ref, pos0_ref[t], sem).start(priority=0)
        _tile_copy(x_ref, t, xs_ref, pos1_ref[t], sem).start(priority=1)
        return carry

    lax.fori_loop(0, DISPATCH_TILE, issue, 0, unroll=DMA_UNROLL)
    for _ in range(2):
        pltpu.make_async_copy(x_ref, xs_ref.at[pl.ds(0, DISPATCH_TILE * ROW_TILE)], sem).wait()

    @pl.when(pl.program_id(0) == pl.num_programs(0) - 1)
    def _():
        for e in range(N_EXPERTS):
            def fill(r, carry):
                _tile_copy(x_ref, 0, xs_ref, pad_start_ref[e] + r, pad_sem).start()
                return carry

            def drain(r, carry):
                _tile_copy(x_ref, 0, xs_ref, 0, pad_sem).wait()
                return carry

            lax.fori_loop(0, pad_len_ref[e], fill, 0)
            lax.fori_loop(0, pad_len_ref[e], drain, 0)

        def idle_copy(r):
            first = pl.multiple_of((pad_start_ref[N_EXPERTS] + r * MOE_TILE) * ROW_TILE, MOE_TILE * ROW_TILE)
            return pltpu.make_async_copy(x_ref.at[pl.ds(0, MOE_TILE * ROW_TILE)],
                                         xs_ref.at[pl.ds(first, MOE_TILE * ROW_TILE)], pad_sem)

        def fill_idle(r, carry):
            idle_copy(r).start()
            return carry

        def drain_idle(r, carry):
            idle_copy(r).wait()
            return carry

        lax.fori_loop(0, pad_len_ref[N_EXPERTS], fill_idle, 0)
        lax.fori_loop(0, pad_len_ref[N_EXPERTS], drain_idle, 0)


def _dispatch(pad_start, pad_len, pos0, pos1, x1, sorted_rows):
    n = x1.shape[0] // ROW_TILE
    assert DISPATCH_TILE >= MOE_TILE
    smem_tile = pl.BlockSpec((DISPATCH_TILE,), lambda i, *_: (i,), memory_space=pltpu.SMEM)
    grid_spec = pltpu.PrefetchScalarGridSpec(
        num_scalar_prefetch=2,
        grid=(n // DISPATCH_TILE,),
        in_specs=[smem_tile, smem_tile,
                  pl.BlockSpec((DISPATCH_TILE * ROW_TILE, LANES), lambda i, *_: (i, 0))],
        out_specs=pl.BlockSpec(memory_space=pl.ANY),
        scratch_shapes=[pltpu.SemaphoreType.DMA(()), pltpu.SemaphoreType.DMA(())],
    )
    return pl.pallas_call(
        _dispatch_kernel,
        grid_spec=grid_spec,
        out_shape=jax.ShapeDtypeStruct((sorted_rows * ROW_TILE, LANES), U32),
        compiler_params=_params(("arbitrary",), disable_bounds_checks=True),
        name="moe_dispatch",
    )(pad_start, pad_len, pos0, pos1, x1)


def _experts_kernel(tile_expert_ref, n_used_ref, x_ref, wg_ref, wu_ref, wd_ref, y_ref, xb_ref, acc_ref):
    del tile_expert_ref
    i = pl.program_id(0)
    j = pl.program_id(1)
    last = pl.num_programs(1) - 1

    @pl.when(i < n_used_ref[0])
    def _():
        @pl.when(j == 0)
        def _():
            xb_ref[...] = _load_row_tiles(x_ref, MOE_TILE).astype(BF16)

        xb = xb_ref[...]
        g = jnp.dot(xb, wg_ref[...], preferred_element_type=F32)
        u = jnp.dot(xb, wu_ref[...], preferred_element_type=F32)
        hid = (jax.nn.silu(g) * u).astype(BF16)
        part = jnp.dot(hid, wd_ref[...], preferred_element_type=F32)

        @pl.when(j == 0)
        def _():
            acc_ref[...] = part

        @pl.when(jnp.logical_and(j > 0, j < last))
        def _():
            acc_ref[...] += part

        @pl.when(j == last)
        def _():
            _store_row_tiles(y_ref, acc_ref[...] + part)

    @pl.when(jnp.logical_and(i >= n_used_ref[0], j == 0))
    def _():
        y_ref[...] = jnp.zeros_like(y_ref)


def _experts(tile_expert, n_used, xs, wg, wu, wd):
    rows = xs.shape[0] // ROW_TILE
    d_ff = wg.shape[2]
    n_tiles = rows // MOE_TILE
    n_f = d_ff // MOE_F_TILE
    assert n_f >= 2

    def row_map(i, j, te, nu):
        return (jnp.minimum(i, nu[0] - 1), 0)

    def f_of(i, j, nu):
        return jnp.where(i < nu[0], j, n_f - 1)

    grid_spec = pltpu.PrefetchScalarGridSpec(
        num_scalar_prefetch=2,
        grid=(n_tiles, n_f),
        in_specs=[pl.BlockSpec((MOE_TILE * ROW_TILE, LANES), row_map),
                  pl.BlockSpec((None, D_MODEL, MOE_F_TILE), lambda i, j, te, nu: (te[i], 0, f_of(i, j, nu))),
                  pl.BlockSpec((None, D_MODEL, MOE_F_TILE), lambda i, j, te, nu: (te[i], 0, f_of(i, j, nu))),
                  pl.BlockSpec((None, MOE_F_TILE, D_MODEL), lambda i, j, te, nu: (te[i], f_of(i, j, nu), 0))],
        out_specs=pl.BlockSpec((MOE_TILE * ROW_TILE, LANES), lambda i, j, te, nu: (i, 0)),
        scratch_shapes=[pltpu.VMEM((MOE_TILE, D_MODEL), BF16), pltpu.VMEM((MOE_TILE, D_MODEL), F32)],
    )
    return pl.pallas_call(
        _experts_kernel,
        grid_spec=grid_spec,
        out_shape=jax.ShapeDtypeStruct((rows * ROW_TILE, LANES), U32),
        compiler_params=_params(("arbitrary", "arbitrary")),
        name="moe_experts",
    )(tile_expert, n_used, xs, wg, wu, wd)


def _combine_kernel(pos0_ref, pos1_ref, nxt0_ref, nxt1_ref, ys_ref, x1_ref, route_ref, g2_ref, b2_ref,
                    o_ref, buf_ref, sem, *, alpha):
    i = pl.program_id(0)
    slot = i % 2

    def issue(p0_ref, p1_ref, dst_slot):
        def body(t, carry):
            for k, p_ref in enumerate((p0_ref, p1_ref)):
                _tile_copy(ys_ref, p_ref[t], buf_ref.at[dst_slot, k], t, sem.at[dst_slot]).start(priority=k)
            return carry
        lax.fori_loop(0, COMBINE_TILE, body, 0, unroll=DMA_UNROLL)

    @pl.when(i == 0)
    def _():
        issue(pos0_ref, pos1_ref, 0)

    @pl.when(i + 1 < pl.num_programs(0))
    def _():
        issue(nxt0_ref, nxt1_ref, 1 - slot)

    for k in range(2):
        pltpu.make_async_copy(ys_ref.at[pl.ds(0, COMBINE_TILE * ROW_TILE)], buf_ref.at[slot, k],
                              sem.at[slot]).wait()
    route = route_ref[...]
    y0 = _load_row_tiles(buf_ref, COMBINE_TILE, (slot, 0))
    y1 = _load_row_tiles(buf_ref, COMBINE_TILE, (slot, 1))
    f = route[:, _L_W:_L_W + 1] * y0 + route[:, _L_W + 1:_L_W + 2] * y1
    o_ref[...] = _layer_norm(alpha * x1_ref[...] + f, g2_ref[...], b2_ref[...])


def _combine(pos0, pos1, ys, x1, route, g2, b2, alpha):
    n = x1.shape[0]
    steps = n // COMBINE_TILE
    tile = lambda w: pl.BlockSpec((COMBINE_TILE, w), lambda i: (i, 0))
    smem_cur = pl.BlockSpec((COMBINE_TILE,), lambda i: (i,), memory_space=pltpu.SMEM)
    smem_next = pl.BlockSpec((COMBINE_TILE,), lambda i: (jnp.minimum(i + 1, steps - 1),),
                             memory_space=pltpu.SMEM)
    return pl.pallas_call(
        functools.partial(_combine_kernel, alpha=alpha),
        grid=(steps,),
        in_specs=[smem_cur, smem_cur, smem_next, smem_next,
                  pl.BlockSpec(memory_space=pl.ANY),
                  tile(D_MODEL), tile(LANES), _resident((1, D_MODEL)), _resident((1, D_MODEL))],
        out_specs=tile(D_MODEL),
        out_shape=jax.ShapeDtypeStruct((n, D_MODEL), F32),
        scratch_shapes=[pltpu.VMEM((2, 2, COMBINE_TILE * ROW_TILE, LANES), U32),
                        pltpu.SemaphoreType.DMA((2,))],
        compiler_params=_params(("arbitrary",), disable_bounds_checks=True),
        name="moe_combine",
    )(pos0, pos1, pos0, pos1, ys, x1, route, g2, b2)


def _moe_block(x1, x1_rows, route, meta, counts, wg, wu, wd, g2, b2, alpha):
    n = route.shape[0]
    n_tiles = (2 * n) // MOE_TILE + N_EXPERTS
    count = counts[:, 0].astype(jnp.int32)
    tiles_per = (count + MOE_TILE - 1) // MOE_TILE
    tile_end = jnp.cumsum(tiles_per)
    group_start = (tile_end - tiles_per) * MOE_TILE
    n_used = tile_end[-1:]
    tile_id = jnp.minimum(jnp.arange(n_tiles, dtype=jnp.int32), n_used[0] - 1)
    tile_expert = jnp.sum((tile_id[:, None] >= tile_end[None, :]).astype(jnp.int32), axis=1)
    idx = meta[_L_IDX:_L_IDX + 2].astype(jnp.int32)
    rank = meta[_L_RANK:_L_RANK + 2].astype(jnp.int32)
    start = jnp.sum(jnp.where(idx[:, :, None] == jnp.arange(N_EXPERTS), group_start, 0), axis=-1)
    pos = start + rank

    pad_start = jnp.concatenate([group_start + count, n_used * MOE_TILE])
    pad_len = jnp.concatenate([tiles_per * MOE_TILE - count, n_tiles - n_used])
    xs = _dispatch(pad_start, pad_len, pos[0], pos[1], x1_rows, n_tiles * MOE_TILE)
    ys = _experts(tile_expert, n_used, xs, wg, wu, wd)
    return _combine(pos[0], pos[1], ys, x1, route, g2, b2, alpha)


def _t5_bucket(rel):
    half = N_BUCKETS // 2
    max_exact = half // 2
    ret = jnp.where(rel > 0, half, 0)
    n = jnp.abs(rel)
    large = max_exact + (jnp.log(jnp.maximum(n, 1).astype(F32) / max_exact)
                         / math.log(MAX_DISTANCE / max_exact) * (half - max_exact)).astype(jnp.int32)
    large = jnp.minimum(large, half - 1)
    return ret + jnp.where(n < max_exact, n, large)


def _block_bias(rel_bias):
    qi = jnp.arange(Q_BLOCK)[:, None]
    kj = jnp.arange(K_BLOCK)[None, :]
    rel = (kj - WINDOW_CHUNKS * CHUNK) - qi
    hit = _t5_bucket(rel)[None, :, :, None] == jnp.arange(N_BUCKETS)
    bias = jnp.sum(jnp.where(hit, rel_bias.T[:, None, None, :], 0.0), axis=-1)
    first = (kj // CHUNK) - (qi // CHUNK)
    visible = jnp.logical_and(first >= 0, first <= WINDOW_CHUNKS)
    return jnp.where(visible[None], bias, NEG_INF).astype(F32) * LOG2E


def _in_weights(w_in):
    q = w_in[:, :ATTN_WIDTH] * (HEAD_DIM ** -0.5 * LOG2E)
    k = w_in[:, ATTN_WIDTH:ATTN_WIDTH + KV_COLS]
    v = w_in[:, ATTN_WIDTH + KV_COLS:ATTN_WIDTH + 2 * KV_COLS]
    rest = w_in[:, ATTN_WIDTH + 2 * KV_COLS:]
    swap = lambda w: jnp.concatenate([w[:, HEAD_DIM:], w[:, :HEAD_DIM]], axis=1)
    return jnp.concatenate([q, k, swap(k), v, swap(v), rest], axis=1).astype(BF16)


def _block_diag(w):
    nb, d, e = w.shape
    eye = jnp.eye(nb, dtype=w.dtype)
    return (eye[:, None, :, None] * w[:, :, None, :]).reshape(nb * d, nb * e)


def kernel(x, rel_bias, w_in, attn_sinks, conv_w, conv_b, gate_r_w, gate_r_b, gate_i_w, gate_i_b,
           lru_lambda, norm_attn_g, norm_lru_g, w_out, ln1_g, ln1_b, ffn_w_gate, ffn_w_up, ffn_w_down,
           router_w, exp_w_gate, exp_w_up, exp_w_down, ln2_g, ln2_b):
    batch, seq, d = x.shape
    depth = w_in.shape[0]
    alpha = float((2 * depth) ** 0.25)
    bias2 = _block_bias(rel_bias)
    row = lambda v: v.reshape(1, -1)

    rides = {l: [] for l in range(depth)}
    flat = lambda w: w.reshape(-1, w.shape[-1])
    for l in range(depth):
        if l % 2 == 0:
            rides[l] += [ffn_w_gate[l // 2], ffn_w_up[l // 2], ffn_w_down[l // 2]]
        else:
            rides[l - 1] += [flat(exp_w_gate[l // 2]), flat(exp_w_up[l // 2])]
            rides[l] += [flat(exp_w_down[l // 2])]
    bf16_weights = {}

    h = x.reshape(batch * seq, d)
    for l in range(depth):
        w_gates = jnp.concatenate([_block_diag(gate_r_w[l]), _block_diag(gate_i_w[l])], axis=1).astype(BF16)
        b_gates = jnp.concatenate([gate_r_b[l], gate_i_b[l]]).reshape(1, -1)
        ya, yl, *bf16_weights[l] = _mixer(h, _in_weights(w_in[l]), attn_sinks[l] * LOG2E, bias2,
                                          row(norm_attn_g[l]), conv_w[l], row(conv_b[l]), w_gates, b_gates,
                                          row(lru_lambda[l]), row(norm_lru_g[l]), seq, rides[l])
        wo = w_out[l].astype(BF16)
        if l % 2 == 0:
            wg, wu, wd = bf16_weights[l][:3]
            h = _dense_block(ya, yl, h, wo, row(ln1_g[l]), row(ln1_b[l]), wg, wu, wd,
                             row(ln2_g[l]), row(ln2_b[l]), alpha)
        else:
            (wg, wu), wd = bf16_weights[l - 1][-2:], bf16_weights[l][-1]
            x1, x1_rows, route, meta, counts = _router_block(ya, yl, h, wo, row(ln1_g[l]), row(ln1_b[l]),
                                                             router_w[l // 2], alpha)
            h = _moe_block(x1, x1_rows, route, meta, counts, wg.reshape(exp_w_gate.shape[1:]),
                           wu.reshape(exp_w_up.shape[1:]), wd.reshape(exp_w_down.shape[1:]),
                           row(ln2_g[l]), row(ln2_b[l]), alpha)
    return h.reshape(batch, seq, d)
```

```python
import functools
import math

import jax
import jax.numpy as jnp
from jax import lax
from jax.experimental import pallas as pl
from jax.experimental.pallas import tpu as pltpu

F32 = jnp.float32
BF16 = jnp.bfloat16

D_MODEL = 1024
CHUNK = 64
HEAD_DIM = 64
N_Q_HEADS = 8
N_KV_HEADS = 2
ATTN_WIDTH = N_Q_HEADS * HEAD_DIM
KV_COLS = N_KV_HEADS * HEAD_DIM
WINDOW_CHUNKS = 2
LRU_WIDTH = 512
LRU_BLOCKS = 8
CONV_WIDTH = 4
LRU_C = 8.0
N_BUCKETS = 32
MAX_DISTANCE = 128
N_EXPERTS = 8
NEG_INF = -1e30
LN_EPS = 1e-5
RMS_EPS = 1e-6
LOG2E = math.log2(math.e)

LANES = 128
SUBLANES = 8
VMEM_LIMIT = 56 * 1024 * 1024

QKV_COLS = ATTN_WIDTH + 4 * KV_COLS
Q_BLOCK = 2 * CHUNK
K_BLOCK = (WINDOW_CHUNKS + 2) * CHUNK
ATTN_TILE = 512
LRU_TILE = 512
TOKEN_TILE = 512
DENSE_TILE = 1024
ROUTER_SLABS = 2
MXU_TILE = 256
MOE_TILE = 512
MOE_F_TILE = 7 * MXU_TILE
DISPATCH_TILE = 4096
COMBINE_TILE = 512
DMA_UNROLL = 8


def _params(sem, **kw):
    return pltpu.CompilerParams(dimension_semantics=sem, vmem_limit_bytes=VMEM_LIMIT, **kw)


def _resident(shape):
    nd = len(shape)
    return pl.BlockSpec(shape, lambda *_: (0,) * nd, pipeline_mode=pl.Buffered(1))


def _layer_norm(x, g, b):
    mu = jnp.mean(x, axis=-1, keepdims=True)
    xc = x - mu
    var = jnp.mean(xc * xc, axis=-1, keepdims=True)
    return xc * lax.rsqrt(var + LN_EPS) * g + b


def _rms_norm(x, g):
    return x * lax.rsqrt(jnp.mean(x * x, axis=-1, keepdims=True) + RMS_EPS) * g


HALF = D_MODEL // 2
ROW_TILE = HALF // LANES
U32 = jnp.uint32
HIGH_HALF = 0xFFFF0000


def _load_row_tiles(ref, rows, lead=()):
    words = jnp.concatenate(
        [ref[lead + (pl.ds(c, rows, stride=ROW_TILE), slice(None))] for c in range(ROW_TILE)], axis=1)
    low = lax.bitcast_convert_type(lax.shift_left(words, U32(16)), F32)
    high = lax.bitcast_convert_type(jnp.bitwise_and(words, U32(HIGH_HALF)), F32)
    return jnp.concatenate([low, high], axis=1)


def _store_row_tiles(ref, value, first_row=0):
    bits = lax.bitcast_convert_type(value.astype(BF16).astype(F32), U32)
    words = jnp.bitwise_or(lax.shift_right_logical(bits[:, :HALF], U32(16)),
                           jnp.bitwise_and(bits[:, HALF:], U32(HIGH_HALF)))
    for c in range(ROW_TILE):
        ref[pl.ds(first_row * ROW_TILE + c, value.shape[0], stride=ROW_TILE), :] = (
            words[:, c * LANES:(c + 1) * LANES])


BF16_ROWS = 2 * SUBLANES


def _with_casts(kernel_fn, n_in, n_out, n_casts):
    def wrapped(*refs):
        ins, refs = refs[:n_in], refs[n_in:]
        srcs, refs = refs[:n_casts], refs[n_casts:]
        outs, refs = refs[:n_out], refs[n_out:]
        dsts, scratch = refs[:n_casts], refs[n_casts:]
        for src, dst in zip(srcs, dsts):
            dst[...] = src[...].astype(BF16)
        kernel_fn(*ins, *outs, *scratch)
    return wrapped


def _cast_specs(weights, linear_step, steps):
    specs, shapes = [], []
    for w in weights:
        rows, cols = w.shape
        per = next(p for p in range(BF16_ROWS, rows + 1, BF16_ROWS) if rows % p == 0 and p * steps >= rows)
        last = rows // per - 1
        specs.append(pl.BlockSpec((per, cols),
                                  lambda *g, last=last: (jnp.minimum(linear_step(*g), last), 0)))
        shapes.append(jax.ShapeDtypeStruct(w.shape, BF16))
    return specs, shapes


ATTN_HIST = Q_BLOCK
K_COL = ATTN_WIDTH


def _attn_fill(kk_ref, vv_ref, qkv_ref):
    lo = lax.broadcasted_iota(jnp.int32, (1, LANES), 1) < HEAD_DIM
    zero = jnp.zeros((), BF16)
    rows = slice(ATTN_HIST, ATTN_HIST + ATTN_TILE)
    for dst, col in ((kk_ref, K_COL), (vv_ref, K_COL + 2 * LANES)):
        for j in range(4):
            dst[j, :ATTN_HIST, :] = dst[j, ATTN_TILE:, :]
        a = qkv_ref[:, col:col + LANES]
        s = qkv_ref[:, col + LANES:col + 2 * LANES]
        dst[0, rows, :] = jnp.where(lo, a, zero)
        dst[1, rows, :] = jnp.where(lo, zero, s)
        dst[2, rows, :] = jnp.where(lo, s, zero)
        dst[3, rows, :] = jnp.where(lo, zero, a)


def _attn_unit(sub, h, seq_start, qkv_ref, kk_ref, vv_ref, bias_ref, sink_ref, acc_ref):
    q0 = sub * Q_BLOCK
    lo = lax.broadcasted_iota(jnp.int32, (1, LANES), 1) < HEAD_DIM
    dn = (((1,), (1,)), ((), ()))
    start_mask = None
    if sub == 0:
        col = lax.broadcasted_iota(jnp.int32, (1, K_BLOCK), 1)
        start_mask = jnp.where(jnp.logical_and(seq_start, col < ATTN_HIST), NEG_INF, 0.0).astype(F32)
    q2 = jnp.concatenate(
        [qkv_ref[pl.ds(q0, Q_BLOCK), pl.ds((2 * h + p) * LANES, LANES)] for p in range(2)], axis=0)
    probs = {}
    inv_den = {}
    for half in range(2):
        kk = kk_ref[2 * h + half, pl.ds(q0, K_BLOCK), :]
        sc = lax.dot_general(q2, kk, dn, preferred_element_type=F32)
        for p in range(2):
            head = 4 * h + 2 * p + half
            logit = sc[p * Q_BLOCK:(p + 1) * Q_BLOCK] + bias_ref[head]
            if start_mask is not None:
                logit = logit + start_mask
            sink = sink_ref[head]
            m = jnp.maximum(jnp.max(logit, axis=-1, keepdims=True), sink)
            e = jnp.exp2(logit - m)
            den = jnp.sum(e, axis=-1, keepdims=True) + jnp.exp2(sink - m)
            probs[(p, half)] = e.astype(BF16)
            inv_den[(p, half)] = 1.0 / den
    lhs = jnp.concatenate(
        [jnp.concatenate([probs[(p, 0)], probs[(p, 1)]], axis=1) for p in range(2)], axis=0)
    rhs = jnp.concatenate([vv_ref[2 * h, pl.ds(q0, K_BLOCK), :],
                           vv_ref[2 * h + 1, pl.ds(q0, K_BLOCK), :]], axis=0)
    out = jnp.dot(lhs, rhs, preferred_element_type=F32)
    for p in range(2):
        scale = jnp.where(lo, inv_den[(p, 0)], inv_den[(p, 1)])
        acc_ref[pl.ds(q0, Q_BLOCK), pl.ds((2 * h + p) * LANES, LANES)] = (
            out[p * Q_BLOCK:(p + 1) * Q_BLOCK] * scale)


def _lru_slab(c, x, gate, prev, h_in, cw_ref, cb_ref, wg_ref, bg_ref, lam_ref, a_ref, u_ref, ext_ref):
    t = x.shape[0]
    lanes = slice(c * LANES, (c + 1) * LANES)
    i_lanes = slice(LRU_WIDTH + c * LANES, LRU_WIDTH + (c + 1) * LANES)

    ext_ref[c, :SUBLANES, :] = prev
    ext_ref[c, SUBLANES:, :] = x
    xc = x * cw_ref[CONV_WIDTH - 1:CONV_WIDTH, lanes] + cb_ref[:, lanes]
    for k in range(1, CONV_WIDTH):
        xc = xc + (ext_ref[c, SUBLANES - k:SUBLANES - k + t, :]
                   * cw_ref[CONV_WIDTH - 1 - k:CONV_WIDTH - k, lanes])

    xcb = xc.astype(BF16)
    r = jax.nn.sigmoid(jnp.dot(xcb, wg_ref[lanes, lanes], preferred_element_type=F32) + bg_ref[:, lanes])
    i = jax.nn.sigmoid(jnp.dot(xcb, wg_ref[lanes, i_lanes], preferred_element_type=F32)
                       + bg_ref[:, i_lanes])
    neg_lam = -lam_ref[:, lanes]
    softplus = jnp.maximum(neg_lam, 0.0) + jnp.log1p(jnp.exp(-jnp.abs(neg_lam)))
    log_a = (-LRU_C) * r * softplus
    a = jnp.exp(log_a)
    var = jnp.tanh(-log_a) * (1.0 + a * a)
    h = jnp.where(var > 0.0, var * lax.rsqrt(var), 0.0) * (i * xc)

    def doubling(av, hv, index, length, axis):
        d = 1
        while d < length:
            keep = index >= d
            a_sh = jnp.where(keep, pltpu.roll(av, d, axis), 1.0)
            h_sh = jnp.where(keep, pltpu.roll(hv, d, axis), 0.0)
            hv = av * h_sh + hv
            av = av * a_sh
            d *= 2
        return av, hv

    groups = t // SUBLANES
    grouped = (groups, SUBLANES, LANES)
    sub = lax.broadcasted_iota(jnp.int32, (1, SUBLANES, 1), 1)
    a, h = doubling(a.reshape(grouped), h.reshape(grouped), sub, SUBLANES, 1)
    a = a.reshape(t, LANES)
    h = h.reshape(t, LANES)
    grow = lax.broadcasted_iota(jnp.int32, (groups, 1), 0)

    def group_last(ref, value):
        ref[c] = value
        return ref[c, pl.ds(SUBLANES - 1, groups, stride=SUBLANES), :]

    ag, hg = doubling(group_last(a_ref, a), group_last(u_ref, h), grow, groups, 0)
    state = hg + ag * h_in
    enter = jnp.where(grow >= 1, pltpu.roll(state, 1, 0), h_in)
    enter = jnp.broadcast_to(enter[:, None, :], grouped).reshape(t, LANES)
    h = h + a * enter
    return jax.nn.gelu(gate) * h, state[groups - 1:groups, :]


def _mixer_kernel(sink_ref, x_ref, w_ref, cw_ref, cb_ref, wg_ref, bg_ref, lam_ref, gl_ref, bias_ref, ga_ref,
                  ya_ref, yl_ref,
                  q0_ref, q1_ref, z0_ref, z1_ref, kk_ref, vv_ref, acc_ref,
                  tail_ref, h_ref, a_ref, u_ref, ext_ref, y_ref, *, tiles_per_seq):
    s = pl.program_id(0)

    @pl.when(s == 0)
    def _():
        for ref in (q1_ref, z1_ref, kk_ref, vv_ref, tail_ref, h_ref):
            ref[...] = jnp.zeros_like(ref)

    def step(q_new_ref, z_new_ref, q_old_ref, z_old_ref):
        xb = x_ref[...].astype(BF16)
        seq_start = lax.rem(s - 1, tiles_per_seq) == 0

        def inproj_chunk(j):
            cols = slice(j * MXU_TILE, (j + 1) * MXU_TILE)
            z = jnp.dot(xb, w_ref[:, cols], preferred_element_type=F32)
            if cols.stop <= QKV_COLS:
                q_new_ref[:, cols] = z.astype(BF16)
            else:
                z_new_ref[:, cols.start - QKV_COLS:cols.stop - QKV_COLS] = z

        _attn_fill(kk_ref, vv_ref, q_old_ref)
        attn_units = [(sub, h) for sub in range(ATTN_TILE // Q_BLOCK) for h in range(N_KV_HEADS)]
        slabs = LRU_WIDTH // LANES
        chunks_per_slab = (QKV_COLS + 2 * LRU_WIDTH) // MXU_TILE // slabs
        units_per_slab = len(attn_units) // slabs
        sq = None
        for c in range(slabs):
            lanes = slice(c * LANES, (c + 1) * LANES)
            inproj_chunk(chunks_per_slab * c)
            x = z_old_ref[:, lanes]
            prev = jnp.where(seq_start, 0.0, tail_ref[:, lanes])
            h_in = jnp.where(seq_start, 0.0, h_ref[:, lanes])
            y, h_out = _lru_slab(c, x, z_old_ref[:, LRU_WIDTH + c * LANES:LRU_WIDTH + (c + 1) * LANES],
                                 prev, h_in, cw_ref, cb_ref, wg_ref, bg_ref, lam_ref, a_ref, u_ref, ext_ref)
            y_ref[:, lanes] = y
            part = jnp.sum(y * y, axis=-1, keepdims=True)
            sq = part if sq is None else sq + part
            h_ref[:, lanes] = h_out
            tail_ref[:, lanes] = x[LRU_TILE - SUBLANES:, :]
            for k in range(max(chunks_per_slab - 1, units_per_slab)):
                if k < units_per_slab:
                    sub, h = attn_units[units_per_slab * c + k]
                    _attn_unit(sub, h, seq_start, q_old_ref, kk_ref, vv_ref, bias_ref, sink_ref, acc_ref)
                if k < chunks_per_slab - 1:
                    inproj_chunk(chunks_per_slab * c + 1 + k)
        scale = lax.rsqrt(sq * (1.0 / LRU_WIDTH) + RMS_EPS)
        yl_ref[...] = (y_ref[...] * scale * gl_ref[...]).astype(BF16)
        ya_ref[...] = _rms_norm(acc_ref[...], ga_ref[...]).astype(BF16)

    @pl.when(s % 2 == 0)
    def _():
        step(q0_ref, z0_ref, q1_ref, z1_ref)

    @pl.when(s % 2 == 1)
    def _():
        step(q1_ref, z1_ref, q0_ref, z0_ref)


def _mixer(x2d, w, sinks, bias2, g_attn, conv_w, conv_b, w_gates, b_gates, lam, g_lru, seq, casts=()):
    n = x2d.shape[0]
    assert ATTN_TILE == LRU_TILE
    tiles = n // LRU_TILE
    last = tiles - 1
    cast_specs, cast_shapes = _cast_specs(casts, lambda s: s, tiles + 1)
    out_tile = lambda width: pl.BlockSpec((LRU_TILE, width), lambda s: (jnp.maximum(s - 1, 0), 0))
    return pl.pallas_call(
        _with_casts(functools.partial(_mixer_kernel, tiles_per_seq=seq // LRU_TILE), 11, 2, len(casts)),
        grid=(tiles + 1,),
        in_specs=[pl.BlockSpec(memory_space=pltpu.SMEM),
                  pl.BlockSpec((LRU_TILE, D_MODEL), lambda s: (jnp.minimum(s, last), 0)),
                  _resident((D_MODEL, QKV_COLS + 2 * LRU_WIDTH)),
                  _resident((CONV_WIDTH, LRU_WIDTH)),
                  _resident((1, LRU_WIDTH)),
                  _resident((LRU_WIDTH, 2 * LRU_WIDTH)),
                  _resident((1, 2 * LRU_WIDTH)),
                  _resident((1, LRU_WIDTH)),
                  _resident((1, LRU_WIDTH)),
                  _resident((N_Q_HEADS, Q_BLOCK, K_BLOCK)),
                  _resident((1, ATTN_WIDTH))] + cast_specs,
        out_specs=[out_tile(ATTN_WIDTH), out_tile(LRU_WIDTH)] + cast_specs,
        out_shape=[jax.ShapeDtypeStruct((n, ATTN_WIDTH), BF16),
                   jax.ShapeDtypeStruct((n, LRU_WIDTH), BF16)] + cast_shapes,
        scratch_shapes=[pltpu.VMEM((ATTN_TILE, QKV_COLS), BF16),
                        pltpu.VMEM((ATTN_TILE, QKV_COLS), BF16),
                        pltpu.VMEM((LRU_TILE, 2 * LRU_WIDTH), F32),
                        pltpu.VMEM((LRU_TILE, 2 * LRU_WIDTH), F32),
                        pltpu.VMEM((4, ATTN_HIST + ATTN_TILE, LANES), BF16),
                        pltpu.VMEM((4, ATTN_HIST + ATTN_TILE, LANES), BF16),
                        pltpu.VMEM((ATTN_TILE, ATTN_WIDTH), F32),
                        pltpu.VMEM((SUBLANES, LRU_WIDTH), F32),
                        pltpu.VMEM((1, LRU_WIDTH), F32),
                        pltpu.VMEM((LRU_WIDTH // LANES, LRU_TILE, LANES), F32),
                        pltpu.VMEM((LRU_WIDTH // LANES, LRU_TILE, LANES), F32),
                        pltpu.VMEM((LRU_WIDTH // LANES, SUBLANES + LRU_TILE, LANES), F32),
                        pltpu.VMEM((LRU_TILE, LRU_WIDTH), F32)],
        compiler_params=_params(("arbitrary",)),
        name="mixer",
    )(sinks, x2d, w, conv_w, conv_b, w_gates, b_gates, lam, g_lru, bias2, g_attn, *casts)


def _mix_ln1(ya_ref, yl_ref, x_ref, wo_ref, g1_ref, b1_ref, alpha, rows=slice(None)):
    y = jnp.dot(ya_ref[rows, :], wo_ref[:ATTN_WIDTH, :], preferred_element_type=F32)
    y = y + jnp.dot(yl_ref[rows, :], wo_ref[ATTN_WIDTH:, :], preferred_element_type=F32)
    return _layer_norm(alpha * x_ref[rows, :] + y, g1_ref[...], b1_ref[...])


def _dense_block_kernel(ya_ref, yl_ref, x_ref, wo_ref, g1_ref, b1_ref, wg_ref, wu_ref, wd_ref,
                        g2_ref, b2_ref, o_ref, *, alpha, f_splits):
    x1 = _mix_ln1(ya_ref, yl_ref, x_ref, wo_ref, g1_ref, b1_ref, alpha)
    xb = x1.astype(BF16)
    f = None
    for lo, hi in zip(f_splits[:-1], f_splits[1:]):
        cols = slice(lo, hi)
        g = jnp.dot(xb, wg_ref[:, cols], preferred_element_type=F32)
        u = jnp.dot(xb, wu_ref[:, cols], preferred_element_type=F32)
        hid = (jax.nn.silu(g) * u).astype(BF16)
        part = jnp.dot(hid, wd_ref[cols, :], preferred_element_type=F32)
        f = part if f is None else f + part
    o_ref[...] = _layer_norm(alpha * x1 + f, g2_ref[...], b2_ref[...])


def _dense_block(ya, yl, x2d, wo, g1, b1, wg, wu, wd, g2, b2, alpha):
    n = x2d.shape[0]
    d_ff = wg.shape[1]
    tile = lambda w: pl.BlockSpec((DENSE_TILE, w), lambda i: (i, 0))
    mxu_tiles = pl.cdiv(d_ff, MXU_TILE)
    f_splits = (0, (mxu_tiles + 1) // 2 * MXU_TILE, d_ff)
    return pl.pallas_call(
        functools.partial(_dense_block_kernel, alpha=alpha, f_splits=f_splits),
        grid=(n // DENSE_TILE,),
        in_specs=[tile(ATTN_WIDTH), tile(LRU_WIDTH), tile(D_MODEL),
                  _resident((D_MODEL, D_MODEL)), _resident((1, D_MODEL)), _resident((1, D_MODEL)),
                  _resident((D_MODEL, d_ff)), _resident((D_MODEL, d_ff)), _resident((d_ff, D_MODEL)),
                  _resident((1, D_MODEL)), _resident((1, D_MODEL))],
        out_specs=tile(D_MODEL),
        out_shape=jax.ShapeDtypeStruct((n, D_MODEL), F32),
        compiler_params=_params(("parallel",)),
        name="dense_block",
    )(ya, yl, x2d, wo, g1, b1, wg, wu, wd, g2, b2)


_L_IDX, _L_W, _L_RANK = 0, 2, 4


def _router_kernel(ya_ref, yl_ref, x_ref, wo_ref, g1_ref, b1_ref, rw_ref,
                   x1_ref, x1p_ref, route_ref, meta_ref, count_ref, carry_ref, *, alpha):
    @pl.when(pl.program_id(0) == 0)
    def _():
        carry_ref[...] = jnp.zeros_like(carry_ref)

    t = x_ref.shape[0] // ROUTER_SLABS
    carry = carry_ref[:, :1]
    for s in range(ROUTER_SLABS):
        carry = _route_slab(s * t, t, carry, ya_ref, yl_ref, x_ref, wo_ref, g1_ref, b1_ref, rw_ref,
                            x1_ref, x1p_ref, route_ref, meta_ref, alpha)
    carry_ref[...] = jnp.broadcast_to(carry, carry_ref.shape)
    count_ref[...] = jnp.broadcast_to(carry, count_ref.shape)


def _route_slab(first, t, carry, ya_ref, yl_ref, x_ref, wo_ref, g1_ref, b1_ref, rw_ref,
                x1_ref, x1p_ref, route_ref, meta_ref, alpha):
    x1 = _mix_ln1(ya_ref, yl_ref, x_ref, wo_ref, g1_ref, b1_ref, alpha, pl.ds(first, t))
    x1_ref[pl.ds(first, t), :] = x1
    _store_row_tiles(x1p_ref, x1, first)
    x_hi = x1.astype(BF16)
    x_lo = (x1 - x_hi.astype(F32)).astype(BF16)
    both = jnp.dot(x_hi, rw_ref[...], preferred_element_type=F32)
    logits = both[:, :LANES] + both[:, LANES:] + jnp.dot(x_lo, rw_ref[:, :LANES], preferred_element_type=F32)

    lt = logits.T[:N_EXPERTS]
    row = lax.broadcasted_iota(jnp.int32, (N_EXPERTS, t), 0).astype(F32)
    lowest = jnp.finfo(F32).min
    m1 = jnp.max(lt, axis=0, keepdims=True)
    i1 = jnp.min(jnp.where(lt == m1, row, float(N_EXPERTS)), axis=0, keepdims=True)
    l2 = jnp.where(row == i1, lowest, lt)
    m2 = jnp.max(l2, axis=0, keepdims=True)
    i2 = jnp.min(jnp.where(l2 == m2, row, float(N_EXPERTS)), axis=0, keepdims=True)
    e = jnp.exp(m2 - m1)
    w1 = 1.0 / (1.0 + e)
    w2 = e / (1.0 + e)

    sel = jnp.where(jnp.logical_or(row == i1, row == i2), 1.0, 0.0)
    r_i = lax.broadcasted_iota(jnp.int32, (t, t), 0)
    c_i = lax.broadcasted_iota(jnp.int32, (t, t), 1)
    earlier = jnp.where(r_i < c_i, 1.0, 0.0).astype(BF16)
    before = jnp.dot(sel.astype(BF16), earlier, preferred_element_type=F32) + carry
    rank1 = jnp.sum(jnp.where(row == i1, before, 0.0), axis=0, keepdims=True)
    rank2 = jnp.sum(jnp.where(row == i2, before, 0.0), axis=0, keepdims=True)

    meta = jnp.where(row == _L_IDX, i1, 0.0)
    meta = jnp.where(row == _L_IDX + 1, i2, meta)
    meta = jnp.where(row == _L_W, w1, meta)
    meta = jnp.where(row == _L_W + 1, w2, meta)
    meta = jnp.where(row == _L_RANK, rank1, meta)
    meta = jnp.where(row == _L_RANK + 1, rank2, meta)
    meta_ref[:, pl.ds(first, t)] = meta
    padded = jnp.concatenate([meta, jnp.zeros((LANES - N_EXPERTS, t), F32)], axis=0)
    route_ref[pl.ds(first, t), :] = padded.T
    return carry + jnp.sum(sel, axis=1, keepdims=True)


def _router_block(ya, yl, x2d, wo, g1, b1, router_w, alpha):
    n = x2d.shape[0]
    assert router_w.shape[1] == N_EXPERTS == SUBLANES
    rw = jnp.pad(router_w, ((0, 0), (0, LANES - N_EXPERTS)))
    w_hi = rw.astype(BF16)
    w_lo = (rw - w_hi.astype(F32)).astype(BF16)
    tile = lambda w: pl.BlockSpec((TOKEN_TILE, w), lambda i: (i, 0))
    return pl.pallas_call(
        functools.partial(_router_kernel, alpha=alpha),
        grid=(n // TOKEN_TILE,),
        in_specs=[tile(ATTN_WIDTH), tile(LRU_WIDTH), tile(D_MODEL),
                  _resident((D_MODEL, D_MODEL)), _resident((1, D_MODEL)), _resident((1, D_MODEL)),
                  _resident((D_MODEL, 2 * LANES))],
        out_specs=[tile(D_MODEL), pl.BlockSpec((TOKEN_TILE * ROW_TILE, LANES), lambda i: (i, 0)), tile(LANES),
                   pl.BlockSpec((SUBLANES, TOKEN_TILE), lambda i: (0, i)),
                   pl.BlockSpec((N_EXPERTS, LANES), lambda i: (0, 0))],
        out_shape=[jax.ShapeDtypeStruct((n, D_MODEL), F32),
                   jax.ShapeDtypeStruct((n * ROW_TILE, LANES), U32),
                   jax.ShapeDtypeStruct((n, LANES), F32),
                   jax.ShapeDtypeStruct((SUBLANES, n), F32),
                   jax.ShapeDtypeStruct((N_EXPERTS, LANES), F32)],
        scratch_shapes=[pltpu.VMEM((N_EXPERTS, LANES), F32)],
        compiler_params=_params(("arbitrary",)),
        name="router_block",
    )(ya, yl, x2d, wo, g1, b1, jnp.concatenate([w_hi, w_lo], axis=1))


def _tile_copy(src_ref, src_row, dst_ref, dst_row, sem):
    src = src_ref.at[pl.ds(pl.multiple_of(src_row * ROW_TILE, ROW_TILE), ROW_TILE)]
    dst = dst_ref.at[pl.ds(pl.multiple_of(dst_row * ROW_TILE, ROW_TILE), ROW_TILE)]
    return pltpu.make_async_copy(src, dst, sem)


def _dispatch_kernel(pad_start_ref, pad_len_ref, pos0_ref, pos1_ref, x_ref, xs_ref, sem, pad_sem):
    def issue(t, carry):
        _tile_copy(x_ref, t, xs_ref, pos0_ref[t], sem).start(priority=0)
        _tile_copy(x_ref, t, xs_ref, pos1_ref[t], sem).start(priority=1)
        return carry

    lax.fori_loop(0, DISPATCH_TILE, issue, 0, unroll=DMA_UNROLL)
    for _ in range(2):
        pltpu.make_async_copy(x_ref, xs_ref.at[pl.ds(0, DISPATCH_TILE * ROW_TILE)], sem).wait()

    @pl.when(pl.program_id(0) == pl.num_programs(0) - 1)
    def _():
        for e in range(N_EXPERTS):
            def fill(r, carry):
                _tile_copy(x_ref, 0, xs_ref, pad_start_ref[e] + r, pad_sem).start()
                return carry

            def drain(r, carry):
                _tile_copy(x_ref, 0, xs_ref, 0, pad_sem).wait()
                return carry

            lax.fori_loop(0, pad_len_ref[e], fill, 0)
            lax.fori_loop(0, pad_len_ref[e], drain, 0)

        def idle_copy(r):
            first = pl.multiple_of((pad_start_ref[N_EXPERTS] + r * MOE_TILE) * ROW_TILE, MOE_TILE * ROW_TILE)
            return pltpu.make_async_copy(x_ref.at[pl.ds(0, MOE_TILE * ROW_TILE)],
                                         xs_ref.at[pl.ds(first, MOE_TILE * ROW_TILE)], pad_sem)

        def fill_idle(r, carry):
            idle_copy(r).start()
            return carry

        def drain_idle(r, carry):
            idle_copy(r).wait()
            return carry

        lax.fori_loop(0, pad_len_ref[N_EXPERTS], fill_idle, 0)
        lax.fori_loop(0, pad_len_ref[N_EXPERTS], drain_idle, 0)


def _dispatch(pad_start, pad_len, pos0, pos1, x1, sorted_rows):
    n = x1.shape[0] // ROW_TILE
    assert DISPATCH_TILE >= MOE_TILE
    smem_tile = pl.BlockSpec((DISPATCH_TILE,), lambda i, *_: (i,), memory_space=pltpu.SMEM)
    grid_spec = pltpu.PrefetchScalarGridSpec(
        num_scalar_prefetch=2,
        grid=(n // DISPATCH_TILE,),
        in_specs=[smem_tile, smem_tile,
                  pl.BlockSpec((DISPATCH_TILE * ROW_TILE, LANES), lambda i, *_: (i, 0))],
        out_specs=pl.BlockSpec(memory_space=pl.ANY),
        scratch_shapes=[pltpu.SemaphoreType.DMA(()), pltpu.SemaphoreType.DMA(())],
    )
    return pl.pallas_call(
        _dispatch_kernel,
        grid_spec=grid_spec,
        out_shape=jax.ShapeDtypeStruct((sorted_rows * ROW_TILE, LANES), U32),
        compiler_params=_params(("arbitrary",), disable_bounds_checks=True),
        name="moe_dispatch",
    )(pad_start, pad_len, pos0, pos1, x1)


def _experts_kernel(tile_expert_ref, n_used_ref, x_ref, wg_ref, wu_ref, wd_ref, y_ref, xb_ref, acc_ref):
    del tile_expert_ref
    i = pl.program_id(0)
    j = pl.program_id(1)
    last = pl.num_programs(1) - 1

    @pl.when(i < n_used_ref[0])
    def _():
        @pl.when(j == 0)
        def _():
            xb_ref[...] = _load_row_tiles(x_ref, MOE_TILE).astype(BF16)

        xb = xb_ref[...]
        g = jnp.dot(xb, wg_ref[...], preferred_element_type=F32)
        u = jnp.dot(xb, wu_ref[...], preferred_element_type=F32)
        hid = (jax.nn.silu(g) * u).astype(BF16)
        part = jnp.dot(hid, wd_ref[...], preferred_element_type=F32)

        @pl.when(j == 0)
        def _():
            acc_ref[...] = part

        @pl.when(jnp.logical_and(j > 0, j < last))
        def _():
            acc_ref[...] += part

        @pl.when(j == last)
        def _():
            _store_row_tiles(y_ref, acc_ref[...] + part)

    @pl.when(jnp.logical_and(i >= n_used_ref[0], j == 0))
    def _():
        y_ref[...] = jnp.zeros_like(y_ref)


def _experts(tile_expert, n_used, xs, wg, wu, wd):
    rows = xs.shape[0] // ROW_TILE
    d_ff = wg.shape[2]
    n_tiles = rows // MOE_TILE
    n_f = d_ff // MOE_F_TILE
    assert n_f >= 2

    def row_map(i, j, te, nu):
        return (jnp.minimum(i, nu[0] - 1), 0)

    def f_of(i, j, nu):
        return jnp.where(i < nu[0], j, n_f - 1)

    grid_spec = pltpu.PrefetchScalarGridSpec(
        num_scalar_prefetch=2,
        grid=(n_tiles, n_f),
        in_specs=[pl.BlockSpec((MOE_TILE * ROW_TILE, LANES), row_map),
                  pl.BlockSpec((None, D_MODEL, MOE_F_TILE), lambda i, j, te, nu: (te[i], 0, f_of(i, j, nu))),
                  pl.BlockSpec((None, D_MODEL, MOE_F_TILE), lambda i, j, te, nu: (te[i], 0, f_of(i, j, nu))),
                  pl.BlockSpec((None, MOE_F_TILE, D_MODEL), lambda i, j, te, nu: (te[i], f_of(i, j, nu), 0))],
        out_specs=pl.BlockSpec((MOE_TILE * ROW_TILE, LANES), lambda i, j, te, nu: (i, 0)),
        scratch_shapes=[pltpu.VMEM((MOE_TILE, D_MODEL), BF16), pltpu.VMEM((MOE_TILE, D_MODEL), F32)],
    )
    return pl.pallas_call(
        _experts_kernel,
        grid_spec=grid_spec,
        out_shape=jax.ShapeDtypeStruct((rows * ROW_TILE, LANES), U32),
        compiler_params=_params(("arbitrary", "arbitrary")),
        name="moe_experts",
    )(tile_expert, n_used, xs, wg, wu, wd)


def _combine_kernel(pos0_ref, pos1_ref, nxt0_ref, nxt1_ref, ys_ref, x1_ref, route_ref, g2_ref, b2_ref,
                    o_ref, buf_ref, sem, *, alpha):
    i = pl.program_id(0)
    slot = i % 2

    def issue(p0_ref, p1_ref, dst_slot):
        def body(t, carry):
            for k, p_ref in enumerate((p0_ref, p1_ref)):
                _tile_copy(ys_ref, p_ref[t], buf_ref.at[dst_slot, k], t, sem.at[dst_slot]).start(priority=k)
            return carry
        lax.fori_loop(0, COMBINE_TILE, body, 0, unroll=DMA_UNROLL)

    @pl.when(i == 0)
    def _():
        issue(pos0_ref, pos1_ref, 0)

    @pl.when(i + 1 < pl.num_programs(0))
    def _():
        issue(nxt0_ref, nxt1_ref, 1 - slot)

    for k in range(2):
        pltpu.make_async_copy(ys_ref.at[pl.ds(0, COMBINE_TILE * ROW_TILE)], buf_ref.at[slot, k],
                              sem.at[slot]).wait()
    route = route_ref[...]
    y0 = _load_row_tiles(buf_ref, COMBINE_TILE, (slot, 0))
    y1 = _load_row_tiles(buf_ref, COMBINE_TILE, (slot, 1))
    f = route[:, _L_W:_L_W + 1] * y0 + route[:, _L_W + 1:_L_W + 2] * y1
    o_ref[...] = _layer_norm(alpha * x1_ref[...] + f, g2_ref[...], b2_ref[...])


def _combine(pos0, pos1, ys, x1, route, g2, b2, alpha):
    n = x1.shape[0]
    steps = n // COMBINE_TILE
    tile = lambda w: pl.BlockSpec((COMBINE_TILE, w), lambda i: (i, 0))
    smem_cur = pl.BlockSpec((COMBINE_TILE,), lambda i: (i,), memory_space=pltpu.SMEM)
    smem_next = pl.BlockSpec((COMBINE_TILE,), lambda i: (jnp.minimum(i + 1, steps - 1),),
                             memory_space=pltpu.SMEM)
    return pl.pallas_call(
        functools.partial(_combine_kernel, alpha=alpha),
        grid=(steps,),
        in_specs=[smem_cur, smem_cur, smem_next, smem_next,
                  pl.BlockSpec(memory_space=pl.ANY),
                  tile(D_MODEL), tile(LANES), _resident((1, D_MODEL)), _resident((1, D_MODEL))],
        out_specs=tile(D_MODEL),
        out_shape=jax.ShapeDtypeStruct((n, D_MODEL), F32),
        scratch_shapes=[pltpu.VMEM((2, 2, COMBINE_TILE * ROW_TILE, LANES), U32),
                        pltpu.SemaphoreType.DMA((2,))],
        compiler_params=_params(("arbitrary",), disable_bounds_checks=True),
        name="moe_combine",
    )(pos0, pos1, pos0, pos1, ys, x1, route, g2, b2)


def _moe_block(x1, x1_rows, route, meta, counts, wg, wu, wd, g2, b2, alpha):
    n = route.shape[0]
    n_tiles = (2 * n) // MOE_TILE + N_EXPERTS
    count = counts[:, 0].astype(jnp.int32)
    tiles_per = (count + MOE_TILE - 1) // MOE_TILE
    tile_end = jnp.cumsum(tiles_per)
    group_start = (tile_end - tiles_per) * MOE_TILE
    n_used = tile_end[-1:]
    tile_id = jnp.minimum(jnp.arange(n_tiles, dtype=jnp.int32), n_used[0] - 1)
    tile_expert = jnp.sum((tile_id[:, None] >= tile_end[None, :]).astype(jnp.int32), axis=1)
    idx = meta[_L_IDX:_L_IDX + 2].astype(jnp.int32)
    rank = meta[_L_RANK:_L_RANK + 2].astype(jnp.int32)
    start = jnp.sum(jnp.where(idx[:, :, None] == jnp.arange(N_EXPERTS), group_start, 0), axis=-1)
    pos = start + rank

    pad_start = jnp.concatenate([group_start + count, n_used * MOE_TILE])
    pad_len = jnp.concatenate([tiles_per * MOE_TILE - count, n_tiles - n_used])
    xs = _dispatch(pad_start, pad_len, pos[0], pos[1], x1_rows, n_tiles * MOE_TILE)
    ys = _experts(tile_expert, n_used, xs, wg, wu, wd)
    return _combine(pos[0], pos[1], ys, x1, route, g2, b2, alpha)


def _t5_bucket(rel):
    half = N_BUCKETS // 2
    max_exact = half // 2
    ret = jnp.where(rel > 0, half, 0)
    n = jnp.abs(rel)
    large = max_exact + (jnp.log(jnp.maximum(n, 1).astype(F32) / max_exact)
                         / math.log(MAX_DISTANCE / max_exact) * (half - max_exact)).astype(jnp.int32)
    large = jnp.minimum(large, half - 1)
    return ret + jnp.where(n < max_exact, n, large)


def _block_bias(rel_bias):
    qi = jnp.arange(Q_BLOCK)[:, None]
    kj = jnp.arange(K_BLOCK)[None, :]
    rel = (kj - WINDOW_CHUNKS * CHUNK) - qi
    hit = _t5_bucket(rel)[None, :, :, None] == jnp.arange(N_BUCKETS)
    bias = jnp.sum(jnp.where(hit, rel_bias.T[:, None, None, :], 0.0), axis=-1)
    first = (kj // CHUNK) - (qi // CHUNK)
    visible = jnp.logical_and(first >= 0, first <= WINDOW_CHUNKS)
    return jnp.where(visible[None], bias, NEG_INF).astype(F32) * LOG2E


def _in_weights(w_in):
    q = w_in[:, :ATTN_WIDTH] * (HEAD_DIM ** -0.5 * LOG2E)
    k = w_in[:, ATTN_WIDTH:ATTN_WIDTH + KV_COLS]
    v = w_in[:, ATTN_WIDTH + KV_COLS:ATTN_WIDTH + 2 * KV_COLS]
    rest = w_in[:, ATTN_WIDTH + 2 * KV_COLS:]
    swap = lambda w: jnp.concatenate([w[:, HEAD_DIM:], w[:, :HEAD_DIM]], axis=1)
    return jnp.concatenate([q, k, swap(k), v, swap(v), rest], axis=1).astype(BF16)


def _block_diag(w):
    nb, d, e = w.shape
    eye = jnp.eye(nb, dtype=w.dtype)
    return (eye[:, None, :, None] * w[:, :, None, :]).reshape(nb * d, nb * e)


def kernel(x, rel_bias, w_in, attn_sinks, conv_w, conv_b, gate_r_w, gate_r_b, gate_i_w, gate_i_b,
           lru_lambda, norm_attn_g, norm_lru_g, w_out, ln1_g, ln1_b, ffn_w_gate, ffn_w_up, ffn_w_down,
           router_w, exp_w_gate, exp_w_up, exp_w_down, ln2_g, ln2_b):
    batch, seq, d = x.shape
    depth = w_in.shape[0]
    alpha = float((2 * depth) ** 0.25)
    bias2 = _block_bias(rel_bias)
    row = lambda v: v.reshape(1, -1)

    rides = {l: [] for l in range(depth)}
    flat = lambda w: w.reshape(-1, w.shape[-1])
    for l in range(depth):
        if l % 2 == 0:
            rides[l] += [ffn_w_gate[l // 2], ffn_w_up[l // 2], ffn_w_down[l // 2]]
        else:
            rides[l - 1] += [flat(exp_w_gate[l // 2]), flat(exp_w_up[l // 2])]
            rides[l] += [flat(exp_w_down[l // 2])]
    bf16_weights = {}

    h = x.reshape(batch * seq, d)
    for l in range(depth):
        w_gates = jnp.concatenate([_block_diag(gate_r_w[l]), _block_diag(gate_i_w[l])], axis=1).astype(BF16)
        b_gates = jnp.concatenate([gate_r_b[l], gate_i_b[l]]).reshape(1, -1)
        ya, yl, *bf16_weights[l] = _mixer(h, _in_weights(w_in[l]), attn_sinks[l] * LOG2E, bias2,
                                          row(norm_attn_g[l]), conv_w[l], row(conv_b[l]), w_gates, b_gates,
                                          row(lru_lambda[l]), row(norm_lru_g[l]), seq, rides[l])
        wo = w_out[l].astype(BF16)
        if l % 2 == 0:
            wg, wu, wd = bf16_weights[l][:3]
            h = _dense_block(ya, yl, h, wo, row(ln1_g[l]), row(ln1_b[l]), wg, wu, wd,
                             row(ln2_g[l]), row(ln2_b[l]), alpha)
        else:
            (wg, wu), wd = bf16_weights[l - 1][-2:], bf16_weights[l][-1]
            x1, x1_rows, route, meta, counts = _router_block(ya, yl, h, wo, row(ln1_g[l]), row(ln1_b[l]),
                                                             router_w[l // 2], alpha)
            h = _moe_block(x1, x1_rows, route, meta, counts, wg.reshape(exp_w_gate.shape[1:]),
                           wu.reshape(exp_w_up.shape[1:]), wd.reshape(exp_w_down.shape[1:]),
                           row(ln2_g[l]), row(ln2_b[l]), alpha)
    return h.reshape(batch, seq, d)
```

```python
import functools
import math

import jax
import jax.numpy as jnp
from jax import lax
from jax.experimental import pallas as pl
from jax.experimental.pallas import tpu as pltpu

F32 = jnp.float32
BF16 = jnp.bfloat16

D_MODEL = 1024
CHUNK = 64
HEAD_DIM = 64
N_Q_HEADS = 8
N_KV_HEADS = 2
ATTN_WIDTH = N_Q_HEADS * HEAD_DIM
KV_COLS = N_KV_HEADS * HEAD_DIM
WINDOW_CHUNKS = 2
LRU_WIDTH = 512
LRU_BLOCKS = 8
CONV_WIDTH = 4
LRU_C = 8.0
N_BUCKETS = 32
MAX_DISTANCE = 128
N_EXPERTS = 8
NEG_INF = -1e30
LN_EPS = 1e-5
RMS_EPS = 1e-6
LOG2E = math.log2(math.e)

LANES = 128
SUBLANES = 8
VMEM_LIMIT = 56 * 1024 * 1024

QKV_COLS = ATTN_WIDTH + 4 * KV_COLS
Q_BLOCK = 2 * CHUNK
K_BLOCK = (WINDOW_CHUNKS + 2) * CHUNK
ATTN_TILE = 512
LRU_TILE = 512
TOKEN_TILE = 1024
DENSE_TILE = 1024
ROUTER_SLABS = 4
MXU_TILE = 256
MOE_TILE = 512
MOE_F_TILE = 7 * MXU_TILE
DISPATCH_TILE = 4096
COMBINE_TILE = 512
DMA_UNROLL = 8


def _params(sem, **kw):
    return pltpu.CompilerParams(dimension_semantics=sem, vmem_limit_bytes=VMEM_LIMIT, **kw)


def _resident(shape):
    nd = len(shape)
    return pl.BlockSpec(shape, lambda *_: (0,) * nd, pipeline_mode=pl.Buffered(1))


def _layer_norm(x, g, b):
    mu = jnp.mean(x, axis=-1, keepdims=True)
    xc = x - mu
    var = jnp.mean(xc * xc, axis=-1, keepdims=True)
    return xc * lax.rsqrt(var + LN_EPS) * g + b


def _rms_norm(x, g):
    return x * lax.rsqrt(jnp.mean(x * x, axis=-1, keepdims=True) + RMS_EPS) * g


HALF = D_MODEL // 2
ROW_TILE = HALF // LANES
U32 = jnp.uint32
HIGH_HALF = 0xFFFF0000


def _load_row_tiles(ref, rows, lead=()):
    words = jnp.concatenate(
        [ref[lead + (pl.ds(c, rows, stride=ROW_TILE), slice(None))] for c in range(ROW_TILE)], axis=1)
    low = lax.bitcast_convert_type(lax.shift_left(words, U32(16)), F32)
    high = lax.bitcast_convert_type(jnp.bitwise_and(words, U32(HIGH_HALF)), F32)
    return jnp.concatenate([low, high], axis=1)


def _store_row_tiles(ref, value, first_row=0):
    bits = lax.bitcast_convert_type(value.astype(BF16).astype(F32), U32)
    words = jnp.bitwise_or(lax.shift_right_logical(bits[:, :HALF], U32(16)),
                           jnp.bitwise_and(bits[:, HALF:], U32(HIGH_HALF)))
    for c in range(ROW_TILE):
        ref[pl.ds(first_row * ROW_TILE + c, value.shape[0], stride=ROW_TILE), :] = (
            words[:, c * LANES:(c + 1) * LANES])


BF16_ROWS = 2 * SUBLANES


def _with_casts(kernel_fn, n_in, n_out, n_casts):
    def wrapped(*refs):
        ins, refs = refs[:n_in], refs[n_in:]
        srcs, refs = refs[:n_casts], refs[n_casts:]
        outs, refs = refs[:n_out], refs[n_out:]
        dsts, scratch = refs[:n_casts], refs[n_casts:]
        for src, dst in zip(srcs, dsts):
            dst[...] = src[...].astype(BF16)
        kernel_fn(*ins, *outs, *scratch)
    return wrapped


def _cast_specs(weights, linear_step, steps):
    specs, shapes = [], []
    for w in weights:
        rows, cols = w.shape
        per = next(p for p in range(BF16_ROWS, rows + 1, BF16_ROWS) if rows % p == 0 and p * steps >= rows)
        last = rows // per - 1
        specs.append(pl.BlockSpec((per, cols),
                                  lambda *g, last=last: (jnp.minimum(linear_step(*g), last), 0)))
        shapes.append(jax.ShapeDtypeStruct(w.shape, BF16))
    return specs, shapes


ATTN_HIST = Q_BLOCK
K_COL = ATTN_WIDTH


def _attn_fill(kk_ref, vv_ref, qkv_ref):
    lo = lax.broadcasted_iota(jnp.int32, (1, LANES), 1) < HEAD_DIM
    zero = jnp.zeros((), BF16)
    rows = slice(ATTN_HIST, ATTN_HIST + ATTN_TILE)
    for dst, col in ((kk_ref, K_COL), (vv_ref, K_COL + 2 * LANES)):
        for j in range(4):
            dst[j, :ATTN_HIST, :] = dst[j, ATTN_TILE:, :]
        a = qkv_ref[:, col:col + LANES]
        s = qkv_ref[:, col + LANES:col + 2 * LANES]
        dst[0, rows, :] = jnp.where(lo, a, zero)
        dst[1, rows, :] = jnp.where(lo, zero, s)
        dst[2, rows, :] = jnp.where(lo, s, zero)
        dst[3, rows, :] = jnp.where(lo, zero, a)


def _attn_unit(sub, h, seq_start, qkv_ref, kk_ref, vv_ref, bias_ref, sink_ref, acc_ref):
    q0 = sub * Q_BLOCK
    lo = lax.broadcasted_iota(jnp.int32, (1, LANES), 1) < HEAD_DIM
    dn = (((1,), (1,)), ((), ()))
    start_mask = None
    if sub == 0:
        col = lax.broadcasted_iota(jnp.int32, (1, K_BLOCK), 1)
        start_mask = jnp.where(jnp.logical_and(seq_start, col < ATTN_HIST), NEG_INF, 0.0).astype(F32)
    q2 = jnp.concatenate(
        [qkv_ref[pl.ds(q0, Q_BLOCK), pl.ds((2 * h + p) * LANES, LANES)] for p in range(2)], axis=0)
    probs = {}
    inv_den = {}
    for half in range(2):
        kk = kk_ref[2 * h + half, pl.ds(q0, K_BLOCK), :]
        sc = lax.dot_general(q2, kk, dn, preferred_element_type=F32)
        for p in range(2):
            head = 4 * h + 2 * p + half
            logit = sc[p * Q_BLOCK:(p + 1) * Q_BLOCK] + bias_ref[head]
            if start_mask is not None:
                logit = logit + start_mask
            sink = sink_ref[head]
            m = jnp.maximum(jnp.max(logit, axis=-1, keepdims=True), sink)
            e = jnp.exp2(logit - m)
            den = jnp.sum(e, axis=-1, keepdims=True) + jnp.exp2(sink - m)
            probs[(p, half)] = e.astype(BF16)
            inv_den[(p, half)] = 1.0 / den
    lhs = jnp.concatenate(
        [jnp.concatenate([probs[(p, 0)], probs[(p, 1)]], axis=1) for p in range(2)], axis=0)
    rhs = jnp.concatenate([vv_ref[2 * h, pl.ds(q0, K_BLOCK), :],
                           vv_ref[2 * h + 1, pl.ds(q0, K_BLOCK), :]], axis=0)
    out = jnp.dot(lhs, rhs, preferred_element_type=F32)
    for p in range(2):
        scale = jnp.where(lo, inv_den[(p, 0)], inv_den[(p, 1)])
        acc_ref[pl.ds(q0, Q_BLOCK), pl.ds((2 * h + p) * LANES, LANES)] = (
            out[p * Q_BLOCK:(p + 1) * Q_BLOCK] * scale)


def _lru_slab(c, x, gate, prev, h_in, cw_ref, cb_ref, wg_ref, bg_ref, lam_ref, a_ref, u_ref, ext_ref):
    t = x.shape[0]
    lanes = slice(c * LANES, (c + 1) * LANES)
    i_lanes = slice(LRU_WIDTH + c * LANES, LRU_WIDTH + (c + 1) * LANES)

    ext_ref[c, :SUBLANES, :] = prev
    ext_ref[c, SUBLANES:, :] = x
    xc = x * cw_ref[CONV_WIDTH - 1:CONV_WIDTH, lanes] + cb_ref[:, lanes]
    for k in range(1, CONV_WIDTH):
        xc = xc + (ext_ref[c, SUBLANES - k:SUBLANES - k + t, :]
                   * cw_ref[CONV_WIDTH - 1 - k:CONV_WIDTH - k, lanes])

    xcb = xc.astype(BF16)
    r = jax.nn.sigmoid(jnp.dot(xcb, wg_ref[lanes, lanes], preferred_element_type=F32) + bg_ref[:, lanes])
    i = jax.nn.sigmoid(jnp.dot(xcb, wg_ref[lanes, i_lanes], preferred_element_type=F32)
                       + bg_ref[:, i_lanes])
    neg_lam = -lam_ref[:, lanes]
    softplus = jnp.maximum(neg_lam, 0.0) + jnp.log1p(jnp.exp(-jnp.abs(neg_lam)))
    log_a = (-LRU_C) * r * softplus
    a = jnp.exp(log_a)
    var = jnp.tanh(-log_a) * (1.0 + a * a)
    h = jnp.where(var > 0.0, var * lax.rsqrt(var), 0.0) * (i * xc)

    def doubling(av, hv, index, length, axis):
        d = 1
        while d < length:
            keep = index >= d
            a_sh = jnp.where(keep, pltpu.roll(av, d, axis), 1.0)
            h_sh = jnp.where(keep, pltpu.roll(hv, d, axis), 0.0)
            hv = av * h_sh + hv
            av = av * a_sh
            d *= 2
        return av, hv

    groups = t // SUBLANES
    grouped = (groups, SUBLANES, LANES)
    sub = lax.broadcasted_iota(jnp.int32, (1, SUBLANES, 1), 1)
    a, h = doubling(a.reshape(grouped), h.reshape(grouped), sub, SUBLANES, 1)
    a = a.reshape(t, LANES)
    h = h.reshape(t, LANES)
    grow = lax.broadcasted_iota(jnp.int32, (groups, 1), 0)

    def group_last(ref, value):
        ref[c] = value
        return ref[c, pl.ds(SUBLANES - 1, groups, stride=SUBLANES), :]

    ag, hg = doubling(group_last(a_ref, a), group_last(u_ref, h), grow, groups, 0)
    state = hg + ag * h_in
    enter = jnp.where(grow >= 1, pltpu.roll(state, 1, 0), h_in)
    enter = jnp.broadcast_to(enter[:, None, :], grouped).reshape(t, LANES)
    h = h + a * enter
    return jax.nn.gelu(gate) * h, state[groups - 1:groups, :]


def _mixer_kernel(sink_ref, x_ref, w_ref, cw_ref, cb_ref, wg_ref, bg_ref, lam_ref, gl_ref, bias_ref, ga_ref,
                  ya_ref, yl_ref,
                  q0_ref, q1_ref, z0_ref, z1_ref, kk_ref, vv_ref, acc_ref,
                  tail_ref, h_ref, a_ref, u_ref, ext_ref, y_ref, *, tiles_per_seq):
    s = pl.program_id(0)

    @pl.when(s == 0)
    def _():
        for ref in (q1_ref, z1_ref, kk_ref, vv_ref, tail_ref, h_ref):
            ref[...] = jnp.zeros_like(ref)

    def step(q_new_ref, z_new_ref, q_old_ref, z_old_ref):
        xb = x_ref[...].astype(BF16)
        seq_start = lax.rem(s - 1, tiles_per_seq) == 0

        def inproj_chunk(j):
            cols = slice(j * MXU_TILE, (j + 1) * MXU_TILE)
            z = jnp.dot(xb, w_ref[:, cols], preferred_element_type=F32)
            if cols.stop <= QKV_COLS:
                q_new_ref[:, cols] = z.astype(BF16)
            else:
                z_new_ref[:, cols.start - QKV_COLS:cols.stop - QKV_COLS] = z

        _attn_fill(kk_ref, vv_ref, q_old_ref)
        attn_units = [(sub, h) for sub in range(ATTN_TILE // Q_BLOCK) for h in range(N_KV_HEADS)]
        slabs = LRU_WIDTH // LANES
        chunks_per_slab = (QKV_COLS + 2 * LRU_WIDTH) // MXU_TILE // slabs
        units_per_slab = len(attn_units) // slabs
        sq = None
        for c in range(slabs):
            lanes = slice(c * LANES, (c + 1) * LANES)
            inproj_chunk(chunks_per_slab * c)
            x = z_old_ref[:, lanes]
            prev = jnp.where(seq_start, 0.0, tail_ref[:, lanes])
            h_in = jnp.where(seq_start, 0.0, h_ref[:, lanes])
            y, h_out = _lru_slab(c, x, z_old_ref[:, LRU_WIDTH + c * LANES:LRU_WIDTH + (c + 1) * LANES],
                                 prev, h_in, cw_ref, cb_ref, wg_ref, bg_ref, lam_ref, a_ref, u_ref, ext_ref)
            y_ref[:, lanes] = y
            part = jnp.sum(y * y, axis=-1, keepdims=True)
            sq = part if sq is None else sq + part
            h_ref[:, lanes] = h_out
            tail_ref[:, lanes] = x[LRU_TILE - SUBLANES:, :]
            for k in range(max(chunks_per_slab - 1, units_per_slab)):
                if k < units_per_slab:
                    sub, h = attn_units[units_per_slab * c + k]
                    _attn_unit(sub, h, seq_start, q_old_ref, kk_ref, vv_ref, bias_ref, sink_ref, acc_ref)
                if k < chunks_per_slab - 1:
                    inproj_chunk(chunks_per_slab * c + 1 + k)
        scale = lax.rsqrt(sq * (1.0 / LRU_WIDTH) + RMS_EPS)
        yl_ref[...] = (y_ref[...] * scale * gl_ref[...]).astype(BF16)
        ya_ref[...] = _rms_norm(acc_ref[...], ga_ref[...]).astype(BF16)

    @pl.when(s % 2 == 0)
    def _():
        step(q0_ref, z0_ref, q1_ref, z1_ref)

    @pl.when(s % 2 == 1)
    def _():
        step(q1_ref, z1_ref, q0_ref, z0_ref)


def _mixer(x2d, w, sinks, bias2, g_attn, conv_w, conv_b, w_gates, b_gates, lam, g_lru, seq, casts=()):
    n = x2d.shape[0]
    assert ATTN_TILE == LRU_TILE
    tiles = n // LRU_TILE
    last = tiles - 1
    cast_specs, cast_shapes = _cast_specs(casts, lambda s: s, tiles + 1)
    out_tile = lambda width: pl.BlockSpec((LRU_TILE, width), lambda s: (jnp.maximum(s - 1, 0), 0))
    return pl.pallas_call(
        _with_casts(functools.partial(_mixer_kernel, tiles_per_seq=seq // LRU_TILE), 11, 2, len(casts)),
        grid=(tiles + 1,),
        in_specs=[pl.BlockSpec(memory_space=pltpu.SMEM),
                  pl.BlockSpec((LRU_TILE, D_MODEL), lambda s: (jnp.minimum(s, last), 0)),
                  _resident((D_MODEL, QKV_COLS + 2 * LRU_WIDTH)),
                  _resident((CONV_WIDTH, LRU_WIDTH)),
                  _resident((1, LRU_WIDTH)),
                  _resident((LRU_WIDTH, 2 * LRU_WIDTH)),
                  _resident((1, 2 * LRU_WIDTH)),
                  _resident((1, LRU_WIDTH)),
                  _resident((1, LRU_WIDTH)),
                  _resident((N_Q_HEADS, Q_BLOCK, K_BLOCK)),
                  _resident((1, ATTN_WIDTH))] + cast_specs,
        out_specs=[out_tile(ATTN_WIDTH), out_tile(LRU_WIDTH)] + cast_specs,
        out_shape=[jax.ShapeDtypeStruct((n, ATTN_WIDTH), BF16),
                   jax.ShapeDtypeStruct((n, LRU_WIDTH), BF16)] + cast_shapes,
        scratch_shapes=[pltpu.VMEM((ATTN_TILE, QKV_COLS), BF16),
                        pltpu.VMEM((ATTN_TILE, QKV_COLS), BF16),
                        pltpu.VMEM((LRU_TILE, 2 * LRU_WIDTH), F32),
                        pltpu.VMEM((LRU_TILE, 2 * LRU_WIDTH), F32),
                        pltpu.VMEM((4, ATTN_HIST + ATTN_TILE, LANES), BF16),
                        pltpu.VMEM((4, ATTN_HIST + ATTN_TILE, LANES), BF16),
                        pltpu.VMEM((ATTN_TILE, ATTN_WIDTH), F32),
                        pltpu.VMEM((SUBLANES, LRU_WIDTH), F32),
                        pltpu.VMEM((1, LRU_WIDTH), F32),
                        pltpu.VMEM((LRU_WIDTH // LANES, LRU_TILE, LANES), F32),
                        pltpu.VMEM((LRU_WIDTH // LANES, LRU_TILE, LANES), F32),
                        pltpu.VMEM((LRU_WIDTH // LANES, SUBLANES + LRU_TILE, LANES), F32),
                        pltpu.VMEM((LRU_TILE, LRU_WIDTH), F32)],
        compiler_params=_params(("arbitrary",)),
        name="mixer",
    )(sinks, x2d, w, conv_w, conv_b, w_gates, b_gates, lam, g_lru, bias2, g_attn, *casts)


def _mix_ln1(ya_ref, yl_ref, x_ref, wo_ref, g1_ref, b1_ref, alpha, rows=slice(None)):
    y = jnp.dot(ya_ref[rows, :], wo_ref[:ATTN_WIDTH, :], preferred_element_type=F32)
    y = y + jnp.dot(yl_ref[rows, :], wo_ref[ATTN_WIDTH:, :], preferred_element_type=F32)
    return _layer_norm(alpha * x_ref[rows, :] + y, g1_ref[...], b1_ref[...])


def _dense_block_kernel(ya_ref, yl_ref, x_ref, wo_ref, g1_ref, b1_ref, wg_ref, wu_ref, wd_ref,
                        g2_ref, b2_ref, o_ref, *, alpha, f_splits):
    x1 = _mix_ln1(ya_ref, yl_ref, x_ref, wo_ref, g1_ref, b1_ref, alpha)
    xb = x1.astype(BF16)
    f = None
    for lo, hi in zip(f_splits[:-1], f_splits[1:]):
        cols = slice(lo, hi)
        g = jnp.dot(xb, wg_ref[:, cols], preferred_element_type=F32)
        u = jnp.dot(xb, wu_ref[:, cols], preferred_element_type=F32)
        hid = (jax.nn.silu(g) * u).astype(BF16)
        part = jnp.dot(hid, wd_ref[cols, :], preferred_element_type=F32)
        f = part if f is None else f + part
    o_ref[...] = _layer_norm(alpha * x1 + f, g2_ref[...], b2_ref[...])


def _dense_block(ya, yl, x2d, wo, g1, b1, wg, wu, wd, g2, b2, alpha):
    n = x2d.shape[0]
    d_ff = wg.shape[1]
    tile = lambda w: pl.BlockSpec((DENSE_TILE, w), lambda i: (i, 0))
    mxu_tiles = pl.cdiv(d_ff, MXU_TILE)
    f_splits = (0, (mxu_tiles + 1) // 2 * MXU_TILE, d_ff)
    return pl.pallas_call(
        functools.partial(_dense_block_kernel, alpha=alpha, f_splits=f_splits),
        grid=(n // DENSE_TILE,),
        in_specs=[tile(ATTN_WIDTH), tile(LRU_WIDTH), tile(D_MODEL),
                  _resident((D_MODEL, D_MODEL)), _resident((1, D_MODEL)), _resident((1, D_MODEL)),
                  _resident((D_MODEL, d_ff)), _resident((D_MODEL, d_ff)), _resident((d_ff, D_MODEL)),
                  _resident((1, D_MODEL)), _resident((1, D_MODEL))],
        out_specs=tile(D_MODEL),
        out_shape=jax.ShapeDtypeStruct((n, D_MODEL), F32),
        compiler_params=_params(("parallel",)),
        name="dense_block",
    )(ya, yl, x2d, wo, g1, b1, wg, wu, wd, g2, b2)


_L_IDX, _L_W, _L_RANK = 0, 2, 4


def _router_kernel(ya_ref, yl_ref, x_ref, wo_ref, g1_ref, b1_ref, rw_ref,
                   x1_ref, x1p_ref, route_ref, meta_ref, count_ref, carry_ref, *, alpha):
    @pl.when(pl.program_id(0) == 0)
    def _():
        carry_ref[...] = jnp.zeros_like(carry_ref)

    t = x_ref.shape[0] // ROUTER_SLABS
    carry = carry_ref[:, :1]
    for s in range(ROUTER_SLABS):
        carry = _route_slab(s * t, t, carry, ya_ref, yl_ref, x_ref, wo_ref, g1_ref, b1_ref, rw_ref,
                            x1_ref, x1p_ref, route_ref, meta_ref, alpha)
    carry_ref[...] = jnp.broadcast_to(carry, carry_ref.shape)
    count_ref[...] = jnp.broadcast_to(carry, count_ref.shape)


def _route_slab(first, t, carry, ya_ref, yl_ref, x_ref, wo_ref, g1_ref, b1_ref, rw_ref,
                x1_ref, x1p_ref, route_ref, meta_ref, alpha):
    x1 = _mix_ln1(ya_ref, yl_ref, x_ref, wo_ref, g1_ref, b1_ref, alpha, pl.ds(first, t))
    x1_ref[pl.ds(first, t), :] = x1
    _store_row_tiles(x1p_ref, x1, first)
    x_hi = x1.astype(BF16)
    x_lo = (x1 - x_hi.astype(F32)).astype(BF16)
    both = jnp.dot(x_hi, rw_ref[...], preferred_element_type=F32)
    logits = both[:, :LANES] + both[:, LANES:] + jnp.dot(x_lo, rw_ref[:, :LANES], preferred_element_type=F32)

    lt = logits.T[:N_EXPERTS]
    row = lax.broadcasted_iota(jnp.int32, (N_EXPERTS, t), 0).astype(F32)
    lowest = jnp.finfo(F32).min
    m1 = jnp.max(lt, axis=0, keepdims=True)
    i1 = jnp.min(jnp.where(lt == m1, row, float(N_EXPERTS)), axis=0, keepdims=True)
    l2 = jnp.where(row == i1, lowest, lt)
    m2 = jnp.max(l2, axis=0, keepdims=True)
    i2 = jnp.min(jnp.where(l2 == m2, row, float(N_EXPERTS)), axis=0, keepdims=True)
    e = jnp.exp(m2 - m1)
    w1 = 1.0 / (1.0 + e)
    w2 = e / (1.0 + e)

    sel = jnp.where(jnp.logical_or(row == i1, row == i2), 1.0, 0.0)
    r_i = lax.broadcasted_iota(jnp.int32, (t, t), 0)
    c_i = lax.broadcasted_iota(jnp.int32, (t, t), 1)
    earlier = jnp.where(r_i < c_i, 1.0, 0.0).astype(BF16)
    before = jnp.dot(sel.astype(BF16), earlier, preferred_element_type=F32) + carry
    rank1 = jnp.sum(jnp.where(row == i1, before, 0.0), axis=0, keepdims=True)
    rank2 = jnp.sum(jnp.where(row == i2, before, 0.0), axis=0, keepdims=True)

    meta = jnp.where(row == _L_IDX, i1, 0.0)
    meta = jnp.where(row == _L_IDX + 1, i2, meta)
    meta = jnp.where(row == _L_W, w1, meta)
    meta = jnp.where(row == _L_W + 1, w2, meta)
    meta = jnp.where(row == _L_RANK, rank1, meta)
    meta = jnp.where(row == _L_RANK + 1, rank2, meta)
    meta_ref[:, pl.ds(first, t)] = meta
    padded = jnp.concatenate([meta, jnp.zeros((LANES - N_EXPERTS, t), F32)], axis=0)
    route_ref[pl.ds(first, t), :] = padded.T
    return carry + jnp.sum(sel, axis=1, keepdims=True)


def _router_block(ya, yl, x2d, wo, g1, b1, router_w, alpha):
    n = x2d.shape[0]
    assert router_w.shape[1] == N_EXPERTS == SUBLANES
    rw = jnp.pad(router_w, ((0, 0), (0, LANES - N_EXPERTS)))
    w_hi = rw.astype(BF16)
    w_lo = (rw - w_hi.astype(F32)).astype(BF16)
    tile = lambda w: pl.BlockSpec((TOKEN_TILE, w), lambda i: (i, 0))
    return pl.pallas_call(
        functools.partial(_router_kernel, alpha=alpha),
        grid=(n // TOKEN_TILE,),
        in_specs=[tile(ATTN_WIDTH), tile(LRU_WIDTH), tile(D_MODEL),
                  _resident((D_MODEL, D_MODEL)), _resident((1, D_MODEL)), _resident((1, D_MODEL)),
                  _resident((D_MODEL, 2 * LANES))],
        out_specs=[tile(D_MODEL), pl.BlockSpec((TOKEN_TILE * ROW_TILE, LANES), lambda i: (i, 0)), tile(LANES),
                   pl.BlockSpec((SUBLANES, TOKEN_TILE), lambda i: (0, i)),
                   pl.BlockSpec((N_EXPERTS, LANES), lambda i: (0, 0))],
        out_shape=[jax.ShapeDtypeStruct((n, D_MODEL), F32),
                   jax.ShapeDtypeStruct((n * ROW_TILE, LANES), U32),
                   jax.ShapeDtypeStruct((n, LANES), F32),
                   jax.ShapeDtypeStruct((SUBLANES, n), F32),
                   jax.ShapeDtypeStruct((N_EXPERTS, LANES), F32)],
        scratch_shapes=[pltpu.VMEM((N_EXPERTS, LANES), F32)],
        compiler_params=_params(("arbitrary",)),
        name="router_block",
    )(ya, yl, x2d, wo, g1, b1, jnp.concatenate([w_hi, w_lo], axis=1))


def _tile_copy(src_ref, src_row, dst_ref, dst_row, sem):
    src = src_ref.at[pl.ds(pl.multiple_of(src_row * ROW_TILE, ROW_TILE), ROW_TILE)]
    dst = dst_ref.at[pl.ds(pl.multiple_of(dst_row * ROW_TILE, ROW_TILE), ROW_TILE)]
    return pltpu.make_async_copy(src, dst, sem)


def _dispatch_kernel(pad_start_ref, pad_len_ref, pos0_ref, pos1_ref, x_ref, xs_ref, sem, pad_sem):
    def issue(t, carry):
        _tile_copy(x_ref, t, xs_ref, pos0_ref[t], sem).start(priority=0)
        _tile_copy(x_ref, t, xs_ref, pos1_ref[t], sem).start(priority=1)
        return carry

    lax.fori_loop(0, DISPATCH_TILE, issue, 0, unroll=DMA_UNROLL)
    for _ in range(2):
        pltpu.make_async_copy(x_ref, xs_ref.at[pl.ds(0, DISPATCH_TILE * ROW_TILE)], sem).wait()

    @pl.when(pl.program_id(0) == pl.num_programs(0) - 1)
    def _():
        for e in range(N_EXPERTS):
            def fill(r, carry):
                _tile_copy(x_ref, 0, xs_ref, pad_start_ref[e] + r, pad_sem).start()
                return carry

            def drain(r, carry):
                _tile_copy(x_ref, 0, xs_ref, 0, pad_sem).wait()
                return carry

            lax.fori_loop(0, pad_len_ref[e], fill, 0)
            lax.fori_loop(0, pad_len_ref[e], drain, 0)

        def idle_copy(r):
            first = pl.multiple_of((pad_start_ref[N_EXPERTS] + r * MOE_TILE) * ROW_TILE, MOE_TILE * ROW_TILE)
            return pltpu.make_async_copy(x_ref.at[pl.ds(0, MOE_TILE * ROW_TILE)],
                                         xs_ref.at[pl.ds(first, MOE_TILE * ROW_TILE)], pad_sem)

        def fill_idle(r, carry):
            idle_copy(r).start()
            return carry

        def drain_idle(r, carry):
            idle_copy(r).wait()
            return carry

        lax.fori_loop(0, pad_len_ref[N_EXPERTS], fill_idle, 0)
        lax.fori_loop(0, pad_len_ref[N_EXPERTS], drain_idle, 0)


def _dispatch(pad_start, pad_len, pos0, pos1, x1, sorted_rows):
    n = x1.shape[0] // ROW_TILE
    assert DISPATCH_TILE >= MOE_TILE
    smem_tile = pl.BlockSpec((DISPATCH_TILE,), lambda i, *_: (i,), memory_space=pltpu.SMEM)
    grid_spec = pltpu.PrefetchScalarGridSpec(
        num_scalar_prefetch=2,
        grid=(n // DISPATCH_TILE,),
        in_specs=[smem_tile, smem_tile,
                  pl.BlockSpec((DISPATCH_TILE * ROW_TILE, LANES), lambda i, *_: (i, 0))],
        out_specs=pl.BlockSpec(memory_space=pl.ANY),
        scratch_shapes=[pltpu.SemaphoreType.DMA(()), pltpu.SemaphoreType.DMA(())],
    )
    return pl.pallas_call(
        _dispatch_kernel,
        grid_spec=grid_spec,
        out_shape=jax.ShapeDtypeStruct((sorted_rows * ROW_TILE, LANES), U32),
        compiler_params=_params(("arbitrary",), disable_bounds_checks=True),
        name="moe_dispatch",
    )(pad_start, pad_len, pos0, pos1, x1)


def _experts_kernel(tile_expert_ref, n_used_ref, x_ref, wg_ref, wu_ref, wd_ref, y_ref, xb_ref, acc_ref):
    del tile_expert_ref
    i = pl.program_id(0)
    j = pl.program_id(1)
    last = pl.num_programs(1) - 1

    @pl.when(i < n_used_ref[0])
    def _():
        @pl.when(j == 0)
        def _():
            xb_ref[...] = _load_row_tiles(x_ref, MOE_TILE).astype(BF16)

        xb = xb_ref[...]
        g = jnp.dot(xb, wg_ref[...], preferred_element_type=F32)
        u = jnp.dot(xb, wu_ref[...], preferred_element_type=F32)
        hid = (jax.nn.silu(g) * u).astype(BF16)
        part = jnp.dot(hid, wd_ref[...], preferred_element_type=F32)

        @pl.when(j == 0)
        def _():
            acc_ref[...] = part

        @pl.when(jnp.logical_and(j > 0, j < last))
        def _():
            acc_ref[...] += part

        @pl.when(j == last)
        def _():
            _store_row_tiles(y_ref, acc_ref[...] + part)

    @pl.when(jnp.logical_and(i >= n_used_ref[0], j == 0))
    def _():
        y_ref[...] = jnp.zeros_like(y_ref)


def _experts(tile_expert, n_used, xs, wg, wu, wd):
    rows = xs.shape[0] // ROW_TILE
    d_ff = wg.shape[2]
    n_tiles = rows // MOE_TILE
    n_f = d_ff // MOE_F_TILE
    assert n_f >= 2

    def row_map(i, j, te, nu):
        return (jnp.minimum(i, nu[0] - 1), 0)

    def f_of(i, j, nu):
        return jnp.where(i < nu[0], j, n_f - 1)

    grid_spec = pltpu.PrefetchScalarGridSpec(
        num_scalar_prefetch=2,
        grid=(n_tiles, n_f),
        in_specs=[pl.BlockSpec((MOE_TILE * ROW_TILE, LANES), row_map),
                  pl.BlockSpec((None, D_MODEL, MOE_F_TILE), lambda i, j, te, nu: (te[i], 0, f_of(i, j, nu))),
                  pl.BlockSpec((None, D_MODEL, MOE_F_TILE), lambda i, j, te, nu: (te[i], 0, f_of(i, j, nu))),
                  pl.BlockSpec((None, MOE_F_TILE, D_MODEL), lambda i, j, te, nu: (te[i], f_of(i, j, nu), 0))],
        out_specs=pl.BlockSpec((MOE_TILE * ROW_TILE, LANES), lambda i, j, te, nu: (i, 0)),
        scratch_shapes=[pltpu.VMEM((MOE_TILE, D_MODEL), BF16), pltpu.VMEM((MOE_TILE, D_MODEL), F32)],
    )
    return pl.pallas_call(
        _experts_kernel,
        grid_spec=grid_spec,
        out_shape=jax.ShapeDtypeStruct((rows * ROW_TILE, LANES), U32),
        compiler_params=_params(("arbitrary", "arbitrary")),
        name="moe_experts",
    )(tile_expert, n_used, xs, wg, wu, wd)


def _combine_kernel(pos0_ref, pos1_ref, nxt0_ref, nxt1_ref, ys_ref, x1_ref, route_ref, g2_ref, b2_ref,
                    o_ref, buf_ref, sem, *, alpha):
    i = pl.program_id(0)
    slot = i % 2

    def issue(p0_ref, p1_ref, dst_slot):
        def body(t, carry):
            for k, p_ref in enumerate((p0_ref, p1_ref)):
                _tile_copy(ys_ref, p_ref[t], buf_ref.at[dst_slot, k], t, sem.at[dst_slot]).start(priority=k)
            return carry
        lax.fori_loop(0, COMBINE_TILE, body, 0, unroll=DMA_UNROLL)

    @pl.when(i == 0)
    def _():
        issue(pos0_ref, pos1_ref, 0)

    @pl.when(i + 1 < pl.num_programs(0))
    def _():
        issue(nxt0_ref, nxt1_ref, 1 - slot)

    for k in range(2):
        pltpu.make_async_copy(ys_ref.at[pl.ds(0, COMBINE_TILE * ROW_TILE)], buf_ref.at[slot, k],
                              sem.at[slot]).wait()
    route = route_ref[...]
    y0 = _load_row_tiles(buf_ref, COMBINE_TILE, (slot, 0))
    y1 = _load_row_tiles(buf_ref, COMBINE_TILE, (slot, 1))
    f = route[:, _L_W:_L_W + 1] * y0 + route[:, _L_W + 1:_L_W + 2] * y1
    o_ref[...] = _layer_norm(alpha * x1_ref[...] + f, g2_ref[...], b2_ref[...])


def _combine(pos0, pos1, ys, x1, route, g2, b2, alpha):
    n = x1.shape[0]
    steps = n // COMBINE_TILE
    tile = lambda w: pl.BlockSpec((COMBINE_TILE, w), lambda i: (i, 0))
    smem_cur = pl.BlockSpec((COMBINE_TILE,), lambda i: (i,), memory_space=pltpu.SMEM)
    smem_next = pl.BlockSpec((COMBINE_TILE,), lambda i: (jnp.minimum(i + 1, steps - 1),),
                             memory_space=pltpu.SMEM)
    return pl.pallas_call(
        functools.partial(_combine_kernel, alpha=alpha),
        grid=(steps,),
        in_specs=[smem_cur, smem_cur, smem_next, smem_next,
                  pl.BlockSpec(memory_space=pl.ANY),
                  tile(D_MODEL), tile(LANES), _resident((1, D_MODEL)), _resident((1, D_MODEL))],
        out_specs=tile(D_MODEL),
        out_shape=jax.ShapeDtypeStruct((n, D_MODEL), F32),
        scratch_shapes=[pltpu.VMEM((2, 2, COMBINE_TILE * ROW_TILE, LANES), U32),
                        pltpu.SemaphoreType.DMA((2,))],
        compiler_params=_params(("arbitrary",), disable_bounds_checks=True),
        name="moe_combine",
    )(pos0, pos1, pos0, pos1, ys, x1, route, g2, b2)


def _moe_block(x1, x1_rows, route, meta, counts, wg, wu, wd, g2, b2, alpha):
    n = route.shape[0]
    n_tiles = (2 * n) // MOE_TILE + N_EXPERTS
    count = counts[:, 0].astype(jnp.int32)
    tiles_per = (count + MOE_TILE - 1) // MOE_TILE
    tile_end = jnp.cumsum(tiles_per)
    group_start = (tile_end - tiles_per) * MOE_TILE
    n_used = tile_end[-1:]
    tile_id = jnp.minimum(jnp.arange(n_tiles, dtype=jnp.int32), n_used[0] - 1)
    tile_expert = jnp.sum((tile_id[:, None] >= tile_end[None, :]).astype(jnp.int32), axis=1)
    idx = meta[_L_IDX:_L_IDX + 2].astype(jnp.int32)
    rank = meta[_L_RANK:_L_RANK + 2].astype(jnp.int32)
    start = jnp.sum(jnp.where(idx[:, :, None] == jnp.arange(N_EXPERTS), group_start, 0), axis=-1)
    pos = start + rank

    pad_start = jnp.concatenate([group_start + count, n_used * MOE_TILE])
    pad_len = jnp.concatenate([tiles_per * MOE_TILE - count, n_tiles - n_used])
    xs = _dispatch(pad_start, pad_len, pos[0], pos[1], x1_rows, n_tiles * MOE_TILE)
    ys = _experts(tile_expert, n_used, xs, wg, wu, wd)
    return _combine(pos[0], pos[1], ys, x1, route, g2, b2, alpha)


def _t5_bucket(rel):
    half = N_BUCKETS // 2
    max_exact = half // 2
    ret = jnp.where(rel > 0, half, 0)
    n = jnp.abs(rel)
    large = max_exact + (jnp.log(jnp.maximum(n, 1).astype(F32) / max_exact)
                         / math.log(MAX_DISTANCE / max_exact) * (half - max_exact)).astype(jnp.int32)
    large = jnp.minimum(large, half - 1)
    return ret + jnp.where(n < max_exact, n, large)


def _block_bias(rel_bias):
    qi = jnp.arange(Q_BLOCK)[:, None]
    kj = jnp.arange(K_BLOCK)[None, :]
    rel = (kj - WINDOW_CHUNKS * CHUNK) - qi
    hit = _t5_bucket(rel)[None, :, :, None] == jnp.arange(N_BUCKETS)
    bias = jnp.sum(jnp.where(hit, rel_bias.T[:, None, None, :], 0.0), axis=-1)
    first = (kj // CHUNK) - (qi // CHUNK)
    visible = jnp.logical_and(first >= 0, first <= WINDOW_CHUNKS)
    return jnp.where(visible[None], bias, NEG_INF).astype(F32) * LOG2E


def _in_weights(w_in):
    q = w_in[:, :ATTN_WIDTH] * (HEAD_DIM ** -0.5 * LOG2E)
    k = w_in[:, ATTN_WIDTH:ATTN_WIDTH + KV_COLS]
    v = w_in[:, ATTN_WIDTH + KV_COLS:ATTN_WIDTH + 2 * KV_COLS]
    rest = w_in[:, ATTN_WIDTH + 2 * KV_COLS:]
    swap = lambda w: jnp.concatenate([w[:, HEAD_DIM:], w[:, :HEAD_DIM]], axis=1)
    return jnp.concatenate([q, k, swap(k), v, swap(v), rest], axis=1).astype(BF16)


def _block_diag(w):
    nb, d, e = w.shape
    eye = jnp.eye(nb, dtype=w.dtype)
    return (eye[:, None, :, None] * w[:, :, None, :]).reshape(nb * d, nb * e)


def kernel(x, rel_bias, w_in, attn_sinks, conv_w, conv_b, gate_r_w, gate_r_b, gate_i_w, gate_i_b,
           lru_lambda, norm_attn_g, norm_lru_g, w_out, ln1_g, ln1_b, ffn_w_gate, ffn_w_up, ffn_w_down,
           router_w, exp_w_gate, exp_w_up, exp_w_down, ln2_g, ln2_b):
    batch, seq, d = x.shape
    depth = w_in.shape[0]
    alpha = float((2 * depth) ** 0.25)
    bias2 = _block_bias(rel_bias)
    row = lambda v: v.reshape(1, -1)

    rides = {l: [] for l in range(depth)}
    flat = lambda w: w.reshape(-1, w.shape[-1])
    for l in range(depth):
        if l % 2 == 0:
            rides[l] += [ffn_w_gate[l // 2], ffn_w_up[l // 2], ffn_w_down[l // 2]]
        else:
            rides[l - 1] += [flat(exp_w_gate[l // 2]), flat(exp_w_up[l // 2])]
            rides[l] += [flat(exp_w_down[l // 2])]
    bf16_weights = {}

    h = x.reshape(batch * seq, d)
    for l in range(depth):
        w_gates = jnp.concatenate([_block_diag(gate_r_w[l]), _block_diag(gate_i_w[l])], axis=1).astype(BF16)
        b_gates = jnp.concatenate([gate_r_b[l], gate_i_b[l]]).reshape(1, -1)
        ya, yl, *bf16_weights[l] = _mixer(h, _in_weights(w_in[l]), attn_sinks[l] * LOG2E, bias2,
                                          row(norm_attn_g[l]), conv_w[l], row(conv_b[l]), w_gates, b_gates,
                                          row(lru_lambda[l]), row(norm_lru_g[l]), seq, rides[l])
        wo = w_out[l].astype(BF16)
        if l % 2 == 0:
            wg, wu, wd = bf16_weights[l][:3]
            h = _dense_block(ya, yl, h, wo, row(ln1_g[l]), row(ln1_b[l]), wg, wu, wd,
                             row(ln2_g[l]), row(ln2_b[l]), alpha)
        else:
            (wg, wu), wd = bf16_weights[l - 1][-2:], bf16_weights[l][-1]
            x1, x1_rows, route, meta, counts = _router_block(ya, yl, h, wo, row(ln1_g[l]), row(ln1_b[l]),
                                                             router_w[l // 2], alpha)
            h = _moe_block(x1, x1_rows, route, meta, counts, wg.reshape(exp_w_gate.shape[1:]),
                           wu.reshape(exp_w_up.shape[1:]), wd.reshape(exp_w_down.shape[1:]),
                           row(ln2_g[l]), row(ln2_b[l]), alpha)
    return h.reshape(batch, seq, d)
```
